```python
import math
import jax, jax.numpy as jnp
from jax import lax
import numpy as np

D_MODEL = 1024
BATCH = 32
SEQ = 2048
DEPTH = 2
DEC_BATCH = 4
DEC_SEQ = 4096
PAST_LEN = 128

D_MIX = D_MODEL
D_A = D_MIX // 2
S5_GROUP = 16
S5_GROUPS = D_A // S5_GROUP
S5_STATE = 64
DT_MIN = 1e-3
DT_MAX = 1e-1
D_B = D_MIX - D_A
HEAD_DIM = 64
N_HEADS_B = D_B // HEAD_DIM
DILATED_CONFIGS = ((128, 1), (512, 4), (2048, 16))
ATTN_BLOCK = 64
NEG_BIG = -1e30
D_C = D_MIX // 2
FNET_GROUPS = 4
FNET_GROUP_DIM = D_C // FNET_GROUPS
D_D = D_MIX - D_C
CONV_W = 3
D_FF = 2816
EPS = 1e-6
N_AB = (DEPTH + 1) // 2
N_CD = DEPTH // 2

kernel_name = "hybrid_bidir_s5_dilattn_fnet_shortconv_encoder"


def _rms_norm(x, g):
    xf = x.astype(jnp.float32)
    y = xf * lax.rsqrt(jnp.mean(xf * xf, axis=-1, keepdims=True) + EPS)
    return (y * g.astype(jnp.float32)).astype(x.dtype)


def _modulate(x, shift, scale):
    return x * (1.0 + scale[:, None, :]) + shift[:, None, :]


def _dwconv_centred(x, w):
    pad = CONV_W // 2
    T = x.shape[1]
    xp = jnp.pad(x, ((0, 0), (pad, pad), (0, 0)))
    y = xp[:, 0:T] * w[0]
    for j in range(1, CONV_W):
        y = y + xp[:, j:j + T] * w[j]
    return y


def _s5_discretize(lam_re, lam_im, log_dt, b_re, b_im):
    lam_re, lam_im, log_dt, b_re, b_im = (a.astype(jnp.float32) for a in (lam_re, lam_im, log_dt, b_re, b_im))
    dt = jnp.exp(log_dt)[:, None]
    mag = jnp.exp(lam_re * dt)
    a_re = mag * jnp.cos(lam_im * dt)
    a_im = mag * jnp.sin(lam_im * dt)
    den = lam_re * lam_re + lam_im * lam_im
    coef_re = ((a_re - 1.0) * lam_re + a_im * lam_im) / den
    coef_im = (a_im * lam_re - (a_re - 1.0) * lam_im) / den
    bb_re = coef_re[..., None] * b_re - coef_im[..., None] * b_im
    bb_im = coef_re[..., None] * b_im + coef_im[..., None] * b_re
    return a_re, a_im, bb_re, bb_im


def _diag_linear_op(e1, e2):
    a1r, a1i, b1r, b1i = e1
    a2r, a2i, b2r, b2i = e2
    ar = a2r * a1r - a2i * a1i
    ai = a2r * a1i + a2i * a1r
    br = a2r * b1r - a2i * b1i + b2r
    bi = a2r * b1i + a2i * b1r + b2i
    return ar, ai, br, bi


def _s5_mixer(u, lam_re_f, lam_im_f, log_dt_f, lam_re_b, lam_im_b, log_dt_b,
              b_re, b_im, c_re, c_im, d_skip, w_glu, b_glu):
    Bsz, T, _ = u.shape
    ug = u.astype(jnp.float32).reshape(Bsz, T, S5_GROUPS, S5_GROUP)
    fwd = _s5_discretize(lam_re_f, lam_im_f, log_dt_f, b_re, b_im)
    bwd = _s5_discretize(lam_re_b, lam_im_b, log_dt_b, b_re, b_im)
    cr = c_re.astype(jnp.float32)
    ci = c_im.astype(jnp.float32)

    def one_sequence(us):
        y = jnp.zeros(us.shape, jnp.float32)
        for (a_re, a_im, bb_re, bb_im), rev in ((fwd, False), (bwd, True)):
            bu_re = jnp.einsum("tgc,gpc->tgp", us, bb_re)
            bu_im = jnp.einsum("tgc,gpc->tgp", us, bb_im)
            a_re_t = jnp.broadcast_to(a_re, bu_re.shape)
            a_im_t = jnp.broadcast_to(a_im, bu_re.shape)
            _, _, s_re, s_im = lax.associative_scan(
                _diag_linear_op, (a_re_t, a_im_t, bu_re, bu_im), reverse=rev, axis=0)
            y = y + jnp.einsum("tgp,gcp->tgc", s_re, cr) - jnp.einsum("tgp,gcp->tgc", s_im, ci)
        return y

    y = lax.map(one_sequence, ug)
    y = y + ug * d_skip.astype(jnp.float32).reshape(S5_GROUPS, S5_GROUP)
    y = jax.nn.gelu(y.reshape(Bsz, T, D_A))
    y = y * jax.nn.sigmoid(y @ w_glu.astype(jnp.float32) + b_glu.astype(jnp.float32))
    return y.astype(u.dtype)


def _alibi_slopes(n):
    return jnp.power(2.0, -8.0 * jnp.arange(1, n + 1, dtype=jnp.float32) / n)


def _dilated_branch(q, k, v, slopes, window, dilation):
    Bsz, T, H, hd = q.shape
    R = window // (2 * dilation)
    L = T // dilation
    QB = ATTN_BLOCK
    nb = -(-L // QB)
    halo = -(-R // QB)
    Lp = nb * QB
    KW = (2 * halo + 1) * QB

    def to_sub(a):
        return a.reshape(Bsz, L, dilation, H, hd).transpose(0, 2, 1, 3, 4)

    qs, ks, vs = to_sub(q), to_sub(k), to_sub(v)
    qb = jnp.pad(qs, ((0, 0), (0, 0), (0, Lp - L), (0, 0), (0, 0))).reshape(Bsz, dilation, nb, QB, H, hd)
    kv_pad = ((0, 0), (0, 0), (halo * QB, Lp - L + halo * QB), (0, 0), (0, 0))
    kp = jnp.pad(ks, kv_pad).reshape(Bsz, dilation, nb + 2 * halo, QB, H, hd)
    vp = jnp.pad(vs, kv_pad).reshape(Bsz, dilation, nb + 2 * halo, QB, H, hd)
    kb = jnp.concatenate([kp[:, :, j:j + nb] for j in range(2 * halo + 1)], axis=3)
    vb = jnp.concatenate([vp[:, :, j:j + nb] for j in range(2 * halo + 1)], axis=3)

    s = jnp.einsum("bdnqhe,bdnkhe->bdnhqk", qb, kb,
                   preferred_element_type=jnp.float32) * (1.0 / math.sqrt(hd))
    qi = jnp.arange(QB)[:, None]
    rel = (jnp.arange(KW)[None, :] - halo * QB) - qi
    kpos = (jnp.arange(nb) * QB)[:, None, None] + qi[None] + rel[None]
    valid = (jnp.abs(rel) <= R)[None] & (kpos >= 0) & (kpos < L)
    dist = jnp.abs(rel).astype(jnp.float32) * dilation
    alibi = -slopes[:, None, None] * dist[None]
    s = jnp.where(valid[:, None], s + alibi, NEG_BIG)
    m = jnp.max(s, axis=-1, keepdims=True)
    p = jnp.exp(s - m)
    l = jnp.sum(p, axis=-1)
    o = jnp.einsum("bdnhqk,bdnkhe->bdnqhe", p, vb.astype(jnp.float32))
    o = o / jnp.swapaxes(l, -1, -2)[..., None]

    def from_sub(a):
        a = a.reshape((Bsz, dilation, Lp) + a.shape[4:])[:, :, :L]
        a = jnp.moveaxis(a, 1, 2)
        return a.reshape((Bsz, T) + a.shape[3:])

    return (from_sub(o), from_sub(jnp.swapaxes(m[..., 0], -1, -2)), from_sub(jnp.swapaxes(l, -1, -2)))


def _dilated_mixture_attention(q, k, v):
    slopes = _alibi_slopes(N_HEADS_B)
    outs = [_dilated_branch(q, k, v, slopes, w, d) for (w, d) in DILATED_CONFIGS]
    m_all = jnp.stack([o[1] for o in outs])
    m_top = jnp.max(m_all, axis=0)
    wts = jnp.stack([o[2] for o in outs]) * jnp.exp(m_all - m_top)
    o_all = jnp.stack([o[0] for o in outs])
    y = jnp.sum(wts[..., None] * o_all, axis=0) / jnp.sum(wts, axis=0)[..., None]
    return y.astype(q.dtype)


def _mixer_ab(h, w_in, w_out, lam_re_f, lam_im_f, log_dt_f, lam_re_b, lam_im_b, log_dt_b,
              b_re, b_im, c_re, c_im, d_skip, w_glu, b_glu, q_g, k_g):
    Bsz, T, _ = h.shape
    z = h @ w_in
    u, q, k, v = jnp.split(z, [D_A, D_A + D_B, D_A + 2 * D_B], axis=-1)
    y_a = _s5_mixer(u, lam_re_f, lam_im_f, log_dt_f, lam_re_b, lam_im_b, log_dt_b,
                    b_re, b_im, c_re, c_im, d_skip, w_glu, b_glu)
    q = _rms_norm(q.reshape(Bsz, T, N_HEADS_B, HEAD_DIM), q_g)
    k = _rms_norm(k.reshape(Bsz, T, N_HEADS_B, HEAD_DIM), k_g)
    v = v.reshape(Bsz, T, N_HEADS_B, HEAD_DIM)
    y_b = _dilated_mixture_attention(q, k, v).reshape(Bsz, T, D_B)
    return jnp.concatenate([y_a, y_b], axis=-1) @ w_out


def _mixer_cd(h, w_in, w_out, sconv_w):
    Bsz, T, _ = h.shape
    z = h @ w_in
    uc, hs, gb, gc = jnp.split(z, [D_C, D_C + D_D, D_C + 2 * D_D], axis=-1)
    ucg = uc.astype(jnp.float32).reshape(Bsz, T, FNET_GROUPS, FNET_GROUP_DIM)
    y_c = jnp.fft.fft2(ucg, axes=(1, 3), norm="ortho").real
    y_c = y_c.reshape(Bsz, T, D_C).astype(h.dtype)
    y_d = gb * _dwconv_centred(gc * hs, sconv_w)
    return jnp.concatenate([y_c, y_d], axis=-1) @ w_out


def _channel_mixer(h, w_gate, w_up, conv_w, conv_b, w_down):
    g = _dwconv_centred(h @ w_gate, conv_w) + conv_b
    return (jax.nn.silu(g) * (h @ w_up)) @ w_down


def _trunk(x, c, ada_w, ada_b, norm_mix_g, norm_ffn_g, ffn_w_gate, ffn_w_up, ffn_conv_w, ffn_conv_b,
           ffn_w_down, ab_w_in, ab_w_out, s5_lam_re_f, s5_lam_im_f, s5_log_dt_f, s5_lam_re_b,
           s5_lam_im_b, s5_log_dt_b, s5_b_re, s5_b_im, s5_c_re, s5_c_im, s5_d, s5_w_glu, s5_b_glu,
           q_norm_g, k_norm_g, cd_w_in, cd_w_out, sconv_w):
    cond = jax.nn.silu(c)
    for layer in range(DEPTH):
        mod = cond @ ada_w[layer] + ada_b[layer]
        sh_m, sc_m, g_m, sh_f, sc_f, g_f = jnp.split(mod, 6, axis=-1)
        h = _modulate(_rms_norm(x, norm_mix_g[layer]), sh_m, sc_m)
        i = layer // 2
        if layer % 2 == 0:
            y = _mixer_ab(h, ab_w_in[i], ab_w_out[i], s5_lam_re_f[i], s5_lam_im_f[i], s5_log_dt_f[i],
                          s5_lam_re_b[i], s5_lam_im_b[i], s5_log_dt_b[i], s5_b_re[i], s5_b_im[i],
                          s5_c_re[i], s5_c_im[i], s5_d[i], s5_w_glu[i], s5_b_glu[i],
                          q_norm_g[i], k_norm_g[i])
        else:
            y = _mixer_cd(h, cd_w_in[i], cd_w_out[i], sconv_w[i])
        x = x + g_m[:, None, :] * y
        h = _modulate(_rms_norm(x, norm_ffn_g[layer]), sh_f, sc_f)
        x = x + g_f[:, None, :] * _channel_mixer(h, ffn_w_gate[layer], ffn_w_up[layer], ffn_conv_w[layer],
                                                 ffn_conv_b[layer], ffn_w_down[layer])
    return x


def setup_inputs(seed: int = 0) -> dict:
    key = jax.random.key(seed)
    ks = iter(jax.random.split(key, 40))

    def nrm(shape, scale):
        return jax.random.normal(next(ks), shape, jnp.float32) * scale

    n_idx = jnp.arange(S5_STATE, dtype=jnp.float32)
    lam_im0 = jnp.pi * n_idx

    def lam_re():
        return -0.5 + nrm((N_AB, S5_GROUPS, S5_STATE), 0.01)

    def lam_im():
        return lam_im0 + nrm((N_AB, S5_GROUPS, S5_STATE), 0.01)

    def log_dt():
        return jax.random.uniform(next(ks), (N_AB, S5_GROUPS), jnp.float32,
                                  minval=math.log(DT_MIN), maxval=math.log(DT_MAX))

    return {
        "x_prompt": nrm((BATCH, SEQ, D_MODEL), 1.0),
        "x_sample": nrm((DEC_BATCH, DEC_SEQ, D_MODEL), 1.0),
        "c_prompt": nrm((BATCH, D_MODEL), 1.0),
        "c_sample": nrm((DEC_BATCH, D_MODEL), 1.0),
        "ada_w": nrm((DEPTH, D_MODEL, 6 * D_MODEL), 0.5 * D_MODEL ** -0.5),
        "ada_b": nrm((DEPTH, 6 * D_MODEL), 0.01),
        "norm_mix_g": 1.0 + nrm((DEPTH, D_MODEL), 0.02),
        "norm_ffn_g": 1.0 + nrm((DEPTH, D_MODEL), 0.02),
        "ffn_w_gate": nrm((DEPTH, D_MODEL, D_FF), D_MODEL ** -0.5),
        "ffn_w_up": nrm((DEPTH, D_MODEL, D_FF), D_MODEL ** -0.5),
        "ffn_conv_w": nrm((DEPTH, CONV_W, D_FF), CONV_W ** -0.5),
        "ffn_conv_b": nrm((DEPTH, D_FF), 0.01),
        "ffn_w_down": nrm((DEPTH, D_FF, D_MODEL), D_FF ** -0.5),
        "ab_w_in": nrm((N_AB, D_MODEL, D_A + 3 * D_B), D_MODEL ** -0.5),
        "ab_w_out": nrm((N_AB, D_A + D_B, D_MODEL), (D_A + D_B) ** -0.5),
        "s5_lam_re_f": lam_re(),
        "s5_lam_im_f": lam_im(),
        "s5_log_dt_f": log_dt(),
        "s5_lam_re_b": lam_re(),
        "s5_lam_im_b": lam_im(),
        "s5_log_dt_b": log_dt(),
        "s5_b_re": nrm((N_AB, S5_GROUPS, S5_STATE, S5_GROUP), (2 * S5_GROUP) ** -0.5),
        "s5_b_im": nrm((N_AB, S5_GROUPS, S5_STATE, S5_GROUP), (2 * S5_GROUP) ** -0.5),
        "s5_c_re": nrm((N_AB, S5_GROUPS, S5_GROUP, S5_STATE), (2 * S5_STATE) ** -0.5),
        "s5_c_im": nrm((N_AB, S5_GROUPS, S5_GROUP, S5_STATE), (2 * S5_STATE) ** -0.5),
        "s5_d": nrm((N_AB, D_A), 1.0),
        "s5_w_glu": nrm((N_AB, D_A, D_A), D_A ** -0.5),
        "s5_b_glu": nrm((N_AB, D_A), 0.01),
        "q_norm_g": 1.0 + nrm((N_AB, HEAD_DIM), 0.02),
        "k_norm_g": 1.0 + nrm((N_AB, HEAD_DIM), 0.02),
        "cd_w_in": nrm((N_CD, D_MODEL, D_C + 3 * D_D), D_MODEL ** -0.5),
        "cd_w_out": nrm((N_CD, D_C + D_D, D_MODEL), (D_C + D_D) ** -0.5),
        "sconv_w": nrm((N_CD, CONV_W, D_D), CONV_W ** -0.5),
    }


def reference(x_prompt, x_sample, c_prompt, c_sample, ada_w, ada_b, norm_mix_g, norm_ffn_g,
              ffn_w_gate, ffn_w_up, ffn_conv_w, ffn_conv_b, ffn_w_down, ab_w_in, ab_w_out,
              s5_lam_re_f, s5_lam_im_f, s5_log_dt_f, s5_lam_re_b, s5_lam_im_b, s5_log_dt_b,
              s5_b_re, s5_b_im, s5_c_re, s5_c_im, s5_d, s5_w_glu, s5_b_glu, q_norm_g, k_norm_g,
              cd_w_in, cd_w_out, sconv_w):
    weights = (ada_w, ada_b, norm_mix_g, norm_ffn_g, ffn_w_gate, ffn_w_up, ffn_conv_w, ffn_conv_b,
               ffn_w_down, ab_w_in, ab_w_out, s5_lam_re_f, s5_lam_im_f, s5_log_dt_f, s5_lam_re_b,
               s5_lam_im_b, s5_log_dt_b, s5_b_re, s5_b_im, s5_c_re, s5_c_im, s5_d, s5_w_glu, s5_b_glu,
               q_norm_g, k_norm_g, cd_w_in, cd_w_out, sconv_w)
    y_prompt = _trunk(x_prompt, c_prompt, *weights)
    y_sample = _trunk(x_sample, c_sample, *weights)
    return (y_prompt, y_sample)
```

```python
import functools
import math

import jax
import jax.numpy as jnp
import numpy as np
from jax import lax
from jax.experimental import pallas as pl
from jax.experimental.pallas import tpu as pltpu

F32 = jnp.float32
BF16 = jnp.bfloat16

D_MODEL = 1024
D_A = 512
S5_GROUP = 16
S5_GROUPS = 32
S5_STATE = 64
S5_CHUNK = 16
S5_LEVELS = 8
S5_LANE_GROUPS = 128 // S5_GROUP
D_B = 512
HEAD_DIM = 64
N_HEADS = 8
DILATED_CONFIGS = ((128, 1), (512, 4), (2048, 16))
ATTN_R = 64
ATTN_QB = 128
ATTN_KW = ATTN_QB + 2 * ATTN_R
ATTN_PAD = ATTN_R * 16
ATTN_UNROLL = 4
NEG_BIG = -1e30
D_C = 512
FNET_GROUP_DIM = 128
D_D = 512
D_FF = 2816
FFN_CHUNK = 256
EPS = 1e-6
HALO = 16
MIB = 2 ** 20


def _params(sem, vmem_mib):
    return pltpu.CompilerParams(dimension_semantics=sem, vmem_limit_bytes=vmem_mib * MIB)


def _const_spec(shape):
    nd = len(shape)
    return pl.BlockSpec(shape, lambda *_: (0,) * nd, pipeline_mode=pl.Buffered(1))


def _dot(a, b):
    return jnp.dot(a, b, preferred_element_type=F32)


def _norm_mod(x, g, shift, scale):
    ms = jnp.mean(x * x, axis=-1, keepdims=True)
    y = x * lax.rsqrt(ms + EPS) * g
    return y * (1.0 + scale) + shift


def _sigmoid(x):
    return 1.0 / (1.0 + jnp.exp(-x))


def _ada_kernel(c_ref, w_ref, b_ref, o_ref):
    c = c_ref[...]
    cond = (c * _sigmoid(c)).astype(BF16)
    o_ref[...] = _dot(cond, w_ref[...].astype(BF16)) + b_ref[...]


def _ada(c_all, ada_w, ada_b):
    depth, d, n = ada_w.shape
    rows = c_all.shape[0]
    tn = 1536
    return pl.pallas_call(
        _ada_kernel,
        grid=(depth, n // tn),
        in_specs=[pl.BlockSpec((rows, d), lambda l, j: (0, 0)),
                  pl.BlockSpec((None, d, tn), lambda l, j: (l, 0, j)),
                  pl.BlockSpec((None, 1, tn), lambda l, j: (l, 0, j))],
        out_specs=pl.BlockSpec((None, rows, tn), lambda l, j: (l, 0, j)),
        out_shape=jax.ShapeDtypeStruct((depth, rows, n), F32),
        compiler_params=_params(("parallel", "parallel"), 32),
        name="ada_mod",
    )(c_all, ada_w, ada_b.reshape(depth, 1, n))


def _ab_in_kernel(x_ref, mod_ref, g_ref, w_ref, hsum_ref, qg_ref, kg_ref, u_ref, q_ref, k_ref, v_ref):
    h = _norm_mod(x_ref[...], g_ref[...], mod_ref[0:1, :], mod_ref[1:2, :])
    z = _dot(h.astype(BF16), w_ref[...])
    u_ref[...] = z[:, 0:D_A]
    hsum = hsum_ref[...]

    def head_norm(a, g):
        sq = a * a
        hi = sq.astype(BF16)
        lo = (sq - hi.astype(F32)).astype(BF16)
        ms = (_dot(hi, hsum) + _dot(lo, hsum)) * (1.0 / HEAD_DIM)
        return a * lax.rsqrt(ms + EPS) * g

    q_ref[...] = head_norm(z[:, D_A:D_A + D_B], qg_ref[...]) * (1.0 / math.sqrt(HEAD_DIM))
    k_ref[...] = head_norm(z[:, D_A + D_B:D_A + 2 * D_B], kg_ref[...])
    v_ref[...] = z[:, D_A + 2 * D_B:]


def _ab_in(x, mod, g, w_in, hsum, qg, kg, tm=512):
    B, T, D = x.shape
    n = w_in.shape[1]
    tok = pl.BlockSpec((None, tm, D_B), lambda b, i: (b, i, 0))
    out = jax.ShapeDtypeStruct((B, T, D_B), F32)
    return pl.pallas_call(
        _ab_in_kernel,
        grid=(B, T // tm),
        in_specs=[pl.BlockSpec((None, tm, D), lambda b, i: (b, i, 0)),
                  pl.BlockSpec((None, 6, D), lambda b, i: (b, 0, 0)),
                  _const_spec((1, D)), _const_spec((D, n)), _const_spec((D_B, D_B)),
                  _const_spec((1, D_B)), _const_spec((1, D_B))],
        out_specs=[tok, tok, tok, tok],
        out_shape=[out, out, out, out],
        compiler_params=_params(("parallel", "parallel"), 48),
        name="ab_in",
    )(x, mod, g, w_in, hsum, qg, kg)


def _cmul(a, b):
    return a[0] * b[0] - a[1] * b[1], a[0] * b[1] + a[1] * b[0]


def _s5_param_kernel(lam_ref, dt_ref, b_re_ref, b_im_ref, ct_re_ref, ct_im_ref, c_re_ref, c_im_ref, d_ref,
                     wf_re_ref, wf_im_ref, wb_re_ref, wb_im_ref, kf_ref, kb_ref,
                     caf_re_ref, caf_im_ref, cab_re_ref, cab_im_ref, ap_ref):
    P, W = S5_STATE, S5_CHUNK * S5_GROUP
    kidx = lax.broadcasted_iota(jnp.int32, (P, W), 1) // S5_GROUP
    lane_ap = lax.broadcasted_iota(jnp.int32, (P, 128), 1)
    ap = jnp.zeros((P, 128), F32)
    c_re, c_im = c_re_ref[...], c_im_ref[...]
    outs = ((wf_re_ref, wf_im_ref, kf_ref, caf_re_ref, caf_im_ref),
            (wb_re_ref, wb_im_ref, kb_ref, cab_re_ref, cab_im_ref))
    for direction in range(2):
        lam_re = lam_ref[:, 2 * direction:2 * direction + 1]
        lam_im = lam_ref[:, 2 * direction + 1:2 * direction + 2]
        dt = jnp.exp(dt_ref[:, direction:direction + 1])
        mag = jnp.exp(lam_re * dt)
        a_re = mag * jnp.cos(lam_im * dt)
        a_im = mag * jnp.sin(lam_im * dt)
        den = lam_re * lam_re + lam_im * lam_im
        coef_re = ((a_re - 1.0) * lam_re + a_im * lam_im) / den
        coef_im = (a_im * lam_re - (a_re - 1.0) * lam_im) / den
        bb = _cmul((coef_re, coef_im), (b_re_ref[...], b_im_ref[...]))
        a_pow = (a_re, a_im)
        pw = (jnp.ones((P, W), F32), jnp.zeros((P, W), F32))
        for j in range(4):
            bit = ((kidx >> j) & 1) == 1
            pw = _cmul(pw, (jnp.where(bit, a_pow[0], 1.0), jnp.where(bit, a_pow[1], 0.0)))
            a_pow = _cmul(a_pow, a_pow)
        for j in range(S5_LEVELS):
            base = 4 * j + 2 * direction
            ap = jnp.where(lane_ap == base, a_pow[0], ap)
            ap = jnp.where(lane_ap == base + 1, a_pow[1], ap)
            a_pow = _cmul(a_pow, a_pow)
        w_re, w_im = _cmul(pw, bb)
        ca_re, ca_im = _cmul(_cmul(pw, (a_re, a_im)), (ct_re_ref[...], ct_im_ref[...]))
        hp = lax.Precision.HIGHEST
        kmat = (jnp.dot(c_re, w_re, precision=hp, preferred_element_type=F32)
                - jnp.dot(c_im, w_im, precision=hp, preferred_element_type=F32))
        o_w_re, o_w_im, o_k, o_ca_re, o_ca_im = outs[direction]
        o_w_re[...] = w_re
        o_w_im[...] = w_im
        o_k[...] = kmat
        o_ca_re[...] = ca_re
        o_ca_im[...] = -ca_im
    row = lax.broadcasted_iota(jnp.int32, (S5_GROUP, W), 0)
    lane = lax.broadcasted_iota(jnp.int32, (S5_GROUP, W), 1)
    lag0 = kb_ref[...] + jnp.where(row == lane, d_ref[...], 0.0)
    kf_ref[...] = kf_ref[...] + jnp.where(lane < S5_GROUP, lag0, 0.0)
    ap_ref[...] = ap


def _s5_tables(lam_re_f, lam_im_f, log_dt_f, lam_re_b, lam_im_b, log_dt_b, b_re, b_im, c_re, c_im, d_skip):
    G, P, C, L, W = S5_GROUPS, S5_STATE, S5_GROUP, S5_CHUNK, S5_CHUNK * S5_GROUP
    lam = jnp.stack([lam_re_f, lam_im_f, lam_re_b, lam_im_b], axis=-1)
    dts = jnp.stack([log_dt_f, log_dt_b], axis=-1).reshape(G, 1, 2)
    tile_k = lambda a: jnp.tile(a, (1, 1, L))
    ct = lambda a: tile_k(jnp.swapaxes(a, 1, 2))
    grp = lambda r, c: pl.BlockSpec((None, r, c), lambda g: (g, 0, 0))
    pw_out = jax.ShapeDtypeStruct((G, P, W), F32)
    k_out = jax.ShapeDtypeStruct((G, C, W), F32)
    wf_re, wf_im, wb_re, wb_im, kf, kb, caf_re, caf_im, cab_re, cab_im, ap = pl.pallas_call(
        _s5_param_kernel,
        grid=(G,),
        in_specs=[grp(P, 4), grp(1, 2), grp(P, W), grp(P, W), grp(P, W), grp(P, W), grp(C, P), grp(C, P), grp(C, 1)],
        out_specs=[grp(P, W)] * 4 + [grp(C, W)] * 2 + [grp(P, W)] * 4 + [grp(P, 128)],
        out_shape=[pw_out] * 4 + [k_out] * 2 + [pw_out] * 4 + [jax.ShapeDtypeStruct((G, P, 128), F32)],
        compiler_params=_params(("parallel",), 32),
        name="s5_params",
    )(lam, dts, tile_k(b_re), tile_k(b_im), ct(c_re), ct(c_im), c_re, c_im, d_skip.reshape(G, C, 1))

    rev_k = lambda a: a.reshape(G, P, L, C)[:, :, ::-1].reshape(G, P, W)
    lag_tab = jnp.concatenate([kb.reshape(G, C, L, C)[:, :, :0:-1], kf.reshape(G, C, L, C)], axis=2)
    lag = (np.arange(L)[None, :] - np.arange(L)[:, None]) + (L - 1)
    toep = lag_tab[:, :, lag, :]
    toep = toep.transpose(0, 2, 4, 3, 1).reshape(G, W, W)
    t_ = lambda a: jnp.swapaxes(a, 1, 2)
    w1 = jnp.concatenate([toep, t_(rev_k(wf_re)), t_(rev_k(wf_im)), t_(wb_re), t_(wb_im)], axis=2)
    w2 = jnp.concatenate([caf_re, caf_im, rev_k(cab_re), rev_k(cab_im)], axis=1)
    apj = ap[:, :, :4 * S5_LEVELS].reshape(G, P, S5_LEVELS, 4).transpose(0, 2, 3, 1)
    mult = jnp.concatenate([apj[:, :, 0], apj[:, :, 0], -apj[:, :, 1], apj[:, :, 1],
                            apj[:, :, 2], apj[:, :, 2], -apj[:, :, 3], apj[:, :, 3]], axis=-1)
    order = _s5_lane_order()[np.arange(G) % S5_LANE_GROUPS]
    gi = np.arange(G)[:, None]
    w1 = w1[gi, order]
    w1 = jnp.concatenate([w1[:, :, :W][gi[:, :, None], np.arange(W)[None, :, None], order[:, None, :]],
                          w1[:, :, W:]], axis=2)
    w2 = w2[gi[:, :, None], np.arange(W)[None, :, None], order[:, None, :]]
    return w1.astype(BF16), w2.astype(BF16), mult


def _s5_lane_order():
    C, NG = S5_GROUP, S5_LANE_GROUPS
    order = np.empty((NG, S5_CHUNK * C), np.int64)
    for gl in range(NG):
        for half in range(S5_CHUNK // NG):
            for blk in range(NG):
                s = half * NG + (blk - gl) % NG
                order[gl, half * 128 + blk * C:half * 128 + (blk + 1) * C] = s * C + np.arange(C)
    return order


def _s5_kernel(u_ref, w1_ref, w2_ref, mult_ref, y_ref, ucbuf, ybuf, *, nch, nlev):
    row = lax.broadcasted_iota(jnp.int32, (nch, 128), 0)
    lane_blk = lax.broadcasted_iota(jnp.int32, (nch, 128), 1) // S5_GROUP
    W, L, C, NG = S5_CHUNK * S5_GROUP, S5_CHUNK, S5_GROUP, S5_LANE_GROUPS

    def pick(slabs, shift):
        acc = slabs[(0 - shift) % NG]
        for blk in range(1, NG):
            acc = jnp.where(lane_blk == blk, slabs[(blk - shift) % NG], acc)
        return acc

    def shift_down(x, s):
        if s % 8 == 0:
            return jnp.concatenate([jnp.zeros((s, 128), F32), x[:nch - s]], axis=0)
        return jnp.where(row >= s, pltpu.roll(x, s, 0), 0.0)

    def shift_up(x, s):
        if s % 8 == 0:
            return jnp.concatenate([x[s:], jnp.zeros((s, 128), F32)], axis=0)
        return jnp.where(row < nch - s, pltpu.roll(x, nch - s, 0), 0.0)

    for half in range(L // NG):
        rolled = []
        for k in range(NG):
            slab = u_ref[pl.ds(half * NG + k, nch, stride=L), :]
            rolled.append(pltpu.roll(slab, k * C, 1) if k else slab)
        for gl in range(NG):
            ucbuf[gl, :, half * 128:(half + 1) * 128] = pick(rolled, gl).astype(BF16)

    def body(g, carry):
        r = _dot(ucbuf[g], w1_ref[g])
        xf = r[:, W:W + 128]
        xb = r[:, W + 128:W + 256]
        mult = mult_ref[g]
        for j in range(nlev):
            s = 1 << j
            sh = shift_down(xf, s)
            xf = xf + mult[j:j + 1, 0:128] * sh + mult[j:j + 1, 128:256] * pltpu.roll(sh, 64, 1)
            sh = shift_up(xb, s)
            xb = xb + mult[j:j + 1, 256:384] * sh + mult[j:j + 1, 384:512] * pltpu.roll(sh, 64, 1)
        x_prev = shift_down(xf, 1)
        x_next = shift_up(xb, 1)
        xs = jnp.concatenate([x_prev, x_next], axis=1).astype(BF16)
        ybuf[g] = r[:, 0:W] + _dot(xs, w2_ref[g])
        return carry

    for g in range(NG):
        body(g, 0)

    for half in range(L // NG):
        for k in range(NG):
            slab = pick([ybuf[gl, :, half * 128:(half + 1) * 128] for gl in range(NG)], k)
            y_ref[pl.ds(half * NG + k, nch, stride=L), :] = pltpu.roll(slab, (NG - k) * C, 1) if k else slab


def _s5(u, w1, w2, mult):
    B, T, D = u.shape
    nch, W, NG = T // S5_CHUNK, S5_CHUNK * S5_GROUP, S5_LANE_GROUPS
    nlev = int(math.log2(nch))
    assert 1 << nlev == nch and nlev <= S5_LEVELS
    tok = pl.BlockSpec((None, T, 128), lambda b, j: (b, 0, j))
    return pl.pallas_call(
        functools.partial(_s5_kernel, nch=nch, nlev=nlev),
        grid=(B, D // 128),
        in_specs=[tok,
                  pl.BlockSpec((NG, W, 2 * W), lambda b, j: (j, 0, 0)),
                  pl.BlockSpec((NG, W, W), lambda b, j: (j, 0, 0)),
                  pl.BlockSpec((NG, S5_LEVELS, 2 * W), lambda b, j: (j, 0, 0))],
        out_specs=tok,
        out_shape=jax.ShapeDtypeStruct((B, T, D), F32),
        scratch_shapes=[pltpu.VMEM((NG, nch, W), BF16), pltpu.VMEM((NG, nch, W), F32)],
        compiler_params=_params(("parallel", "parallel"), 32),
        name="s5_scan",
    )(u, w1, w2, mult)


def _attn_bias_table():
    i = np.arange(ATTN_QB)[:, None]
    j = np.arange(ATTN_KW)[None, :]
    rel = j - ATTN_R - i
    slopes = 2.0 ** (-8.0 * np.arange(1, N_HEADS + 1, dtype=np.float64) / N_HEADS)
    tab = np.empty((N_HEADS // 2, len(DILATED_CONFIGS), 4, 2, ATTN_QB, ATTN_KW), np.float32)
    for bi, (_, d) in enumerate(DILATED_CONFIGS):
        for ty in range(4):
            valid = np.abs(rel) <= ATTN_R
            if ty & 1:
                valid = valid & (j >= ATTN_R)
            if ty & 2:
                valid = valid & (j < ATTN_QB + ATTN_R)
            for h in range(N_HEADS):
                dist = (np.abs(rel) * d).astype(np.float32)
                alibi = (-np.float32(slopes[h])) * dist
                tab[h // 2, bi, ty, h % 2] = np.where(valid, alibi, np.float32(NEG_BIG))
    return tab.reshape(N_HEADS // 2, len(DILATED_CONFIGS) * 4, 2 * ATTN_QB, ATTN_KW)


def _attn_kernel(q_ref, k_ref, v_ref, bias_ref, o_ref, kp, vp, acc_o, acc_m, acc_l, *, T):
    zeros = jnp.zeros((ATTN_PAD, 128), F32)
    for buf, src in ((kp, k_ref), (vp, v_ref)):
        buf[0:ATTN_PAD, :] = zeros
        buf[ATTN_PAD + T:ATTN_PAD + T + ATTN_PAD, :] = zeros
        buf[ATTN_PAD:ATTN_PAD + T, :] = src[...]
    first_head = lax.broadcasted_iota(jnp.int32, (ATTN_QB, 128), 1) < HEAD_DIM
    ones = jnp.ones((ATTN_KW, 128), BF16)
    n_branch = len(DILATED_CONFIGS)

    def block_softmax(bi, d, nblk, idx):
        r = idx % d
        blk = idx // d
        q_start = r + d * ATTN_QB * blk
        rows = pl.ds(q_start, ATTN_QB, stride=d)
        keys = pl.ds(ATTN_PAD + q_start - ATTN_R * d, ATTN_KW, stride=d)
        ty = jnp.where(blk == 0, 1, 0) + jnp.where(blk == nblk - 1, 2, 0)
        qf = q_ref[rows, :]
        q2 = jnp.concatenate([jnp.where(first_head, qf, 0.0), jnp.where(first_head, 0.0, qf)], axis=0).astype(BF16)
        kw = kp[keys, :].astype(BF16)
        s = lax.dot_general(q2, kw, (((1,), (1,)), ((), ())), preferred_element_type=F32)
        s = s + bias_ref[bi * 4 + ty]
        m = jnp.max(s, axis=-1, keepdims=True)
        p = jnp.exp(s - m).astype(BF16)
        ov = _dot(p, jnp.concatenate([vp[keys, :].astype(BF16), ones], axis=1))
        o_new = jnp.where(first_head, ov[0:ATTN_QB, 0:128], ov[ATTN_QB:, 0:128])
        l_new = jnp.where(first_head, ov[0:ATTN_QB, 128:], ov[ATTN_QB:, 128:])
        m_new = jnp.where(first_head, m[0:ATTN_QB], m[ATTN_QB:])
        return rows, o_new, m_new, l_new

    for bi, (_, d) in enumerate(DILATED_CONFIGS):
        nblk = T // d // ATTN_QB

        def group(it, carry, bi=bi, d=d, nblk=nblk):
            new = [block_softmax(bi, d, nblk, it * ATTN_UNROLL + u) for u in range(ATTN_UNROLL)]
            if bi == 0:
                for rows, o_new, m_new, l_new in new:
                    acc_o[rows, :] = o_new
                    acc_m[rows, :] = m_new
                    acc_l[rows, :] = l_new
                return carry
            old = [(acc_o[rows, :], acc_m[rows, :], acc_l[rows, :]) for rows, _, _, _ in new]
            merged = []
            for (rows, o_new, m_new, l_new), (o_old, m_old, l_old) in zip(new, old):
                e = jnp.exp(-jnp.abs(m_old - m_new))
                keep = m_old >= m_new
                a = jnp.where(keep, 1.0, e)
                b = jnp.where(keep, e, 1.0)
                merged.append((rows, a * o_old + b * o_new, jnp.maximum(m_old, m_new), a * l_old + b * l_new))
            for rows, o_tot, m_tot, l_tot in merged:
                if bi == n_branch - 1:
                    o_ref[rows, :] = o_tot / l_tot
                else:
                    acc_o[rows, :] = o_tot
                    acc_m[rows, :] = m_tot
                    acc_l[rows, :] = l_tot
            return carry

        lax.fori_loop(0, d * nblk // ATTN_UNROLL, group, 0)


def _attention(q, k, v, bias):
    B, T, _ = q.shape
    assert T % (ATTN_QB * DILATED_CONFIGS[-1][1]) == 0
    pair = pl.BlockSpec((None, T, 128), lambda b, p: (b, 0, p))
    nb = bias.shape[1]
    return pl.pallas_call(
        functools.partial(_attn_kernel, T=T),
        grid=(B, N_HEADS // 2),
        in_specs=[pair, pair, pair, pl.BlockSpec((None, nb, 2 * ATTN_QB, ATTN_KW), lambda b, p: (p, 0, 0, 0))],
        out_specs=pair,
        out_shape=jax.ShapeDtypeStruct((B, T, D_B), F32),
        scratch_shapes=[pltpu.VMEM((T + 2 * ATTN_PAD, 128), F32), pltpu.VMEM((T + 2 * ATTN_PAD, 128), F32),
                        pltpu.VMEM((T, 128), F32), pltpu.VMEM((T, 128), F32), pltpu.VMEM((T, 128), F32)],
        compiler_params=_params(("parallel", "arbitrary"), 48),
        name="dilated_attn",
    )(q, k, v, bias)


def _ab_out_kernel(x_ref, ya_ref, yb_ref, mod_ref, wglu_ref, bglu_ref, wo_ref, o_ref):
    y = ya_ref[...]
    y = 0.5 * y * (1.0 + jnp.tanh(math.sqrt(2.0 / math.pi) * (y + 0.044715 * (y * y * y))))
    y = y * _sigmoid(_dot(y.astype(BF16), wglu_ref[...]) + bglu_ref[...])
    out = _dot(y.astype(BF16), wo_ref[0:D_A, :]) + _dot(yb_ref[...].astype(BF16), wo_ref[D_A:, :])
    o_ref[...] = x_ref[...] + mod_ref[2:3, :] * out


def _ab_out(x, ya, yb, mod, w_glu, b_glu, w_out, tm=512):
    B, T, D = x.shape
    half = pl.BlockSpec((None, tm, D_A), lambda b, i: (b, i, 0))
    full = pl.BlockSpec((None, tm, D), lambda b, i: (b, i, 0))
    return pl.pallas_call(
        _ab_out_kernel,
        grid=(B, T // tm),
        in_specs=[full, half, half, pl.BlockSpec((None, 6, D), lambda b, i: (b, 0, 0)),
                  _const_spec((D_A, D_A)), _const_spec((1, D_A)), _const_spec((D, D))],
        out_specs=full,
        out_shape=jax.ShapeDtypeStruct((B, T, D), F32),
        compiler_params=_params(("parallel", "parallel"), 48),
        name="ab_out",
    )(x, ya, yb, mod, w_glu, b_glu, w_out)


def _halo_specs(tm, T, D):
    nh = T // HALO
    per = tm // HALO
    main = pl.BlockSpec((None, tm, D), lambda b, i: (b, i, 0))
    prev = pl.BlockSpec((None, HALO, D), lambda b, i: (b, jnp.maximum(i * per - 1, 0), 0))
    nxt = pl.BlockSpec((None, HALO, D), lambda b, i: (b, jnp.minimum((i + 1) * per, nh - 1), 0))
    return main, prev, nxt


def _conv3_rows(buf, w_ref, tm):
    return (buf[pl.ds(HALO - 1, tm), :] * w_ref[0:1, :] + buf[pl.ds(HALO, tm), :] * w_ref[1:2, :]
            + buf[pl.ds(HALO + 1, tm), :] * w_ref[2:3, :])


def _zero_sequence_edges(buf, tm):
    zero_row = jnp.zeros((1, buf.shape[1]), F32)

    @pl.when(pl.program_id(1) == 0)
    def _():
        buf[HALO - 1:HALO, :] = zero_row

    @pl.when(pl.program_id(1) == pl.num_programs(1) - 1)
    def _():
        buf[HALO + tm:HALO + tm + 1, :] = zero_row


def _ffn_kernel(x_ref, xp_ref, xn_ref, mod_ref, g_ref, wg_ref, wu_ref, cw_ref, cb_ref, wd_ref, o_ref, gbuf, abuf, *, tm):
    x = x_ref[...]
    g, shift, scale = g_ref[...], mod_ref[3:4, :], mod_ref[4:5, :]
    has_prev = jnp.where(pl.program_id(1) > 0, 1.0, 0.0)
    has_next = jnp.where(pl.program_id(1) < pl.num_programs(1) - 1, 1.0, 0.0)
    h = _norm_mod(x, g, shift, scale)
    he = jnp.concatenate([_norm_mod(xp_ref[...], g, shift, scale) * has_prev, h,
                          _norm_mod(xn_ref[...], g, shift, scale) * has_next], axis=0).astype(BF16)
    h = he[HALO:HALO + tm]
    for c in range(D_FF // FFN_CHUNK):
        cols = slice(c * FFN_CHUNK, (c + 1) * FFN_CHUNK)
        gbuf[:, cols] = _dot(he, wg_ref[:, cols])
        gate = (gbuf[pl.ds(HALO - 1, tm), cols] * cw_ref[0:1, cols] + gbuf[pl.ds(HALO, tm), cols] * cw_ref[1:2, cols]
                + gbuf[pl.ds(HALO + 1, tm), cols] * cw_ref[2:3, cols] + cb_ref[:, cols])
        up = _dot(h, wu_ref[:, cols])
        abuf[:, cols] = ((gate * _sigmoid(gate)) * up).astype(BF16)
    o_ref[...] = x + mod_ref[5:6, :] * _dot(abuf[...], wd_ref[...])


def _ffn(x, mod, g, w_gate, w_up, conv_w, conv_b, w_down, tm=512):
    B, T, D = x.shape
    main, prev, nxt = _halo_specs(tm, T, D)
    return pl.pallas_call(
        functools.partial(_ffn_kernel, tm=tm),
        grid=(B, T // tm),
        in_specs=[main, prev, nxt, pl.BlockSpec((None, 6, D), lambda b, i: (b, 0, 0)), _const_spec((1, D)),
                  _const_spec((D, D_FF)), _const_spec((D, D_FF)), _const_spec((3, D_FF)), _const_spec((1, D_FF)),
                  _const_spec((D_FF, D))],
        out_specs=main,
        out_shape=jax.ShapeDtypeStruct((B, T, D), F32),
        scratch_shapes=[pltpu.VMEM((tm + 2 * HALO, D_FF), F32), pltpu.VMEM((tm, D_FF), BF16)],
        compiler_params=_params(("parallel", "arbitrary"), 56),
        name="conv_ffn",
    )(x, x, x, mod, g, w_gate, w_up, conv_w, conv_b, w_down)


def _cd_in_kernel(x_ref, xp_ref, xn_ref, mod_ref, g_ref, w_ref, cbd_ref, sbd_ref, sw_ref, ab_ref, yd_ref, cbuf, *, tm):
    xe = jnp.concatenate([xp_ref[...], x_ref[...], xn_ref[...]], axis=0)
    he = _norm_mod(xe, g_ref[...], mod_ref[0:1, :], mod_ref[1:2, :]).astype(BF16)
    z = _dot(he, w_ref[...])
    cbuf[...] = z[:, D_C + 2 * D_D:] * z[:, D_C:D_C + D_D]
    _zero_sequence_edges(cbuf, tm)
    yd_ref[...] = z[HALO:HALO + tm, D_C + D_D:D_C + 2 * D_D] * _conv3_rows(cbuf, sw_ref, tm)
    uc = z[HALO:HALO + tm, 0:D_C].astype(BF16)
    ab_ref[0] = _dot(uc, cbd_ref[...]).astype(BF16)
    ab_ref[1] = _dot(uc, sbd_ref[...]).astype(BF16)


def _cd_in(x, mod, g, w_in, cbd, sbd, sconv_w, tm=256):
    B, T, D = x.shape
    main, prev, nxt = _halo_specs(tm, T, D)
    return pl.pallas_call(
        functools.partial(_cd_in_kernel, tm=tm),
        grid=(B, T // tm),
        in_specs=[main, prev, nxt, pl.BlockSpec((None, 6, D), lambda b, i: (b, 0, 0)), _const_spec((1, D)),
                  _const_spec((D, w_in.shape[1])), _const_spec((D_C, D_C)), _const_spec((D_C, D_C)),
                  _const_spec((3, D_D))],
        out_specs=[pl.BlockSpec((None, 2, tm, D_C), lambda b, i: (b, 0, i, 0)),
                   pl.BlockSpec((None, tm, D_D), lambda b, i: (b, i, 0))],
        out_shape=[jax.ShapeDtypeStruct((B, 2, T, D_C), BF16), jax.ShapeDtypeStruct((B, T, D_D), F32)],
        scratch_shapes=[pltpu.VMEM((tm + 2 * HALO, D_D), F32)],
        compiler_params=_params(("parallel", "arbitrary"), 48),
        name="cd_in",
    )(x, x, x, mod, g, w_in, cbd, sbd, sconv_w)


def _cd_out_kernel(dft_ref, ab_ref, yd_ref, x_ref, mod_ref, wo_ref, o_ref):
    yc = _dot(dft_ref[...], ab_ref[...])
    out = _dot(yc.astype(BF16), wo_ref[0:D_C, :]) + _dot(yd_ref[...].astype(BF16), wo_ref[D_C:, :])
    o_ref[...] = x_ref[...] + mod_ref[2:3, :] * out


def _cd_out(dft, ab, yd, x, mod, w_out, tm=256):
    B, T, D = x.shape
    return pl.pallas_call(
        _cd_out_kernel,
        grid=(B, T // tm),
        in_specs=[pl.BlockSpec((tm, 2 * T), lambda b, i: (i, 0)),
                  pl.BlockSpec((None, 2 * T, D_C), lambda b, i: (b, 0, 0)),
                  pl.BlockSpec((None, tm, D_D), lambda b, i: (b, i, 0)),
                  pl.BlockSpec((None, tm, D), lambda b, i: (b, i, 0)),
                  pl.BlockSpec((None, 6, D), lambda b, i: (b, 0, 0)), _const_spec((D, D))],
        out_specs=pl.BlockSpec((None, tm, D), lambda b, i: (b, i, 0)),
        out_shape=jax.ShapeDtypeStruct((B, T, D), F32),
        compiler_params=_params(("parallel", "arbitrary"), 56),
        name="cd_out",
    )(dft, ab, yd, x, mod, w_out)


def _seq_dft_matrix(T):
    t = jnp.arange(T, dtype=jnp.int32)
    ang = ((t[:, None] * t[None, :]) % T).astype(F32) * (2.0 * math.pi / T)
    scale = 1.0 / math.sqrt(T)
    return jnp.concatenate([jnp.cos(ang) * scale, jnp.sin(ang) * (-scale)], axis=1).astype(BF16)


def _channel_dft_matrices():
    c = np.arange(D_C)
    ang = 2.0 * np.pi * ((c[:, None] % FNET_GROUP_DIM) * (c[None, :] % FNET_GROUP_DIM) % FNET_GROUP_DIM) / FNET_GROUP_DIM
    same = (c[:, None] // FNET_GROUP_DIM) == (c[None, :] // FNET_GROUP_DIM)
    scale = 1.0 / math.sqrt(FNET_GROUP_DIM)
    cbd = np.where(same, np.cos(ang) * scale, 0.0).astype(np.float32)
    sbd = np.where(same, np.sin(ang) * scale, 0.0).astype(np.float32)
    return jnp.asarray(cbd).astype(BF16), jnp.asarray(sbd).astype(BF16)


def _trunk(x, mods, wts):
    B, T, D = x.shape
    nch = T // S5_CHUNK
    mod = mods[0]
    u, q, k, v = _ab_in(x, mod, wts["norm_mix_g"][0], wts["ab_w_in"], wts["hsum"], wts["q_g"], wts["k_g"])
    ya = _s5(u, wts["s5_w1"], wts["s5_w2"], wts["s5_mult"])
    yb = _attention(q, k, v, wts["attn_bias"])
    x = _ab_out(x, ya, yb, mod, wts["s5_w_glu"], wts["s5_b_glu"], wts["ab_w_out"])
    x = _ffn(x, mod, wts["norm_ffn_g"][0], *wts["ffn"][0])
    mod = mods[1]
    ab, yd = _cd_in(x, mod, wts["norm_mix_g"][1], wts["cd_w_in"], wts["cbd"], wts["sbd"], wts["sconv_w"])
    x = _cd_out(_seq_dft_matrix(T), ab.reshape(B, 2 * T, D_C), yd, x, mod, wts["cd_w_out"])
    x = _ffn(x, mod, wts["norm_ffn_g"][1], *wts["ffn"][1])
    return x


def kernel(x_prompt, x_sample, c_prompt, c_sample, ada_w, ada_b, norm_mix_g, norm_ffn_g, ffn_w_gate, ffn_w_up, ffn_conv_w, ffn_conv_b, ffn_w_down, ab_w_in, ab_w_out, s5_lam_re_f, s5_lam_im_f, s5_log_dt_f, s5_lam_re_b, s5_lam_im_b, s5_log_dt_b, s5_b_re, s5_b_im, s5_c_re, s5_c_im, s5_d, s5_w_glu, s5_b_glu, q_norm_g, k_norm_g, cd_w_in, cd_w_out, sconv_w):
    depth = ada_w.shape[0]
    assert depth == 2 and ab_w_in.shape[0] == 1 and cd_w_in.shape[0] == 1
    bp = x_prompt.shape[0]
    mod_all = _ada(jnp.concatenate([c_prompt, c_sample], axis=0), ada_w, ada_b)
    s5_w1, s5_w2, s5_mult = _s5_tables(s5_lam_re_f[0], s5_lam_im_f[0], s5_log_dt_f[0], s5_lam_re_b[0],
                                       s5_lam_im_b[0], s5_log_dt_b[0], s5_b_re[0], s5_b_im[0],
                                       s5_c_re[0], s5_c_im[0], s5_d[0])
    head = np.arange(D_B) // HEAD_DIM
    cbd, sbd = _channel_dft_matrices()
    wts = dict(
        norm_mix_g=norm_mix_g.reshape(depth, 1, D_MODEL), norm_ffn_g=norm_ffn_g.reshape(depth, 1, D_MODEL),
        ab_w_in=ab_w_in[0].astype(BF16), ab_w_out=ab_w_out[0].astype(BF16),
        hsum=jnp.asarray(head[:, None] == head[None, :], BF16),
        q_g=jnp.tile(q_norm_g[0], N_HEADS).reshape(1, D_B), k_g=jnp.tile(k_norm_g[0], N_HEADS).reshape(1, D_B),
        s5_w1=s5_w1, s5_w2=s5_w2, s5_mult=s5_mult,
        s5_w_glu=s5_w_glu[0].astype(BF16), s5_b_glu=s5_b_glu[0].reshape(1, D_A),
        attn_bias=jnp.asarray(_attn_bias_table()),
        ffn=[(ffn_w_gate[l].astype(BF16), ffn_w_up[l].astype(BF16), ffn_conv_w[l], ffn_conv_b[l].reshape(1, D_FF),
              ffn_w_down[l].astype(BF16)) for l in range(depth)],
        cd_w_in=cd_w_in[0].astype(BF16), cd_w_out=cd_w_out[0].astype(BF16), cbd=cbd, sbd=sbd, sconv_w=sconv_w[0],
    )
    outs = []
    for x, rows in ((x_prompt, slice(0, bp)), (x_sample, slice(bp, None))):
        mods = [mod_all[l, rows].reshape(x.shape[0], 6, D_MODEL) for l in range(depth)]
        outs.append(_trunk(x, mods, wts))
    return tuple(outs)
```

```python
import functools
import math

import jax
import jax.numpy as jnp
import numpy as np
from jax import lax
from jax.experimental import pallas as pl
from jax.experimental.pallas import tpu as pltpu

F32 = jnp.float32
BF16 = jnp.bfloat16

D_MODEL = 1024
D_A = 512
S5_GROUP = 16
S5_GROUPS = 32
S5_STATE = 64
S5_CHUNK = 16
S5_LEVELS = 8
S5_LANE_GROUPS = 128 // S5_GROUP
D_B = 512
HEAD_DIM = 64
N_HEADS = 8
DILATED_CONFIGS = ((128, 1), (512, 4), (2048, 16))
ATTN_R = 64
ATTN_QB = 128
ATTN_KW = ATTN_QB + 2 * ATTN_R
ATTN_PAD = ATTN_R * 16
ATTN_UNROLL = 8
NEG_BIG = -1e30
D_C = 512
FNET_GROUP_DIM = 128
D_D = 512
D_FF = 2816
FFN_CHUNK = 256
EPS = 1e-6
HALO = 16
MIB = 2 ** 20


def _params(sem, vmem_mib):
    return pltpu.CompilerParams(dimension_semantics=sem, vmem_limit_bytes=vmem_mib * MIB)


def _const_spec(shape):
    nd = len(shape)
    return pl.BlockSpec(shape, lambda *_: (0,) * nd, pipeline_mode=pl.Buffered(1))


def _dot(a, b):
    return jnp.dot(a, b, preferred_element_type=F32)


def _norm_mod(x, g, shift, scale):
    ms = jnp.mean(x * x, axis=-1, keepdims=True)
    y = x * lax.rsqrt(ms + EPS) * g
    return y * (1.0 + scale) + shift


def _sigmoid(x):
    return 1.0 / (1.0 + jnp.exp(-x))


def _ada_kernel(c_ref, w_ref, b_ref, o_ref):
    c = c_ref[...]
    cond = (c * _sigmoid(c)).astype(BF16)
    o_ref[...] = _dot(cond, w_ref[...].astype(BF16)) + b_ref[...]


def _ada(c_all, ada_w, ada_b):
    depth, d, n = ada_w.shape
    rows = c_all.shape[0]
    tn = 1536
    return pl.pallas_call(
        _ada_kernel,
        grid=(depth, n // tn),
        in_specs=[pl.BlockSpec((rows, d), lambda l, j: (0, 0)),
                  pl.BlockSpec((None, d, tn), lambda l, j: (l, 0, j)),
                  pl.BlockSpec((None, 1, tn), lambda l, j: (l, 0, j))],
        out_specs=pl.BlockSpec((None, rows, tn), lambda l, j: (l, 0, j)),
        out_shape=jax.ShapeDtypeStruct((depth, rows, n), F32),
        compiler_params=_params(("parallel", "parallel"), 32),
        name="ada_mod",
    )(c_all, ada_w, ada_b.reshape(depth, 1, n))


def _ab_in_kernel(x_ref, mod_ref, g_ref, w_ref, hsum_ref, qg_ref, kg_ref, u_ref, q_ref, k_ref, v_ref):
    h = _norm_mod(x_ref[...], g_ref[...], mod_ref[0:1, :], mod_ref[1:2, :])
    z = _dot(h.astype(BF16), w_ref[...])
    u_ref[...] = z[:, 0:D_A]
    hsum = hsum_ref[...]

    def head_norm(a, g):
        sq = a * a
        hi = sq.astype(BF16)
        lo = (sq - hi.astype(F32)).astype(BF16)
        ms = (_dot(hi, hsum) + _dot(lo, hsum)) * (1.0 / HEAD_DIM)
        return a * lax.rsqrt(ms + EPS) * g

    q_ref[...] = head_norm(z[:, D_A:D_A + D_B], qg_ref[...]) * (1.0 / math.sqrt(HEAD_DIM))
    k_ref[...] = head_norm(z[:, D_A + D_B:D_A + 2 * D_B], kg_ref[...])
    v_ref[...] = z[:, D_A + 2 * D_B:]


def _ab_in(x, mod, g, w_in, hsum, qg, kg, tm=512):
    B, T, D = x.shape
    n = w_in.shape[1]
    tok = pl.BlockSpec((None, tm, D_B), lambda b, i: (b, i, 0))
    out = jax.ShapeDtypeStruct((B, T, D_B), F32)
    return pl.pallas_call(
        _ab_in_kernel,
        grid=(B, T // tm),
        in_specs=[pl.BlockSpec((None, tm, D), lambda b, i: (b, i, 0)),
                  pl.BlockSpec((None, 6, D), lambda b, i: (b, 0, 0)),
                  _const_spec((1, D)), _const_spec((D, n)), _const_spec((D_B, D_B)),
                  _const_spec((1, D_B)), _const_spec((1, D_B))],
        out_specs=[tok, tok, tok, tok],
        out_shape=[out, out, out, out],
        compiler_params=_params(("parallel", "parallel"), 48),
        name="ab_in",
    )(x, mod, g, w_in, hsum, qg, kg)


def _cmul(a, b):
    return a[0] * b[0] - a[1] * b[1], a[0] * b[1] + a[1] * b[0]


def _s5_param_kernel(lam_ref, dt_ref, b_re_ref, b_im_ref, ct_re_ref, ct_im_ref, c_re_ref, c_im_ref, d_ref,
                     wf_re_ref, wf_im_ref, wb_re_ref, wb_im_ref, kf_ref, kb_ref,
                     caf_re_ref, caf_im_ref, cab_re_ref, cab_im_ref, ap_ref):
    P, W = S5_STATE, S5_CHUNK * S5_GROUP
    kidx = lax.broadcasted_iota(jnp.int32, (P, W), 1) // S5_GROUP
    lane_ap = lax.broadcasted_iota(jnp.int32, (P, 128), 1)
    ap = jnp.zeros((P, 128), F32)
    c_re, c_im = c_re_ref[...], c_im_ref[...]
    outs = ((wf_re_ref, wf_im_ref, kf_ref, caf_re_ref, caf_im_ref),
            (wb_re_ref, wb_im_ref, kb_ref, cab_re_ref, cab_im_ref))
    for direction in range(2):
        lam_re = lam_ref[:, 2 * direction:2 * direction + 1]
        lam_im = lam_ref[:, 2 * direction + 1:2 * direction + 2]
        dt = jnp.exp(dt_ref[:, direction:direction + 1])
        mag = jnp.exp(lam_re * dt)
        a_re = mag * jnp.cos(lam_im * dt)
        a_im = mag * jnp.sin(lam_im * dt)
        den = lam_re * lam_re + lam_im * lam_im
        coef_re = ((a_re - 1.0) * lam_re + a_im * lam_im) / den
        coef_im = (a_im * lam_re - (a_re - 1.0) * lam_im) / den
        bb = _cmul((coef_re, coef_im), (b_re_ref[...], b_im_ref[...]))
        a_pow = (a_re, a_im)
        pw = (jnp.ones((P, W), F32), jnp.zeros((P, W), F32))
        for j in range(4):
            bit = ((kidx >> j) & 1) == 1
            pw = _cmul(pw, (jnp.where(bit, a_pow[0], 1.0), jnp.where(bit, a_pow[1], 0.0)))
            a_pow = _cmul(a_pow, a_pow)
        for j in range(S5_LEVELS):
            base = 4 * j + 2 * direction
            ap = jnp.where(lane_ap == base, a_pow[0], ap)
            ap = jnp.where(lane_ap == base + 1, a_pow[1], ap)
            a_pow = _cmul(a_pow, a_pow)
        w_re, w_im = _cmul(pw, bb)
        ca_re, ca_im = _cmul(_cmul(pw, (a_re, a_im)), (ct_re_ref[...], ct_im_ref[...]))
        hp = lax.Precision.HIGHEST
        kmat = (jnp.dot(c_re, w_re, precision=hp, preferred_element_type=F32)
                - jnp.dot(c_im, w_im, precision=hp, preferred_element_type=F32))
        o_w_re, o_w_im, o_k, o_ca_re, o_ca_im = outs[direction]
        o_w_re[...] = w_re
        o_w_im[...] = w_im
        o_k[...] = kmat
        o_ca_re[...] = ca_re
        o_ca_im[...] = -ca_im
    row = lax.broadcasted_iota(jnp.int32, (S5_GROUP, W), 0)
    lane = lax.broadcasted_iota(jnp.int32, (S5_GROUP, W), 1)
    lag0 = kb_ref[...] + jnp.where(row == lane, d_ref[...], 0.0)
    kf_ref[...] = kf_ref[...] + jnp.where(lane < S5_GROUP, lag0, 0.0)
    ap_ref[...] = ap


def _s5_tables(lam_re_f, lam_im_f, log_dt_f, lam_re_b, lam_im_b, log_dt_b, b_re, b_im, c_re, c_im, d_skip):
    G, P, C, L, W = S5_GROUPS, S5_STATE, S5_GROUP, S5_CHUNK, S5_CHUNK * S5_GROUP
    lam = jnp.stack([lam_re_f, lam_im_f, lam_re_b, lam_im_b], axis=-1)
    dts = jnp.stack([log_dt_f, log_dt_b], axis=-1).reshape(G, 1, 2)
    tile_k = lambda a: jnp.tile(a, (1, 1, L))
    ct = lambda a: tile_k(jnp.swapaxes(a, 1, 2))
    grp = lambda r, c: pl.BlockSpec((None, r, c), lambda g: (g, 0, 0))
    pw_out = jax.ShapeDtypeStruct((G, P, W), F32)
    k_out = jax.ShapeDtypeStruct((G, C, W), F32)
    wf_re, wf_im, wb_re, wb_im, kf, kb, caf_re, caf_im, cab_re, cab_im, ap = pl.pallas_call(
        _s5_param_kernel,
        grid=(G,),
        in_specs=[grp(P, 4), grp(1, 2), grp(P, W), grp(P, W), grp(P, W), grp(P, W), grp(C, P), grp(C, P), grp(C, 1)],
        out_specs=[grp(P, W)] * 4 + [grp(C, W)] * 2 + [grp(P, W)] * 4 + [grp(P, 128)],
        out_shape=[pw_out] * 4 + [k_out] * 2 + [pw_out] * 4 + [jax.ShapeDtypeStruct((G, P, 128), F32)],
        compiler_params=_params(("parallel",), 32),
        name="s5_params",
    )(lam, dts, tile_k(b_re), tile_k(b_im), ct(c_re), ct(c_im), c_re, c_im, d_skip.reshape(G, C, 1))

    NG, J, hp = S5_LANE_GROUPS, G // S5_LANE_GROUPS, lax.Precision.HIGHEST
    order = _s5_token_order()
    place = (order[:, :, None] == np.arange(L)).astype(np.float32)
    place_rev = (order[:, :, None] == L - 1 - np.arange(L)).astype(np.float32)
    lag = order[:, None, :] - order[:, :, None] + (L - 1)
    lag_hot = (lag[..., None] == np.arange(2 * L - 1)).astype(np.float32)
    lag_tab = jnp.concatenate([kb.reshape(G, C, L, C)[:, :, :0:-1], kf.reshape(G, C, L, C)], axis=2)
    toep = jnp.einsum("xstl,jxoli->jxsito", lag_hot, lag_tab.reshape(J, NG, C, 2 * L - 1, C), precision=hp)
    toep = toep.reshape(G, W, W)

    def slots(a, hot):
        return jnp.einsum("xqk,jxpkc->jxpqc", hot, a.reshape(J, NG, P, L, C), precision=hp).reshape(G, P, W)

    t_ = lambda a: jnp.swapaxes(a, 1, 2)
    sf_re, sf_im = t_(slots(wf_re, place_rev)), t_(slots(wf_im, place_rev))
    sb_re, sb_im = t_(slots(wb_re, place)), t_(slots(wb_im, place))
    w1 = jnp.concatenate([toep, sf_re, sf_im, sf_im, sf_re, sb_re, sb_im, sb_im, sb_re], axis=2)

    def pair_rows(a):
        a = a.reshape(G // 2, 2, P, W)
        zero = jnp.zeros_like(a[:, 0])
        return jnp.concatenate([jnp.concatenate([a[:, 0], zero], axis=2),
                                jnp.concatenate([zero, a[:, 1]], axis=2)], axis=1)

    w2 = jnp.concatenate([pair_rows(slots(caf_re, place)), pair_rows(slots(caf_im, place)),
                          pair_rows(slots(cab_re, place_rev)), pair_rows(slots(cab_im, place_rev))], axis=1)
    apj = ap[:, :, :4 * S5_LEVELS].reshape(G // 2, 2, P, S5_LEVELS, 4).transpose(0, 3, 4, 1, 2)
    mult = apj.reshape(G // 2, S5_LEVELS, 4 * 2 * P)
    return w1.astype(BF16), w2.astype(BF16), mult


def _s5_token_order():
    NG = S5_LANE_GROUPS
    order = np.empty((NG, S5_CHUNK), np.int64)
    for gl in range(NG):
        for half in range(S5_CHUNK // NG):
            for blk in range(NG):
                order[gl, half * NG + blk] = half * NG + (blk - gl) % NG
    return order


def _s5_kernel(u_ref, w1_ref, w2_ref, mult_ref, y_ref, ucbuf, ybuf, *, nch, nlev):
    row = lax.broadcasted_iota(jnp.int32, (nch, 128), 0)
    lane_blk = lax.broadcasted_iota(jnp.int32, (nch, 128), 1) // S5_GROUP
    W, L, C, NG = S5_CHUNK * S5_GROUP, S5_CHUNK, S5_GROUP, S5_LANE_GROUPS

    def pick(slabs, shift):
        acc = slabs[(0 - shift) % NG]
        for blk in range(1, NG):
            acc = jnp.where(lane_blk == blk, slabs[(blk - shift) % NG], acc)
        return acc

    def shift_down(x, s):
        if s % 8 == 0:
            return jnp.concatenate([jnp.zeros((s, 128), F32), x[:nch - s]], axis=0)
        return jnp.where(row >= s, pltpu.roll(x, s, 0), 0.0)

    def shift_up(x, s):
        if s % 8 == 0:
            return jnp.concatenate([x[s:], jnp.zeros((s, 128), F32)], axis=0)
        return jnp.where(row < nch - s, pltpu.roll(x, nch - s, 0), 0.0)

    for half in range(L // NG):
        rolled = []
        for k in range(NG):
            slab = u_ref[pl.ds(half * NG + k, nch, stride=L), :]
            rolled.append(pltpu.roll(slab, k * C, 1) if k else slab)
        for gl in range(NG):
            ucbuf[gl, :, half * 128:(half + 1) * 128] = pick(rolled, gl).astype(BF16)

    first_group = lax.broadcasted_iota(jnp.int32, (nch, 128), 1) < S5_STATE

    for m in range(NG // 2):
        r0 = _dot(ucbuf[2 * m], w1_ref[2 * m])
        r1 = _dot(ucbuf[2 * m + 1], w1_ref[2 * m + 1])
        xf_re = jnp.where(first_group, r0[:, W:W + 128], r1[:, W + 128:W + 256])
        xf_im = jnp.where(first_group, r0[:, W + 128:W + 256], r1[:, W:W + 128])
        xb_re = jnp.where(first_group, r0[:, W + 256:W + 384], r1[:, W + 384:W + 512])
        xb_im = jnp.where(first_group, r0[:, W + 384:W + 512], r1[:, W + 256:W + 384])
        mult = mult_ref[m]
        for j in range(nlev):
            s = 1 << j
            a_re, a_im = mult[j:j + 1, 0:128], mult[j:j + 1, 128:256]
            s_re, s_im = shift_down(xf_re, s), shift_down(xf_im, s)
            xf_re, xf_im = xf_re + (a_re * s_re - a_im * s_im), xf_im + (a_re * s_im + a_im * s_re)
            a_re, a_im = mult[j:j + 1, 256:384], mult[j:j + 1, 384:512]
            s_re, s_im = shift_up(xb_re, s), shift_up(xb_im, s)
            xb_re, xb_im = xb_re + (a_re * s_re - a_im * s_im), xb_im + (a_re * s_im + a_im * s_re)
        xs = jnp.concatenate([shift_down(xf_re, 1), shift_down(xf_im, 1), shift_up(xb_re, 1), shift_up(xb_im, 1)],
                             axis=1).astype(BF16)
        y = _dot(xs, w2_ref[m])
        ybuf[2 * m] = r0[:, 0:W] + y[:, 0:W]
        ybuf[2 * m + 1] = r1[:, 0:W] + y[:, W:]

    for half in range(L // NG):
        for k in range(NG):
            slab = pick([ybuf[gl, :, half * 128:(half + 1) * 128] for gl in range(NG)], k)
            y_ref[pl.ds(half * NG + k, nch, stride=L), :] = pltpu.roll(slab, (NG - k) * C, 1) if k else slab


def _s5(u, w1, w2, mult):
    B, T, D = u.shape
    nch, W, NG = T // S5_CHUNK, S5_CHUNK * S5_GROUP, S5_LANE_GROUPS
    nlev = int(math.log2(nch))
    assert 1 << nlev == nch and nlev <= S5_LEVELS
    tok = pl.BlockSpec((None, T, 128), lambda b, j: (b, 0, j))
    return pl.pallas_call(
        functools.partial(_s5_kernel, nch=nch, nlev=nlev),
        grid=(B, D // 128),
        in_specs=[tok,
                  pl.BlockSpec((NG, W, 3 * W), lambda b, j: (j, 0, 0)),
                  pl.BlockSpec((NG // 2, 2 * W, 2 * W), lambda b, j: (j, 0, 0)),
                  pl.BlockSpec((NG // 2, S5_LEVELS, 2 * W), lambda b, j: (j, 0, 0))],
        out_specs=tok,
        out_shape=jax.ShapeDtypeStruct((B, T, D), F32),
        scratch_shapes=[pltpu.VMEM((NG, nch, W), BF16), pltpu.VMEM((NG, nch, W), F32)],
        compiler_params=_params(("parallel", "parallel"), 32),
        name="s5_scan",
    )(u, w1, w2, mult)


def _attn_bias_table():
    i = np.arange(ATTN_QB)[:, None]
    j = np.arange(ATTN_KW)[None, :]
    rel = j - ATTN_R - i
    slopes = 2.0 ** (-8.0 * np.arange(1, N_HEADS + 1, dtype=np.float64) / N_HEADS)
    tab = np.empty((N_HEADS // 2, len(DILATED_CONFIGS), 4, 2, ATTN_QB, ATTN_KW), np.float32)
    for bi, (_, d) in enumerate(DILATED_CONFIGS):
        for ty in range(4):
            valid = np.abs(rel) <= ATTN_R
            if ty & 1:
                valid = valid & (j >= ATTN_R)
            if ty & 2:
                valid = valid & (j < ATTN_QB + ATTN_R)
            for h in range(N_HEADS):
                dist = (np.abs(rel) * d).astype(np.float32)
                alibi = (-np.float32(slopes[h])) * dist
                tab[h // 2, bi, ty, h % 2] = np.where(valid, alibi, np.float32(NEG_BIG))
    return tab.reshape(N_HEADS // 2, len(DILATED_CONFIGS) * 4, 2 * ATTN_QB, ATTN_KW)


def _attn_kernel(q_ref, k_ref, v_ref, bias_ref, o_ref, kp, vp, acc_o, acc_m, acc_l, *, T):
    zeros = jnp.zeros((ATTN_PAD, 128), F32)
    for buf, src in ((kp, k_ref), (vp, v_ref)):
        buf[0:ATTN_PAD, :] = zeros
        buf[ATTN_PAD + T:ATTN_PAD + T + ATTN_PAD, :] = zeros
        buf[ATTN_PAD:ATTN_PAD + T, :] = src[...]
    first_head = lax.broadcasted_iota(jnp.int32, (ATTN_QB, 128), 1) < HEAD_DIM
    ones = jnp.ones((ATTN_KW, 128), BF16)
    n_branch = len(DILATED_CONFIGS)

    def block_softmax(bi, d, nblk, idx):
        r = idx % d
        blk = idx // d
        q_start = r + d * ATTN_QB * blk
        rows = pl.ds(q_start, ATTN_QB, stride=d)
        keys = pl.ds(ATTN_PAD + q_start - ATTN_R * d, ATTN_KW, stride=d)
        ty = jnp.where(blk == 0, 1, 0) + jnp.where(blk == nblk - 1, 2, 0)
        qf = q_ref[rows, :]
        q2 = jnp.concatenate([jnp.where(first_head, qf, 0.0), jnp.where(first_head, 0.0, qf)], axis=0).astype(BF16)
        kw = kp[keys, :].astype(BF16)
        s = lax.dot_general(q2, kw, (((1,), (1,)), ((), ())), preferred_element_type=F32)
        s = s + bias_ref[bi * 4 + ty]
        m = jnp.max(s, axis=-1, keepdims=True)
        p = jnp.exp(s - m).astype(BF16)
        ov = _dot(p, jnp.concatenate([vp[keys, :].astype(BF16), ones], axis=1))
        o_new = jnp.where(first_head, ov[0:ATTN_QB, 0:128], ov[ATTN_QB:, 0:128])
        l_new = jnp.where(first_head, ov[0:ATTN_QB, 128:], ov[ATTN_QB:, 128:])
        m_new = jnp.where(first_head, m[0:ATTN_QB], m[ATTN_QB:])
        return rows, o_new, m_new, l_new

    for bi, (_, d) in enumerate(DILATED_CONFIGS):
        nblk = T // d // ATTN_QB

        def group(it, carry, bi=bi, d=d, nblk=nblk):
            new = [block_softmax(bi, d, nblk, it * ATTN_UNROLL + u) for u in range(ATTN_UNROLL)]
            if bi == 0:
                for rows, o_new, m_new, l_new in new:
                    acc_o[rows, :] = o_new
                    acc_m[rows, :] = m_new
                    acc_l[rows, :] = l_new
                return carry
            old = [(acc_o[rows, :], acc_m[rows, :], acc_l[rows, :]) for rows, _, _, _ in new]
            merged = []
            for (rows, o_new, m_new, l_new), (o_old, m_old, l_old) in zip(new, old):
                e = jnp.exp(-jnp.abs(m_old - m_new))
                keep = m_old >= m_new
                a = jnp.where(keep, 1.0, e)
                b = jnp.where(keep, e, 1.0)
                merged.append((rows, a * o_old + b * o_new, jnp.maximum(m_old, m_new), a * l_old + b * l_new))
            for rows, o_tot, m_tot, l_tot in merged:
                if bi == n_branch - 1:
                    o_ref[rows, :] = o_tot / l_tot
                else:
                    acc_o[rows, :] = o_tot
                    acc_m[rows, :] = m_tot
                    acc_l[rows, :] = l_tot
            return carry

        lax.fori_loop(0, d * nblk // ATTN_UNROLL, group, 0)


def _attention(q, k, v, bias):
    B, T, _ = q.shape
    assert T % (ATTN_QB * DILATED_CONFIGS[-1][1]) == 0
    pair = pl.BlockSpec((None, T, 128), lambda b, p: (b, 0, p))
    nb = bias.shape[1]
    return pl.pallas_call(
        functools.partial(_attn_kernel, T=T),
        grid=(B, N_HEADS // 2),
        in_specs=[pair, pair, pair, pl.BlockSpec((None, nb, 2 * ATTN_QB, ATTN_KW), lambda b, p: (p, 0, 0, 0))],
        out_specs=pair,
        out_shape=jax.ShapeDtypeStruct((B, T, D_B), F32),
        scratch_shapes=[pltpu.VMEM((T + 2 * ATTN_PAD, 128), F32), pltpu.VMEM((T + 2 * ATTN_PAD, 128), F32),
                        pltpu.VMEM((T, 128), F32), pltpu.VMEM((T, 128), F32), pltpu.VMEM((T, 128), F32)],
        compiler_params=_params(("parallel", "arbitrary"), 48),
        name="dilated_attn",
    )(q, k, v, bias)


def _ab_out_kernel(x_ref, ya_ref, yb_ref, mod_ref, wglu_ref, bglu_ref, wo_ref, o_ref):
    y = ya_ref[...]
    y = 0.5 * y * (1.0 + jnp.tanh(math.sqrt(2.0 / math.pi) * (y + 0.044715 * (y * y * y))))
    y = y * _sigmoid(_dot(y.astype(BF16), wglu_ref[...]) + bglu_ref[...])
    out = _dot(y.astype(BF16), wo_ref[0:D_A, :]) + _dot(yb_ref[...].astype(BF16), wo_ref[D_A:, :])
    o_ref[...] = x_ref[...] + mod_ref[2:3, :] * out


def _ab_out(x, ya, yb, mod, w_glu, b_glu, w_out, tm=512):
    B, T, D = x.shape
    half = pl.BlockSpec((None, tm, D_A), lambda b, i: (b, i, 0))
    full = pl.BlockSpec((None, tm, D), lambda b, i: (b, i, 0))
    return pl.pallas_call(
        _ab_out_kernel,
        grid=(B, T // tm),
        in_specs=[full, half, half, pl.BlockSpec((None, 6, D), lambda b, i: (b, 0, 0)),
                  _const_spec((D_A, D_A)), _const_spec((1, D_A)), _const_spec((D, D))],
        out_specs=full,
        out_shape=jax.ShapeDtypeStruct((B, T, D), F32),
        compiler_params=_params(("parallel", "parallel"), 48),
        name="ab_out",
    )(x, ya, yb, mod, w_glu, b_glu, w_out)


def _halo_specs(tm, T, D):
    nh = T // HALO
    per = tm // HALO
    main = pl.BlockSpec((None, tm, D), lambda b, i: (b, i, 0))
    prev = pl.BlockSpec((None, HALO, D), lambda b, i: (b, jnp.maximum(i * per - 1, 0), 0))
    nxt = pl.BlockSpec((None, HALO, D), lambda b, i: (b, jnp.minimum((i + 1) * per, nh - 1), 0))
    return main, prev, nxt


def _conv3_rows(buf, w_ref, tm):
    return (buf[pl.ds(HALO - 1, tm), :] * w_ref[0:1, :] + buf[pl.ds(HALO, tm), :] * w_ref[1:2, :]
            + buf[pl.ds(HALO + 1, tm), :] * w_ref[2:3, :])


def _zero_sequence_edges(buf, tm):
    zero_row = jnp.zeros((1, buf.shape[1]), F32)

    @pl.when(pl.program_id(1) == 0)
    def _():
        buf[HALO - 1:HALO, :] = zero_row

    @pl.when(pl.program_id(1) == pl.num_programs(1) - 1)
    def _():
        buf[HALO + tm:HALO + tm + 1, :] = zero_row


def _ffn_kernel(x_ref, xp_ref, xn_ref, mod_ref, g_ref, wg_ref, wu_ref, cw_ref, cb_ref, wd_ref, o_ref, gbuf, abuf, *, tm):
    x = x_ref[...]
    g, shift, scale = g_ref[...], mod_ref[3:4, :], mod_ref[4:5, :]
    has_prev = jnp.where(pl.program_id(1) > 0, 1.0, 0.0)
    has_next = jnp.where(pl.program_id(1) < pl.num_programs(1) - 1, 1.0, 0.0)
    h = _norm_mod(x, g, shift, scale)
    he = jnp.concatenate([_norm_mod(xp_ref[...], g, shift, scale) * has_prev, h,
                          _norm_mod(xn_ref[...], g, shift, scale) * has_next], axis=0).astype(BF16)
    h = he[HALO:HALO + tm]
    for c in range(D_FF // FFN_CHUNK):
        cols = slice(c * FFN_CHUNK, (c + 1) * FFN_CHUNK)
        gbuf[:, cols] = _dot(he, wg_ref[:, cols])
        gate = (gbuf[pl.ds(HALO - 1, tm), cols] * cw_ref[0:1, cols] + gbuf[pl.ds(HALO, tm), cols] * cw_ref[1:2, cols]
                + gbuf[pl.ds(HALO + 1, tm), cols] * cw_ref[2:3, cols] + cb_ref[:, cols])
        up = _dot(h, wu_ref[:, cols])
        abuf[:, cols] = ((gate * _sigmoid(gate)) * up).astype(BF16)
    o_ref[...] = x + mod_ref[5:6, :] * _dot(abuf[...], wd_ref[...])


def _ffn(x, mod, g, w_gate, w_up, conv_w, conv_b, w_down, tm=512):
    B, T, D = x.shape
    main, prev, nxt = _halo_specs(tm, T, D)
    return pl.pallas_call(
        functools.partial(_ffn_kernel, tm=tm),
        grid=(B, T // tm),
        in_specs=[main, prev, nxt, pl.BlockSpec((None, 6, D), lambda b, i: (b, 0, 0)), _const_spec((1, D)),
                  _const_spec((D, D_FF)), _const_spec((D, D_FF)), _const_spec((3, D_FF)), _const_spec((1, D_FF)),
                  _const_spec((D_FF, D))],
        out_specs=main,
        out_shape=jax.ShapeDtypeStruct((B, T, D), F32),
        scratch_shapes=[pltpu.VMEM((tm + 2 * HALO, D_FF), F32), pltpu.VMEM((tm, D_FF), BF16)],
        compiler_params=_params(("parallel", "arbitrary"), 56),
        name="conv_ffn",
    )(x, x, x, mod, g, w_gate, w_up, conv_w, conv_b, w_down)


def _cd_in_kernel(x_ref, xp_ref, xn_ref, mod_ref, g_ref, w_ref, cbd_ref, sbd_ref, sw_ref, ab_ref, yd_ref, cbuf, *, tm):
    xe = jnp.concatenate([xp_ref[...], x_ref[...], xn_ref[...]], axis=0)
    he = _norm_mod(xe, g_ref[...], mod_ref[0:1, :], mod_ref[1:2, :]).astype(BF16)
    z = _dot(he, w_ref[...])
    cbuf[...] = z[:, D_C + 2 * D_D:] * z[:, D_C:D_C + D_D]
    _zero_sequence_edges(cbuf, tm)
    yd_ref[...] = z[HALO:HALO + tm, D_C + D_D:D_C + 2 * D_D] * _conv3_rows(cbuf, sw_ref, tm)
    uc = z[HALO:HALO + tm, 0:D_C].astype(BF16)
    ab_ref[0] = _dot(uc, cbd_ref[...]).astype(BF16)
    ab_ref[1] = _dot(uc, sbd_ref[...]).astype(BF16)


def _cd_in(x, mod, g, w_in, cbd, sbd, sconv_w, tm=512):
    B, T, D = x.shape
    main, prev, nxt = _halo_specs(tm, T, D)
    return pl.pallas_call(
        functools.partial(_cd_in_kernel, tm=tm),
        grid=(B, T // tm),
        in_specs=[main, prev, nxt, pl.BlockSpec((None, 6, D), lambda b, i: (b, 0, 0)), _const_spec((1, D)),
                  _const_spec((D, w_in.shape[1])), _const_spec((D_C, D_C)), _const_spec((D_C, D_C)),
                  _const_spec((3, D_D))],
        out_specs=[pl.BlockSpec((None, 2, tm, D_C), lambda b, i: (b, 0, i, 0)),
                   pl.BlockSpec((None, tm, D_D), lambda b, i: (b, i, 0))],
        out_shape=[jax.ShapeDtypeStruct((B, 2, T, D_C), BF16), jax.ShapeDtypeStruct((B, T, D_D), F32)],
        scratch_shapes=[pltpu.VMEM((tm + 2 * HALO, D_D), F32)],
        compiler_params=_params(("parallel", "arbitrary"), 48),
        name="cd_in",
    )(x, x, x, mod, g, w_in, cbd, sbd, sconv_w)


def _cd_out_kernel(dft_ref, ab_ref, yd_ref, x_ref, mod_ref, wo_ref, o_ref, fold, *, T, tf):
    half = T // 2

    @pl.when(pl.program_id(1) == 0)
    def _():
        r = lax.broadcasted_iota(jnp.int32, (tf, tf + HALO), 0)
        c = lax.broadcasted_iota(jnp.int32, (tf, tf + HALO), 1)
        mirror = jnp.where(c == tf - r, 1.0, 0.0).astype(BF16)
        first_row = lax.broadcasted_iota(jnp.int32, (tf, D_C), 0) == 0
        for part, sign in ((0, 1.0), (1, -1.0)):
            base = part * T
            for s0 in range(0, half, tf):
                nxt = ab_ref[base + T - s0:base + T - s0 + HALO, :] if s0 else jnp.zeros((HALO, D_C), BF16)
                src = jnp.concatenate([ab_ref[base + T - s0 - tf:base + T - s0, :], nxt], axis=0)
                folded = ab_ref[base + s0:base + s0 + tf, :].astype(F32) + sign * _dot(mirror, src)
                if part == 1 and s0 == 0:
                    folded = jnp.where(first_row, ab_ref[half:half + HALO, :].astype(F32)[0:1, :], folded)
                fold[part * half + s0:part * half + s0 + tf, :] = folded.astype(BF16)

    yc = _dot(dft_ref[...], fold[...])
    out = _dot(yc.astype(BF16), wo_ref[0:D_C, :]) + _dot(yd_ref[...].astype(BF16), wo_ref[D_C:, :])
    o_ref[...] = x_ref[...] + mod_ref[2:3, :] * out


def _cd_out(dft, ab, yd, x, mod, w_out, tm=256):
    B, T, D = x.shape
    return pl.pallas_call(
        functools.partial(_cd_out_kernel, T=T, tf=256),
        grid=(B, T // tm),
        in_specs=[pl.BlockSpec((tm, T), lambda b, i: (i, 0)),
                  pl.BlockSpec((None, 2 * T, D_C), lambda b, i: (b, 0, 0)),
                  pl.BlockSpec((None, tm, D_D), lambda b, i: (b, i, 0)),
                  pl.BlockSpec((None, tm, D), lambda b, i: (b, i, 0)),
                  pl.BlockSpec((None, 6, D), lambda b, i: (b, 0, 0)), _const_spec((D, D))],
        out_specs=pl.BlockSpec((None, tm, D), lambda b, i: (b, i, 0)),
        out_shape=jax.ShapeDtypeStruct((B, T, D), F32),
        scratch_shapes=[pltpu.VMEM((T, D_C), BF16)],
        compiler_params=_params(("parallel", "arbitrary"), 56),
        name="cd_out",
    )(dft, ab, yd, x, mod, w_out)


def _seq_dft_matrix(T):
    t = jnp.arange(T, dtype=jnp.int32)
    s = jnp.arange(T // 2, dtype=jnp.int32)
    ang = ((t[:, None] * s[None, :]) % T).astype(F32) * (2.0 * math.pi / T)
    scale = 1.0 / math.sqrt(T)
    nyquist = jnp.where(t % 2 == 0, scale, -scale)[:, None]
    sin_part = jnp.where(s[None, :] == 0, nyquist, jnp.sin(ang) * (-scale))
    return jnp.concatenate([jnp.cos(ang) * scale, sin_part], axis=1).astype(BF16)


def _channel_dft_matrices():
    c = np.arange(D_C)
    ang = 2.0 * np.pi * ((c[:, None] % FNET_GROUP_DIM) * (c[None, :] % FNET_GROUP_DIM) % FNET_GROUP_DIM) / FNET_GROUP_DIM
    same = (c[:, None] // FNET_GROUP_DIM) == (c[None, :] // FNET_GROUP_DIM)
    scale = 1.0 / math.sqrt(FNET_GROUP_DIM)
    cbd = np.where(same, np.cos(ang) * scale, 0.0).astype(np.float32)
    sbd = np.where(same, np.sin(ang) * scale, 0.0).astype(np.float32)
    return jnp.asarray(cbd).astype(BF16), jnp.asarray(sbd).astype(BF16)


def _trunk(x, mods, wts):
    B, T, D = x.shape
    nch = T // S5_CHUNK
    mod = mods[0]
    u, q, k, v = _ab_in(x, mod, wts["norm_mix_g"][0], wts["ab_w_in"], wts["hsum"], wts["q_g"], wts["k_g"])
    ya = _s5(u, wts["s5_w1"], wts["s5_w2"], wts["s5_mult"])
    yb = _attention(q, k, v, wts["attn_bias"])
    x = _ab_out(x, ya, yb, mod, wts["s5_w_glu"], wts["s5_b_glu"], wts["ab_w_out"])
    x = _ffn(x, mod, wts["norm_ffn_g"][0], *wts["ffn"][0])
    mod = mods[1]
    ab, yd = _cd_in(x, mod, wts["norm_mix_g"][1], wts["cd_w_in"], wts["cbd"], wts["sbd"], wts["sconv_w"])
    x = _cd_out(_seq_dft_matrix(T), ab.reshape(B, 2 * T, D_C), yd, x, mod, wts["cd_w_out"])
    x = _ffn(x, mod, wts["norm_ffn_g"][1], *wts["ffn"][1])
    return x


def kernel(x_prompt, x_sample, c_prompt, c_sample, ada_w, ada_b, norm_mix_g, norm_ffn_g, ffn_w_gate, ffn_w_up, ffn_conv_w, ffn_conv_b, ffn_w_down, ab_w_in, ab_w_out, s5_lam_re_f, s5_lam_im_f, s5_log_dt_f, s5_lam_re_b, s5_lam_im_b, s5_log_dt_b, s5_b_re, s5_b_im, s5_c_re, s5_c_im, s5_d, s5_w_glu, s5_b_glu, q_norm_g, k_norm_g, cd_w_in, cd_w_out, sconv_w):
    depth = ada_w.shape[0]
    assert depth == 2 and ab_w_in.shape[0] == 1 and cd_w_in.shape[0] == 1
    bp = x_prompt.shape[0]
    mod_all = _ada(jnp.concatenate([c_prompt, c_sample], axis=0), ada_w, ada_b)
    s5_w1, s5_w2, s5_mult = _s5_tables(s5_lam_re_f[0], s5_lam_im_f[0], s5_log_dt_f[0], s5_lam_re_b[0],
                                       s5_lam_im_b[0], s5_log_dt_b[0], s5_b_re[0], s5_b_im[0],
                                       s5_c_re[0], s5_c_im[0], s5_d[0])
    head = np.arange(D_B) // HEAD_DIM
    cbd, sbd = _channel_dft_matrices()
    wts = dict(
        norm_mix_g=norm_mix_g.reshape(depth, 1, D_MODEL), norm_ffn_g=norm_ffn_g.reshape(depth, 1, D_MODEL),
        ab_w_in=ab_w_in[0].astype(BF16), ab_w_out=ab_w_out[0].astype(BF16),
        hsum=jnp.asarray(head[:, None] == head[None, :], BF16),
        q_g=jnp.tile(q_norm_g[0], N_HEADS).reshape(1, D_B), k_g=jnp.tile(k_norm_g[0], N_HEADS).reshape(1, D_B),
        s5_w1=s5_w1, s5_w2=s5_w2, s5_mult=s5_mult,
        s5_w_glu=s5_w_glu[0].astype(BF16), s5_b_glu=s5_b_glu[0].reshape(1, D_A),
        attn_bias=jnp.asarray(_attn_bias_table()),
        ffn=[(ffn_w_gate[l].astype(BF16), ffn_w_up[l].astype(BF16), ffn_conv_w[l], ffn_conv_b[l].reshape(1, D_FF),
              ffn_w_down[l].astype(BF16)) for l in range(depth)],
        cd_w_in=cd_w_in[0].astype(BF16), cd_w_out=cd_w_out[0].astype(BF16), cbd=cbd, sbd=sbd, sconv_w=sconv_w[0],
    )
    outs = []
    for x, rows in ((x_prompt, slice(0, bp)), (x_sample, slice(bp, None))):
        mods = [mod_all[l, rows].reshape(x.shape[0], 6, D_MODEL) for l in range(depth)]
        outs.append(_trunk(x, mods, wts))
    return tuple(outs)
```

```python
import functools
import math

import jax
import jax.numpy as jnp
import numpy as np
from jax import lax
from jax.experimental import pallas as pl
from jax.experimental.pallas import tpu as pltpu

F32 = jnp.float32
BF16 = jnp.bfloat16

D_MODEL = 1024
D_A = 512
S5_GROUP = 16
S5_GROUPS = 32
S5_STATE = 64
S5_CHUNK = 16
S5_LEVELS = 8
S5_LANE_GROUPS = 128 // S5_GROUP
D_B = 512
HEAD_DIM = 64
N_HEADS = 8
DILATED_CONFIGS = ((128, 1), (512, 4), (2048, 16))
ATTN_R = 64
ATTN_QB = 128
ATTN_KW = ATTN_QB + 2 * ATTN_R
ATTN_PAD = ATTN_R * 16
ATTN_UNROLL = 8
NEG_BIG = -1e30
D_C = 512
FNET_GROUP_DIM = 128
D_D = 512
D_FF = 2816
FFN_CHUNK = 256
EPS = 1e-6
ROW_BLOCK = 256
HALO = 16
MIB = 2 ** 20


def _params(sem, vmem_mib):
    return pltpu.CompilerParams(dimension_semantics=sem, vmem_limit_bytes=vmem_mib * MIB)


def _const_spec(shape):
    nd = len(shape)
    return pl.BlockSpec(shape, lambda *_: (0,) * nd, pipeline_mode=pl.Buffered(1))


def _dot(a, b):
    return jnp.dot(a, b, preferred_element_type=F32)


def _norm_mod(x, g, shift, scale):
    ms = jnp.mean(x * x, axis=-1, keepdims=True)
    y = x * lax.rsqrt(ms + EPS) * g
    return y * (1.0 + scale) + shift


def _sigmoid(x):
    return 1.0 / (1.0 + jnp.exp(-x))


def _ada_kernel(c_ref, w_ref, b_ref, o_ref):
    c = c_ref[...]
    cond = (c * _sigmoid(c)).astype(BF16)
    o_ref[...] = _dot(cond, w_ref[...].astype(BF16)) + b_ref[...]


def _ada(c_all, ada_w, ada_b):
    depth, d, n = ada_w.shape
    rows = c_all.shape[0]
    tn = 1536
    return pl.pallas_call(
        _ada_kernel,
        grid=(depth, n // tn),
        in_specs=[pl.BlockSpec((rows, d), lambda l, j: (0, 0)),
                  pl.BlockSpec((None, d, tn), lambda l, j: (l, 0, j)),
                  pl.BlockSpec((None, 1, tn), lambda l, j: (l, 0, j))],
        out_specs=pl.BlockSpec((None, rows, tn), lambda l, j: (l, 0, j)),
        out_shape=jax.ShapeDtypeStruct((depth, rows, n), F32),
        compiler_params=_params(("parallel", "parallel"), 32),
        name="ada_mod",
    )(c_all, ada_w, ada_b.reshape(depth, 1, n))


def _ab_in_kernel(x_ref, mod_ref, g_ref, w_ref, hsum_ref, qg_ref, kg_ref, u_ref, q_ref, k_ref, v_ref):
    hsum = hsum_ref[...]

    def head_norm(a, g):
        ms = _dot((a * a).astype(BF16), hsum) * (1.0 / HEAD_DIM)
        return a * lax.rsqrt(ms + EPS) * g

    for r0 in range(0, x_ref.shape[0], ROW_BLOCK):
        rows = slice(r0, r0 + ROW_BLOCK)
        h = _norm_mod(x_ref[rows, :], g_ref[...], mod_ref[0:1, :], mod_ref[1:2, :])
        z = _dot(h.astype(BF16), w_ref[...])
        u_ref[rows, :] = z[:, 0:D_A]
        q_ref[rows, :] = head_norm(z[:, D_A:D_A + D_B], qg_ref[...]) * (1.0 / math.sqrt(HEAD_DIM))
        k_ref[rows, :] = head_norm(z[:, D_A + D_B:D_A + 2 * D_B], kg_ref[...])
        v_ref[rows, :] = z[:, D_A + 2 * D_B:]


def _ab_in(x, mod, g, w_in, hsum, qg, kg, tm=512):
    B, T, D = x.shape
    n = w_in.shape[1]
    tok = pl.BlockSpec((None, tm, D_B), lambda b, i: (b, i, 0))
    out = jax.ShapeDtypeStruct((B, T, D_B), F32)
    return pl.pallas_call(
        _ab_in_kernel,
        grid=(B, T // tm),
        in_specs=[pl.BlockSpec((None, tm, D), lambda b, i: (b, i, 0)),
                  pl.BlockSpec((None, 6, D), lambda b, i: (b, 0, 0)),
                  _const_spec((1, D)), _const_spec((D, n)), _const_spec((D_B, D_B)),
                  _const_spec((1, D_B)), _const_spec((1, D_B))],
        out_specs=[tok, tok, tok, tok],
        out_shape=[out, out, out, out],
        compiler_params=_params(("parallel", "parallel"), 48),
        name="ab_in",
    )(x, mod, g, w_in, hsum, qg, kg)


def _cmul(a, b):
    return a[0] * b[0] - a[1] * b[1], a[0] * b[1] + a[1] * b[0]


def _s5_param_kernel(lam_ref, dt_ref, b_re_ref, b_im_ref, ct_re_ref, ct_im_ref, c_re_ref, c_im_ref, d_ref,
                     wf_re_ref, wf_im_ref, wb_re_ref, wb_im_ref, kf_ref, kb_ref,
                     caf_re_ref, caf_im_ref, cab_re_ref, cab_im_ref, ap_ref):
    P, W = S5_STATE, S5_CHUNK * S5_GROUP
    kidx = lax.broadcasted_iota(jnp.int32, (P, W), 1) // S5_GROUP
    lane_ap = lax.broadcasted_iota(jnp.int32, (P, 128), 1)
    ap = jnp.zeros((P, 128), F32)
    c_re, c_im = c_re_ref[...], c_im_ref[...]
    outs = ((wf_re_ref, wf_im_ref, kf_ref, caf_re_ref, caf_im_ref),
            (wb_re_ref, wb_im_ref, kb_ref, cab_re_ref, cab_im_ref))
    for direction in range(2):
        lam_re = lam_ref[:, 2 * direction:2 * direction + 1]
        lam_im = lam_ref[:, 2 * direction + 1:2 * direction + 2]
        dt = jnp.exp(dt_ref[:, direction:direction + 1])
        mag = jnp.exp(lam_re * dt)
        a_re = mag * jnp.cos(lam_im * dt)
        a_im = mag * jnp.sin(lam_im * dt)
        den = lam_re * lam_re + lam_im * lam_im
        coef_re = ((a_re - 1.0) * lam_re + a_im * lam_im) / den
        coef_im = (a_im * lam_re - (a_re - 1.0) * lam_im) / den
        bb = _cmul((coef_re, coef_im), (b_re_ref[...], b_im_ref[...]))
        a_pow = (a_re, a_im)
        pw = (jnp.ones((P, W), F32), jnp.zeros((P, W), F32))
        for j in range(4):
            bit = ((kidx >> j) & 1) == 1
            pw = _cmul(pw, (jnp.where(bit, a_pow[0], 1.0), jnp.where(bit, a_pow[1], 0.0)))
            a_pow = _cmul(a_pow, a_pow)
        for j in range(S5_LEVELS):
            base = 4 * j + 2 * direction
            ap = jnp.where(lane_ap == base, a_pow[0], ap)
            ap = jnp.where(lane_ap == base + 1, a_pow[1], ap)
            a_pow = _cmul(a_pow, a_pow)
        w_re, w_im = _cmul(pw, bb)
        ca_re, ca_im = _cmul(_cmul(pw, (a_re, a_im)), (ct_re_ref[...], ct_im_ref[...]))
        hp = lax.Precision.HIGHEST
        kmat = (jnp.dot(c_re, w_re, precision=hp, preferred_element_type=F32)
                - jnp.dot(c_im, w_im, precision=hp, preferred_element_type=F32))
        o_w_re, o_w_im, o_k, o_ca_re, o_ca_im = outs[direction]
        o_w_re[...] = w_re
        o_w_im[...] = w_im
        o_k[...] = kmat
        o_ca_re[...] = ca_re
        o_ca_im[...] = -ca_im
    row = lax.broadcasted_iota(jnp.int32, (S5_GROUP, W), 0)
    lane = lax.broadcasted_iota(jnp.int32, (S5_GROUP, W), 1)
    lag0 = kb_ref[...] + jnp.where(row == lane, d_ref[...], 0.0)
    kf_ref[...] = kf_ref[...] + jnp.where(lane < S5_GROUP, lag0, 0.0)
    ap_ref[...] = ap


def _s5_tables(lam_re_f, lam_im_f, log_dt_f, lam_re_b, lam_im_b, log_dt_b, b_re, b_im, c_re, c_im, d_skip):
    G, P, C, L, W = S5_GROUPS, S5_STATE, S5_GROUP, S5_CHUNK, S5_CHUNK * S5_GROUP
    lam = jnp.stack([lam_re_f, lam_im_f, lam_re_b, lam_im_b], axis=-1)
    dts = jnp.stack([log_dt_f, log_dt_b], axis=-1).reshape(G, 1, 2)
    tile_k = lambda a: jnp.tile(a, (1, 1, L))
    ct = lambda a: tile_k(jnp.swapaxes(a, 1, 2))
    grp = lambda r, c: pl.BlockSpec((None, r, c), lambda g: (g, 0, 0))
    pw_out = jax.ShapeDtypeStruct((G, P, W), F32)
    k_out = jax.ShapeDtypeStruct((G, C, W), F32)
    wf_re, wf_im, wb_re, wb_im, kf, kb, caf_re, caf_im, cab_re, cab_im, ap = pl.pallas_call(
        _s5_param_kernel,
        grid=(G,),
        in_specs=[grp(P, 4), grp(1, 2), grp(P, W), grp(P, W), grp(P, W), grp(P, W), grp(C, P), grp(C, P), grp(C, 1)],
        out_specs=[grp(P, W)] * 4 + [grp(C, W)] * 2 + [grp(P, W)] * 4 + [grp(P, 128)],
        out_shape=[pw_out] * 4 + [k_out] * 2 + [pw_out] * 4 + [jax.ShapeDtypeStruct((G, P, 128), F32)],
        compiler_params=_params(("parallel",), 32),
        name="s5_params",
    )(lam, dts, tile_k(b_re), tile_k(b_im), ct(c_re), ct(c_im), c_re, c_im, d_skip.reshape(G, C, 1))

    NG, J, hp = S5_LANE_GROUPS, G // S5_LANE_GROUPS, lax.Precision.HIGHEST
    order = _s5_token_order()
    place = (order[:, :, None] == np.arange(L)).astype(np.float32)
    place_rev = (order[:, :, None] == L - 1 - np.arange(L)).astype(np.float32)
    lag = order[:, None, :] - order[:, :, None] + (L - 1)
    lag_hot = (lag[..., None] == np.arange(2 * L - 1)).astype(np.float32)
    lag_tab = jnp.concatenate([kb.reshape(G, C, L, C)[:, :, :0:-1], kf.reshape(G, C, L, C)], axis=2)
    toep = jnp.einsum("xstl,jxoli->jxsito", lag_hot, lag_tab.reshape(J, NG, C, 2 * L - 1, C), precision=hp)
    toep = toep.reshape(G, W, W)

    def slots(a, hot):
        return jnp.einsum("xqk,jxpkc->jxpqc", hot, a.reshape(J, NG, P, L, C), precision=hp).reshape(G, P, W)

    t_ = lambda a: jnp.swapaxes(a, 1, 2)
    sf_re, sf_im = t_(slots(wf_re, place_rev)), t_(slots(wf_im, place_rev))
    sb_re, sb_im = t_(slots(wb_re, place)), t_(slots(wb_im, place))
    w1 = jnp.concatenate([toep, sf_re, sf_im, sf_im, sf_re, sb_re, sb_im, sb_im, sb_re], axis=2)

    def pair_rows(a):
        a = a.reshape(G // 2, 2, P, W)
        zero = jnp.zeros_like(a[:, 0])
        return jnp.concatenate([jnp.concatenate([a[:, 0], zero], axis=2),
                                jnp.concatenate([zero, a[:, 1]], axis=2)], axis=1)

    w2 = jnp.concatenate([pair_rows(slots(caf_re, place)), pair_rows(slots(caf_im, place)),
                          pair_rows(slots(cab_re, place_rev)), pair_rows(slots(cab_im, place_rev))], axis=1)
    apj = ap[:, :, :4 * S5_LEVELS].reshape(G // 2, 2, P, S5_LEVELS, 4).transpose(0, 3, 4, 1, 2)
    mult = apj.reshape(G // 2, S5_LEVELS, 4 * 2 * P)
    return w1.astype(BF16), w2.astype(BF16), mult


def _s5_token_order():
    NG = S5_LANE_GROUPS
    order = np.empty((NG, S5_CHUNK), np.int64)
    for gl in range(NG):
        for half in range(S5_CHUNK // NG):
            for blk in range(NG):
                order[gl, half * NG + blk] = half * NG + (blk - gl) % NG
    return order


def _s5_kernel(u_ref, w1_ref, w2_ref, mult_ref, y_ref, ucbuf, ybuf, *, nch, nlev):
    row = lax.broadcasted_iota(jnp.int32, (nch, 128), 0)
    lane_blk = lax.broadcasted_iota(jnp.int32, (nch, 128), 1) // S5_GROUP
    W, L, C, NG = S5_CHUNK * S5_GROUP, S5_CHUNK, S5_GROUP, S5_LANE_GROUPS

    def pick(slabs, shift):
        acc = slabs[(0 - shift) % NG]
        for blk in range(1, NG):
            acc = jnp.where(lane_blk == blk, slabs[(blk - shift) % NG], acc)
        return acc

    def shift_down(x, s):
        if s % 8 == 0:
            return jnp.concatenate([jnp.zeros((s, 128), F32), x[:nch - s]], axis=0)
        return jnp.where(row >= s, pltpu.roll(x, s, 0), 0.0)

    def shift_up(x, s):
        if s % 8 == 0:
            return jnp.concatenate([x[s:], jnp.zeros((s, 128), F32)], axis=0)
        return jnp.where(row < nch - s, pltpu.roll(x, nch - s, 0), 0.0)

    for half in range(L // NG):
        rolled = []
        for k in range(NG):
            slab = u_ref[pl.ds(half * NG + k, nch, stride=L), :]
            rolled.append(pltpu.roll(slab, k * C, 1) if k else slab)
        for gl in range(NG):
            ucbuf[gl, :, half * 128:(half + 1) * 128] = pick(rolled, gl).astype(BF16)

    first_group = lax.broadcasted_iota(jnp.int32, (nch, 128), 1) < S5_STATE

    for m in range(NG // 2):
        r0 = _dot(ucbuf[2 * m], w1_ref[2 * m])
        r1 = _dot(ucbuf[2 * m + 1], w1_ref[2 * m + 1])
        xf_re = jnp.where(first_group, r0[:, W:W + 128], r1[:, W + 128:W + 256])
        xf_im = jnp.where(first_group, r0[:, W + 128:W + 256], r1[:, W:W + 128])
        xb_re = jnp.where(first_group, r0[:, W + 256:W + 384], r1[:, W + 384:W + 512])
        xb_im = jnp.where(first_group, r0[:, W + 384:W + 512], r1[:, W + 256:W + 384])
        mult = mult_ref[m]
        for j in range(nlev):
            s = 1 << j
            a_re, a_im = mult[j:j + 1, 0:128], mult[j:j + 1, 128:256]
            s_re, s_im = shift_down(xf_re, s), shift_down(xf_im, s)
            xf_re, xf_im = xf_re + (a_re * s_re - a_im * s_im), xf_im + (a_re * s_im + a_im * s_re)
            a_re, a_im = mult[j:j + 1, 256:384], mult[j:j + 1, 384:512]
            s_re, s_im = shift_up(xb_re, s), shift_up(xb_im, s)
            xb_re, xb_im = xb_re + (a_re * s_re - a_im * s_im), xb_im + (a_re * s_im + a_im * s_re)
        xs = jnp.concatenate([shift_down(xf_re, 1), shift_down(xf_im, 1), shift_up(xb_re, 1), shift_up(xb_im, 1)],
                             axis=1).astype(BF16)
        y = _dot(xs, w2_ref[m])
        ybuf[2 * m] = r0[:, 0:W] + y[:, 0:W]
        ybuf[2 * m + 1] = r1[:, 0:W] + y[:, W:]

    for half in range(L // NG):
        for k in range(NG):
            slab = pick([ybuf[gl, :, half * 128:(half + 1) * 128] for gl in range(NG)], k)
            y_ref[pl.ds(half * NG + k, nch, stride=L), :] = pltpu.roll(slab, (NG - k) * C, 1) if k else slab


def _s5(u, w1, w2, mult):
    B, T, D = u.shape
    nch, W, NG = T // S5_CHUNK, S5_CHUNK * S5_GROUP, S5_LANE_GROUPS
    nlev = int(math.log2(nch))
    assert 1 << nlev == nch and nlev <= S5_LEVELS
    tok = pl.BlockSpec((None, T, 128), lambda b, j: (b, 0, j))
    return pl.pallas_call(
        functools.partial(_s5_kernel, nch=nch, nlev=nlev),
        grid=(B, D // 128),
        in_specs=[tok,
                  pl.BlockSpec((NG, W, 3 * W), lambda b, j: (j, 0, 0)),
                  pl.BlockSpec((NG // 2, 2 * W, 2 * W), lambda b, j: (j, 0, 0)),
                  pl.BlockSpec((NG // 2, S5_LEVELS, 2 * W), lambda b, j: (j, 0, 0))],
        out_specs=tok,
        out_shape=jax.ShapeDtypeStruct((B, T, D), F32),
        scratch_shapes=[pltpu.VMEM((NG, nch, W), BF16), pltpu.VMEM((NG, nch, W), F32)],
        compiler_params=_params(("parallel", "parallel"), 32),
        name="s5_scan",
    )(u, w1, w2, mult)


def _attn_bias_table():
    i = np.arange(ATTN_QB)[:, None]
    j = np.arange(ATTN_KW)[None, :]
    rel = j - ATTN_R - i
    slopes = 2.0 ** (-8.0 * np.arange(1, N_HEADS + 1, dtype=np.float64) / N_HEADS)
    tab = np.empty((N_HEADS // 2, len(DILATED_CONFIGS), 4, 2, ATTN_QB, ATTN_KW), np.float32)
    for bi, (_, d) in enumerate(DILATED_CONFIGS):
        for ty in range(4):
            valid = np.abs(rel) <= ATTN_R
            if ty & 1:
                valid = valid & (j >= ATTN_R)
            if ty & 2:
                valid = valid & (j < ATTN_QB + ATTN_R)
            for h in range(N_HEADS):
                dist = (np.abs(rel) * d).astype(np.float32)
                alibi = (-np.float32(slopes[h])) * dist
                bias = np.where(valid, alibi, np.float32(NEG_BIG))
                if ty == 3:
                    bias = np.concatenate([bias[:, ATTN_R:ATTN_R + ATTN_QB],
                                           np.full((ATTN_QB, 2 * ATTN_R), NEG_BIG, np.float32)], axis=1)
                tab[h // 2, bi, ty, h % 2] = bias
    return tab.reshape(N_HEADS // 2, len(DILATED_CONFIGS) * 4, 2 * ATTN_QB, ATTN_KW)


def _attn_kernel(q_ref, k_ref, v_ref, bias_ref, o_ref, kp, vp, acc_o, acc_m, acc_l, *, T):
    zeros = jnp.zeros((ATTN_PAD, 128), F32)
    for buf, src in ((kp, k_ref), (vp, v_ref)):
        buf[0:ATTN_PAD, :] = zeros
        buf[ATTN_PAD + T:ATTN_PAD + T + ATTN_PAD, :] = zeros
        buf[ATTN_PAD:ATTN_PAD + T, :] = src[...]
    first_head = lax.broadcasted_iota(jnp.int32, (ATTN_QB, 128), 1) < HEAD_DIM
    ones = jnp.ones((ATTN_KW, 128), BF16)
    n_branch = len(DILATED_CONFIGS)

    def block_softmax(bi, d, nblk, idx):
        r = idx % d
        blk = idx // d
        q_start = r + d * ATTN_QB * blk
        rows = pl.ds(q_start, ATTN_QB, stride=d)
        qf = q_ref[rows, :]
        q2 = jnp.concatenate([jnp.where(first_head, qf, 0.0), jnp.where(first_head, 0.0, qf)], axis=0).astype(BF16)
        if nblk == 1:
            keys = pl.ds(ATTN_PAD + q_start, ATTN_QB, stride=d)
            bias = bias_ref[bi * 4 + 3, :, 0:ATTN_QB]
        else:
            keys = pl.ds(ATTN_PAD + q_start - ATTN_R * d, ATTN_KW, stride=d)
            bias = bias_ref[bi * 4 + jnp.where(blk == 0, 1, 0) + jnp.where(blk == nblk - 1, 2, 0)]
        kw = kp[keys, :].astype(BF16)
        s = lax.dot_general(q2, kw, (((1,), (1,)), ((), ())), preferred_element_type=F32)
        s = s + bias
        m = jnp.max(s, axis=-1, keepdims=True)
        p = jnp.exp(s - m).astype(BF16)
        ov = _dot(p, jnp.concatenate([vp[keys, :].astype(BF16), ones[0:kw.shape[0]]], axis=1))
        o_new = jnp.where(first_head, ov[0:ATTN_QB, 0:128], ov[ATTN_QB:, 0:128])
        l_new = jnp.where(first_head, ov[0:ATTN_QB, 128:], ov[ATTN_QB:, 128:])
        m_new = jnp.where(first_head, m[0:ATTN_QB], m[ATTN_QB:])
        return rows, o_new, m_new, l_new

    for bi, (_, d) in enumerate(DILATED_CONFIGS):
        nblk = T // d // ATTN_QB

        def group(it, carry, bi=bi, d=d, nblk=nblk):
            new = [block_softmax(bi, d, nblk, it * ATTN_UNROLL + u) for u in range(ATTN_UNROLL)]
            if bi == 0:
                for rows, o_new, m_new, l_new in new:
                    acc_o[rows, :] = o_new
                    acc_m[rows, :] = m_new
                    acc_l[rows, :] = l_new
                return carry
            old = [(acc_o[rows, :], acc_m[rows, :], acc_l[rows, :]) for rows, _, _, _ in new]
            merged = []
            for (rows, o_new, m_new, l_new), (o_old, m_old, l_old) in zip(new, old):
                e = jnp.exp(-jnp.abs(m_old - m_new))
                keep = m_old >= m_new
                a = jnp.where(keep, 1.0, e)
                b = jnp.where(keep, e, 1.0)
                merged.append((rows, a * o_old + b * o_new, jnp.maximum(m_old, m_new), a * l_old + b * l_new))
            for rows, o_tot, m_tot, l_tot in merged:
                if bi == n_branch - 1:
                    o_ref[rows, :] = o_tot / l_tot
                else:
                    acc_o[rows, :] = o_tot
                    acc_m[rows, :] = m_tot
                    acc_l[rows, :] = l_tot
            return carry

        lax.fori_loop(0, d * nblk // ATTN_UNROLL, group, 0)


def _attention(q, k, v, bias):
    B, T, _ = q.shape
    assert T % (ATTN_QB * DILATED_CONFIGS[-1][1]) == 0
    pair = pl.BlockSpec((None, T, 128), lambda b, p: (b, 0, p))
    nb = bias.shape[1]
    return pl.pallas_call(
        functools.partial(_attn_kernel, T=T),
        grid=(B, N_HEADS // 2),
        in_specs=[pair, pair, pair, pl.BlockSpec((None, nb, 2 * ATTN_QB, ATTN_KW), lambda b, p: (p, 0, 0, 0))],
        out_specs=pair,
        out_shape=jax.ShapeDtypeStruct((B, T, D_B), F32),
        scratch_shapes=[pltpu.VMEM((T + 2 * ATTN_PAD, 128), F32), pltpu.VMEM((T + 2 * ATTN_PAD, 128), F32),
                        pltpu.VMEM((T, 128), F32), pltpu.VMEM((T, 128), F32), pltpu.VMEM((T, 128), F32)],
        compiler_params=_params(("parallel", "arbitrary"), 48),
        name="dilated_attn",
    )(q, k, v, bias)


def _ab_out_kernel(x_ref, ya_ref, yb_ref, mod_ref, wglu_ref, bglu_ref, wo_ref, o_ref):
    for r0 in range(0, x_ref.shape[0], ROW_BLOCK):
        rows = slice(r0, r0 + ROW_BLOCK)
        y = ya_ref[rows, :]
        y = 0.5 * y * (1.0 + jnp.tanh(math.sqrt(2.0 / math.pi) * (y + 0.044715 * (y * y * y))))
        y = y * _sigmoid(_dot(y.astype(BF16), wglu_ref[...]) + bglu_ref[...])
        out = _dot(y.astype(BF16), wo_ref[0:D_A, :]) + _dot(yb_ref[rows, :].astype(BF16), wo_ref[D_A:, :])
        o_ref[rows, :] = x_ref[rows, :] + mod_ref[2:3, :] * out


def _ab_out(x, ya, yb, mod, w_glu, b_glu, w_out, tm=512):
    B, T, D = x.shape
    half = pl.BlockSpec((None, tm, D_A), lambda b, i: (b, i, 0))
    full = pl.BlockSpec((None, tm, D), lambda b, i: (b, i, 0))
    return pl.pallas_call(
        _ab_out_kernel,
        grid=(B, T // tm),
        in_specs=[full, half, half, pl.BlockSpec((None, 6, D), lambda b, i: (b, 0, 0)),
                  _const_spec((D_A, D_A)), _const_spec((1, D_A)), _const_spec((D, D))],
        out_specs=full,
        out_shape=jax.ShapeDtypeStruct((B, T, D), F32),
        compiler_params=_params(("parallel", "parallel"), 48),
        name="ab_out",
    )(x, ya, yb, mod, w_glu, b_glu, w_out)


def _halo_specs(tm, T, D):
    nh = T // HALO
    per = tm // HALO
    main = pl.BlockSpec((None, tm, D), lambda b, i: (b, i, 0))
    prev = pl.BlockSpec((None, HALO, D), lambda b, i: (b, jnp.maximum(i * per - 1, 0), 0))
    nxt = pl.BlockSpec((None, HALO, D), lambda b, i: (b, jnp.minimum((i + 1) * per, nh - 1), 0))
    return main, prev, nxt


def _conv3_rows(buf, w_ref, tm):
    return (buf[pl.ds(HALO - 1, tm), :] * w_ref[0:1, :] + buf[pl.ds(HALO, tm), :] * w_ref[1:2, :]
            + buf[pl.ds(HALO + 1, tm), :] * w_ref[2:3, :])


def _zero_sequence_edges(buf, tm):
    zero_row = jnp.zeros((1, buf.shape[1]), F32)

    @pl.when(pl.program_id(1) == 0)
    def _():
        buf[HALO - 1:HALO, :] = zero_row

    @pl.when(pl.program_id(1) == pl.num_programs(1) - 1)
    def _():
        buf[HALO + tm:HALO + tm + 1, :] = zero_row


def _ffn_kernel(x_ref, xp_ref, xn_ref, mod_ref, g_ref, wg_ref, wu_ref, cw_ref, cb_ref, wd_ref, o_ref, gbuf, abuf, *, tm):
    x = x_ref[...]
    g, shift, scale = g_ref[...], mod_ref[3:4, :], mod_ref[4:5, :]
    has_prev = jnp.where(pl.program_id(1) > 0, 1.0, 0.0)
    has_next = jnp.where(pl.program_id(1) < pl.num_programs(1) - 1, 1.0, 0.0)
    h = _norm_mod(x, g, shift, scale)
    he = jnp.concatenate([_norm_mod(xp_ref[...], g, shift, scale) * has_prev, h,
                          _norm_mod(xn_ref[...], g, shift, scale) * has_next], axis=0).astype(BF16)
    h = he[HALO:HALO + tm]
    for c in range(D_FF // FFN_CHUNK):
        cols = slice(c * FFN_CHUNK, (c + 1) * FFN_CHUNK)
        gbuf[:, cols] = _dot(he, wg_ref[:, cols])
        gate = (gbuf[pl.ds(HALO - 1, tm), cols] * cw_ref[0:1, cols] + gbuf[pl.ds(HALO, tm), cols] * cw_ref[1:2, cols]
                + gbuf[pl.ds(HALO + 1, tm), cols] * cw_ref[2:3, cols] + cb_ref[:, cols])
        up = _dot(h, wu_ref[:, cols])
        abuf[:, cols] = ((gate * _sigmoid(gate)) * up).astype(BF16)
    o_ref[...] = x + mod_ref[5:6, :] * _dot(abuf[...], wd_ref[...])


def _ffn(x, mod, g, w_gate, w_up, conv_w, conv_b, w_down, tm=512):
    B, T, D = x.shape
    main, prev, nxt = _halo_specs(tm, T, D)
    return pl.pallas_call(
        functools.partial(_ffn_kernel, tm=tm),
        grid=(B, T // tm),
        in_specs=[main, prev, nxt, pl.BlockSpec((None, 6, D), lambda b, i: (b, 0, 0)), _const_spec((1, D)),
                  _const_spec((D, D_FF)), _const_spec((D, D_FF)), _const_spec((3, D_FF)), _const_spec((1, D_FF)),
                  _const_spec((D_FF, D))],
        out_specs=main,
        out_shape=jax.ShapeDtypeStruct((B, T, D), F32),
        scratch_shapes=[pltpu.VMEM((tm + 2 * HALO, D_FF), F32), pltpu.VMEM((tm, D_FF), BF16)],
        compiler_params=_params(("parallel", "arbitrary"), 56),
        name="conv_ffn",
    )(x, x, x, mod, g, w_gate, w_up, conv_w, conv_b, w_down)


def _cd_in_kernel(x_ref, xp_ref, xn_ref, mod_ref, g_ref, w_ref, cbd_ref, sbd_ref, sw_ref, ab_ref, yd_ref, cbuf, *, tm):
    xe = jnp.concatenate([xp_ref[...], x_ref[...], xn_ref[...]], axis=0)
    he = _norm_mod(xe, g_ref[...], mod_ref[0:1, :], mod_ref[1:2, :]).astype(BF16)
    z = _dot(he, w_ref[...])
    cbuf[...] = z[:, D_C + 2 * D_D:] * z[:, D_C:D_C + D_D]
    _zero_sequence_edges(cbuf, tm)
    yd_ref[...] = z[HALO:HALO + tm, D_C + D_D:D_C + 2 * D_D] * _conv3_rows(cbuf, sw_ref, tm)
    uc = z[HALO:HALO + tm, 0:D_C].astype(BF16)
    ab_ref[0] = _dot(uc, cbd_ref[...]).astype(BF16)
    ab_ref[1] = _dot(uc, sbd_ref[...]).astype(BF16)


def _cd_in(x, mod, g, w_in, cbd, sbd, sconv_w, tm=512):
    B, T, D = x.shape
    main, prev, nxt = _halo_specs(tm, T, D)
    return pl.pallas_call(
        functools.partial(_cd_in_kernel, tm=tm),
        grid=(B, T // tm),
        in_specs=[main, prev, nxt, pl.BlockSpec((None, 6, D), lambda b, i: (b, 0, 0)), _const_spec((1, D)),
                  _const_spec((D, w_in.shape[1])), _const_spec((D_C, D_C)), _const_spec((D_C, D_C)),
                  _const_spec((3, D_D))],
        out_specs=[pl.BlockSpec((None, 2, tm, D_C), lambda b, i: (b, 0, i, 0)),
                   pl.BlockSpec((None, tm, D_D), lambda b, i: (b, i, 0))],
        out_shape=[jax.ShapeDtypeStruct((B, 2, T, D_C), BF16), jax.ShapeDtypeStruct((B, T, D_D), F32)],
        scratch_shapes=[pltpu.VMEM((tm + 2 * HALO, D_D), F32)],
        compiler_params=_params(("parallel", "arbitrary"), 48),
        name="cd_in",
    )(x, x, x, mod, g, w_in, cbd, sbd, sconv_w)


def _cd_out_kernel(dft_ref, ab_ref, yd_ref, x_ref, mod_ref, wo_ref, o_ref, fold, *, T, tf):
    half = T // 2

    @pl.when(pl.program_id(1) == 0)
    def _():
        r = lax.broadcasted_iota(jnp.int32, (tf, tf + HALO), 0)
        c = lax.broadcasted_iota(jnp.int32, (tf, tf + HALO), 1)
        mirror = jnp.where(c == tf - r, 1.0, 0.0).astype(BF16)
        first_row = lax.broadcasted_iota(jnp.int32, (tf, D_C), 0) == 0
        for part, sign in ((0, 1.0), (1, -1.0)):
            base = part * T
            for s0 in range(0, half, tf):
                nxt = ab_ref[base + T - s0:base + T - s0 + HALO, :] if s0 else jnp.zeros((HALO, D_C), BF16)
                src = jnp.concatenate([ab_ref[base + T - s0 - tf:base + T - s0, :], nxt], axis=0)
                folded = ab_ref[base + s0:base + s0 + tf, :].astype(F32) + sign * _dot(mirror, src)
                if part == 1 and s0 == 0:
                    folded = jnp.where(first_row, ab_ref[half:half + HALO, :].astype(F32)[0:1, :], folded)
                fold[part * half + s0:part * half + s0 + tf, :] = folded.astype(BF16)

    yc = _dot(dft_ref[...], fold[...])
    out = _dot(yc.astype(BF16), wo_ref[0:D_C, :]) + _dot(yd_ref[...].astype(BF16), wo_ref[D_C:, :])
    o_ref[...] = x_ref[...] + mod_ref[2:3, :] * out


def _cd_out(dft, ab, yd, x, mod, w_out, tm=256):
    B, T, D = x.shape
    return pl.pallas_call(
        functools.partial(_cd_out_kernel, T=T, tf=256),
        grid=(B, T // tm),
        in_specs=[pl.BlockSpec((tm, T), lambda b, i: (i, 0)),
                  pl.BlockSpec((None, 2 * T, D_C), lambda b, i: (b, 0, 0)),
                  pl.BlockSpec((None, tm, D_D), lambda b, i: (b, i, 0)),
                  pl.BlockSpec((None, tm, D), lambda b, i: (b, i, 0)),
                  pl.BlockSpec((None, 6, D), lambda b, i: (b, 0, 0)), _const_spec((D, D))],
        out_specs=pl.BlockSpec((None, tm, D), lambda b, i: (b, i, 0)),
        out_shape=jax.ShapeDtypeStruct((B, T, D), F32),
        scratch_shapes=[pltpu.VMEM((T, D_C), BF16)],
        compiler_params=_params(("parallel", "arbitrary"), 56),
        name="cd_out",
    )(dft, ab, yd, x, mod, w_out)


def _seq_dft_matrix(T):
    rows = 64
    s = jnp.arange(T // 2, dtype=jnp.int32)[None, :]
    angle = lambda t: ((t[:, None] * s) % T).astype(F32) * (2.0 * math.pi / T)
    ang_a = angle(jnp.arange(T // rows, dtype=jnp.int32) * rows)[:, None, :]
    ang_b = angle(jnp.arange(rows, dtype=jnp.int32))[None, :, :]
    scale = 1.0 / math.sqrt(T)
    cos_a, sin_a, cos_b, sin_b = jnp.cos(ang_a) * scale, jnp.sin(ang_a) * scale, jnp.cos(ang_b), jnp.sin(ang_b)
    cos_ts = (cos_a * cos_b - sin_a * sin_b).reshape(T, T // 2)
    sin_ts = (sin_a * cos_b + cos_a * sin_b).reshape(T, T // 2)
    nyquist = jnp.where(jnp.arange(T) % 2 == 0, scale, -scale)[:, None]
    return jnp.concatenate([cos_ts, jnp.where(s == 0, nyquist, -sin_ts)], axis=1).astype(BF16)


def _channel_dft_matrices():
    c = np.arange(D_C)
    ang = 2.0 * np.pi * ((c[:, None] % FNET_GROUP_DIM) * (c[None, :] % FNET_GROUP_DIM) % FNET_GROUP_DIM) / FNET_GROUP_DIM
    same = (c[:, None] // FNET_GROUP_DIM) == (c[None, :] // FNET_GROUP_DIM)
    scale = 1.0 / math.sqrt(FNET_GROUP_DIM)
    cbd = np.where(same, np.cos(ang) * scale, 0.0).astype(np.float32)
    sbd = np.where(same, np.sin(ang) * scale, 0.0).astype(np.float32)
    return jnp.asarray(cbd).astype(BF16), jnp.asarray(sbd).astype(BF16)


def _trunk(x, mods, wts):
    B, T, D = x.shape
    nch = T // S5_CHUNK
    mod = mods[0]
    u, q, k, v = _ab_in(x, mod, wts["norm_mix_g"][0], wts["ab_w_in"], wts["hsum"], wts["q_g"], wts["k_g"])
    ya = _s5(u, wts["s5_w1"], wts["s5_w2"], wts["s5_mult"])
    yb = _attention(q, k, v, wts["attn_bias"])
    x = _ab_out(x, ya, yb, mod, wts["s5_w_glu"], wts["s5_b_glu"], wts["ab_w_out"])
    x = _ffn(x, mod, wts["norm_ffn_g"][0], *wts["ffn"][0])
    mod = mods[1]
    ab, yd = _cd_in(x, mod, wts["norm_mix_g"][1], wts["cd_w_in"], wts["cbd"], wts["sbd"], wts["sconv_w"])
    x = _cd_out(_seq_dft_matrix(T), ab.reshape(B, 2 * T, D_C), yd, x, mod, wts["cd_w_out"])
    x = _ffn(x, mod, wts["norm_ffn_g"][1], *wts["ffn"][1])
    return x


def kernel(x_prompt, x_sample, c_prompt, c_sample, ada_w, ada_b, norm_mix_g, norm_ffn_g, ffn_w_gate, ffn_w_up, ffn_conv_w, ffn_conv_b, ffn_w_down, ab_w_in, ab_w_out, s5_lam_re_f, s5_lam_im_f, s5_log_dt_f, s5_lam_re_b, s5_lam_im_b, s5_log_dt_b, s5_b_re, s5_b_im, s5_c_re, s5_c_im, s5_d, s5_w_glu, s5_b_glu, q_norm_g, k_norm_g, cd_w_in, cd_w_out, sconv_w):
    depth = ada_w.shape[0]
    assert depth == 2 and ab_w_in.shape[0] == 1 and cd_w_in.shape[0] == 1
    bp = x_prompt.shape[0]
    mod_all = _ada(jnp.concatenate([c_prompt, c_sample], axis=0), ada_w, ada_b)
    s5_w1, s5_w2, s5_mult = _s5_tables(s5_lam_re_f[0], s5_lam_im_f[0], s5_log_dt_f[0], s5_lam_re_b[0],
                                       s5_lam_im_b[0], s5_log_dt_b[0], s5_b_re[0], s5_b_im[0],
                                       s5_c_re[0], s5_c_im[0], s5_d[0])
    head = np.arange(D_B) // HEAD_DIM
    cbd, sbd = _channel_dft_matrices()
    wts = dict(
        norm_mix_g=norm_mix_g.reshape(depth, 1, D_MODEL), norm_ffn_g=norm_ffn_g.reshape(depth, 1, D_MODEL),
        ab_w_in=ab_w_in[0].astype(BF16), ab_w_out=ab_w_out[0].astype(BF16),
        hsum=jnp.asarray(head[:, None] == head[None, :], BF16),
        q_g=jnp.tile(q_norm_g[0], N_HEADS).reshape(1, D_B), k_g=jnp.tile(k_norm_g[0], N_HEADS).reshape(1, D_B),
        s5_w1=s5_w1, s5_w2=s5_w2, s5_mult=s5_mult,
        s5_w_glu=s5_w_glu[0].astype(BF16), s5_b_glu=s5_b_glu[0].reshape(1, D_A),
        attn_bias=jnp.asarray(_attn_bias_table()),
        ffn=[(ffn_w_gate[l].astype(BF16), ffn_w_up[l].astype(BF16), ffn_conv_w[l], ffn_conv_b[l].reshape(1, D_FF),
              ffn_w_down[l].astype(BF16)) for l in range(depth)],
        cd_w_in=cd_w_in[0].astype(BF16), cd_w_out=cd_w_out[0].astype(BF16), cbd=cbd, sbd=sbd, sconv_w=sconv_w[0],
    )
    outs = []
    for x, rows in ((x_prompt, slice(0, bp)), (x_sample, slice(bp, None))):
        mods = [mod_all[l, rows].reshape(x.shape[0], 6, D_MODEL) for l in range(depth)]
        outs.append(_trunk(x, mods, wts))
    return tuple(outs)
```

```python
import functools
import math

import jax
import jax.numpy as jnp
import numpy as np
from jax import lax
from jax.experimental import pallas as pl
from jax.experimental.pallas import tpu as pltpu

F32 = jnp.float32
BF16 = jnp.bfloat16

D_MODEL = 1024
D_A = 512
S5_GROUP = 16
S5_GROUPS = 32
S5_STATE = 64
S5_CHUNK = 16
S5_LEVELS = 8
S5_LANE_GROUPS = 128 // S5_GROUP
D_B = 512
HEAD_DIM = 64
N_HEADS = 8
DILATED_CONFIGS = ((128, 1), (512, 4), (2048, 16))
ATTN_R = 64
ATTN_QB = 128
ATTN_KW = ATTN_QB + 2 * ATTN_R
ATTN_PAD = ATTN_R * 16
ATTN_UNROLL = 8
NEG_BIG = -1e30
D_C = 512
FNET_GROUP_DIM = 128
D_D = 512
D_FF = 2816
FFN_CHUNK = 256
CD_CHUNK = 256
EPS = 1e-6
ROW_BLOCK = 256
HALO = 16
MIB = 2 ** 20


def _params(sem, vmem_mib):
    return pltpu.CompilerParams(dimension_semantics=sem, vmem_limit_bytes=vmem_mib * MIB)


def _const_spec(shape):
    nd = len(shape)
    return pl.BlockSpec(shape, lambda *_: (0,) * nd, pipeline_mode=pl.Buffered(1))


def _dot(a, b):
    return jnp.dot(a, b, preferred_element_type=F32)


def _norm_mod(x, g, shift, scale):
    ms = jnp.mean(x * x, axis=-1, keepdims=True)
    y = x * lax.rsqrt(ms + EPS) * g
    return y * (1.0 + scale) + shift


def _sigmoid(x):
    return 1.0 / (1.0 + jnp.exp(-x))


def _ada_kernel(c_ref, w_ref, b_ref, o_ref):
    c = c_ref[...]
    cond = (c * _sigmoid(c)).astype(BF16)
    o_ref[...] = _dot(cond, w_ref[...].astype(BF16)) + b_ref[...]


def _ada(c_all, ada_w, ada_b):
    depth, d, n = ada_w.shape
    rows = c_all.shape[0]
    tn = 1536
    return pl.pallas_call(
        _ada_kernel,
        grid=(depth, n // tn),
        in_specs=[pl.BlockSpec((rows, d), lambda l, j: (0, 0)),
                  pl.BlockSpec((None, d, tn), lambda l, j: (l, 0, j)),
                  pl.BlockSpec((None, 1, tn), lambda l, j: (l, 0, j))],
        out_specs=pl.BlockSpec((None, rows, tn), lambda l, j: (l, 0, j)),
        out_shape=jax.ShapeDtypeStruct((depth, rows, n), F32),
        compiler_params=_params(("parallel", "parallel"), 32),
        name="ada_mod",
    )(c_all, ada_w, ada_b.reshape(depth, 1, n))


def _ab_in_kernel(x_ref, mod_ref, g_ref, w_ref, hsum_ref, qg_ref, kg_ref, u_ref, q_ref, k_ref, v_ref):
    hsum = hsum_ref[...]

    def head_norm(a, g):
        ms = _dot((a * a).astype(BF16), hsum) * (1.0 / HEAD_DIM)
        return a * lax.rsqrt(ms + EPS) * g

    for r0 in range(0, x_ref.shape[0], ROW_BLOCK):
        rows = slice(r0, r0 + ROW_BLOCK)
        h = _norm_mod(x_ref[rows, :], g_ref[...], mod_ref[0:1, :], mod_ref[1:2, :])
        z = _dot(h.astype(BF16), w_ref[...])
        u_ref[rows, :] = z[:, 0:D_A]
        q_ref[rows, :] = head_norm(z[:, D_A:D_A + D_B], qg_ref[...]) * (1.0 / math.sqrt(HEAD_DIM))
        k_ref[rows, :] = head_norm(z[:, D_A + D_B:D_A + 2 * D_B], kg_ref[...])
        v_ref[rows, :] = z[:, D_A + 2 * D_B:]


def _ab_in(x, mod, g, w_in, hsum, qg, kg, tm=512):
    B, T, D = x.shape
    n = w_in.shape[1]
    tok = pl.BlockSpec((None, tm, D_B), lambda b, i: (b, i, 0))
    out = jax.ShapeDtypeStruct((B, T, D_B), F32)
    return pl.pallas_call(
        _ab_in_kernel,
        grid=(B, T // tm),
        in_specs=[pl.BlockSpec((None, tm, D), lambda b, i: (b, i, 0)),
                  pl.BlockSpec((None, 6, D), lambda b, i: (b, 0, 0)),
                  _const_spec((1, D)), _const_spec((D, n)), _const_spec((D_B, D_B)),
                  _const_spec((1, D_B)), _const_spec((1, D_B))],
        out_specs=[tok, tok, tok, tok],
        out_shape=[out, out, out, out],
        compiler_params=_params(("parallel", "parallel"), 48),
        name="ab_in",
    )(x, mod, g, w_in, hsum, qg, kg)


def _cmul(a, b):
    return a[0] * b[0] - a[1] * b[1], a[0] * b[1] + a[1] * b[0]


def _s5_param_kernel(lam_ref, dt_ref, b_re_ref, b_im_ref, ct_re_ref, ct_im_ref, c_re_ref, c_im_ref, d_ref,
                     wf_re_ref, wf_im_ref, wb_re_ref, wb_im_ref, kf_ref, kb_ref,
                     caf_re_ref, caf_im_ref, cab_re_ref, cab_im_ref, ap_ref):
    P, W = S5_STATE, S5_CHUNK * S5_GROUP
    kidx = lax.broadcasted_iota(jnp.int32, (P, W), 1) // S5_GROUP
    lane_ap = lax.broadcasted_iota(jnp.int32, (P, 128), 1)
    ap = jnp.zeros((P, 128), F32)
    c_re, c_im = c_re_ref[...], c_im_ref[...]
    outs = ((wf_re_ref, wf_im_ref, kf_ref, caf_re_ref, caf_im_ref),
            (wb_re_ref, wb_im_ref, kb_ref, cab_re_ref, cab_im_ref))
    for direction in range(2):
        lam_re = lam_ref[:, 2 * direction:2 * direction + 1]
        lam_im = lam_ref[:, 2 * direction + 1:2 * direction + 2]
        dt = jnp.exp(dt_ref[:, direction:direction + 1])
        mag = jnp.exp(lam_re * dt)
        a_re = mag * jnp.cos(lam_im * dt)
        a_im = mag * jnp.sin(lam_im * dt)
        den = lam_re * lam_re + lam_im * lam_im
        coef_re = ((a_re - 1.0) * lam_re + a_im * lam_im) / den
        coef_im = (a_im * lam_re - (a_re - 1.0) * lam_im) / den
        bb = _cmul((coef_re, coef_im), (b_re_ref[...], b_im_ref[...]))
        a_pow = (a_re, a_im)
        pw = (jnp.ones((P, W), F32), jnp.zeros((P, W), F32))
        for j in range(4):
            bit = ((kidx >> j) & 1) == 1
            pw = _cmul(pw, (jnp.where(bit, a_pow[0], 1.0), jnp.where(bit, a_pow[1], 0.0)))
            a_pow = _cmul(a_pow, a_pow)
        for j in range(S5_LEVELS):
            base = 4 * j + 2 * direction
            ap = jnp.where(lane_ap == base, a_pow[0], ap)
            ap = jnp.where(lane_ap == base + 1, a_pow[1], ap)
            a_pow = _cmul(a_pow, a_pow)
        w_re, w_im = _cmul(pw, bb)
        ca_re, ca_im = _cmul(_cmul(pw, (a_re, a_im)), (ct_re_ref[...], ct_im_ref[...]))
        hp = lax.Precision.HIGHEST
        kmat = (jnp.dot(c_re, w_re, precision=hp, preferred_element_type=F32)
                - jnp.dot(c_im, w_im, precision=hp, preferred_element_type=F32))
        o_w_re, o_w_im, o_k, o_ca_re, o_ca_im = outs[direction]
        o_w_re[...] = w_re
        o_w_im[...] = w_im
        o_k[...] = kmat
        o_ca_re[...] = ca_re
        o_ca_im[...] = -ca_im
    row = lax.broadcasted_iota(jnp.int32, (S5_GROUP, W), 0)
    lane = lax.broadcasted_iota(jnp.int32, (S5_GROUP, W), 1)
    lag0 = kb_ref[...] + jnp.where(row == lane, d_ref[...], 0.0)
    kf_ref[...] = kf_ref[...] + jnp.where(lane < S5_GROUP, lag0, 0.0)
    ap_ref[...] = ap


def _s5_tables(lam_re_f, lam_im_f, log_dt_f, lam_re_b, lam_im_b, log_dt_b, b_re, b_im, c_re, c_im, d_skip):
    G, P, C, L, W = S5_GROUPS, S5_STATE, S5_GROUP, S5_CHUNK, S5_CHUNK * S5_GROUP
    lam = jnp.stack([lam_re_f, lam_im_f, lam_re_b, lam_im_b], axis=-1)
    dts = jnp.stack([log_dt_f, log_dt_b], axis=-1).reshape(G, 1, 2)
    tile_k = lambda a: jnp.tile(a, (1, 1, L))
    ct = lambda a: tile_k(jnp.swapaxes(a, 1, 2))
    grp = lambda r, c: pl.BlockSpec((None, r, c), lambda g: (g, 0, 0))
    pw_out = jax.ShapeDtypeStruct((G, P, W), F32)
    k_out = jax.ShapeDtypeStruct((G, C, W), F32)
    wf_re, wf_im, wb_re, wb_im, kf, kb, caf_re, caf_im, cab_re, cab_im, ap = pl.pallas_call(
        _s5_param_kernel,
        grid=(G,),
        in_specs=[grp(P, 4), grp(1, 2), grp(P, W), grp(P, W), grp(P, W), grp(P, W), grp(C, P), grp(C, P), grp(C, 1)],
        out_specs=[grp(P, W)] * 4 + [grp(C, W)] * 2 + [grp(P, W)] * 4 + [grp(P, 128)],
        out_shape=[pw_out] * 4 + [k_out] * 2 + [pw_out] * 4 + [jax.ShapeDtypeStruct((G, P, 128), F32)],
        compiler_params=_params(("parallel",), 32),
        name="s5_params",
    )(lam, dts, tile_k(b_re), tile_k(b_im), ct(c_re), ct(c_im), c_re, c_im, d_skip.reshape(G, C, 1))

    NG, J, hp = S5_LANE_GROUPS, G // S5_LANE_GROUPS, lax.Precision.HIGHEST
    order = _s5_token_order()
    place = (order[:, :, None] == np.arange(L)).astype(np.float32)
    place_rev = (order[:, :, None] == L - 1 - np.arange(L)).astype(np.float32)
    lag = order[:, None, :] - order[:, :, None] + (L - 1)
    lag_hot = (lag[..., None] == np.arange(2 * L - 1)).astype(np.float32)
    lag_tab = jnp.concatenate([kb.reshape(G, C, L, C)[:, :, :0:-1], kf.reshape(G, C, L, C)], axis=2)
    toep = jnp.einsum("xstl,jxoli->jxsito", lag_hot, lag_tab.reshape(J, NG, C, 2 * L - 1, C), precision=hp)
    toep = toep.reshape(G, W, W)

    def slots(a, hot):
        return jnp.einsum("xqk,jxpkc->jxpqc", hot, a.reshape(J, NG, P, L, C), precision=hp).reshape(G, P, W)

    t_ = lambda a: jnp.swapaxes(a, 1, 2)
    sf_re, sf_im = t_(slots(wf_re, place_rev)), t_(slots(wf_im, place_rev))
    sb_re, sb_im = t_(slots(wb_re, place)), t_(slots(wb_im, place))
    w1 = jnp.concatenate([toep, sf_re, sf_im, sf_im, sf_re, sb_re, sb_im, sb_im, sb_re], axis=2)

    def pair_rows(a):
        a = a.reshape(G // 2, 2, P, W)
        zero = jnp.zeros_like(a[:, 0])
        return jnp.concatenate([jnp.concatenate([a[:, 0], zero], axis=2),
                                jnp.concatenate([zero, a[:, 1]], axis=2)], axis=1)

    w2 = jnp.concatenate([pair_rows(slots(caf_re, place)), pair_rows(slots(caf_im, place)),
                          pair_rows(slots(cab_re, place_rev)), pair_rows(slots(cab_im, place_rev))], axis=1)
    apj = ap[:, :, :4 * S5_LEVELS].reshape(G // 2, 2, P, S5_LEVELS, 4).transpose(0, 3, 4, 1, 2)
    mult = apj.reshape(G // 2, S5_LEVELS, 4 * 2 * P)
    return w1.astype(BF16), w2.astype(BF16), mult


def _s5_token_order():
    NG = S5_LANE_GROUPS
    order = np.empty((NG, S5_CHUNK), np.int64)
    for gl in range(NG):
        for half in range(S5_CHUNK // NG):
            for blk in range(NG):
                order[gl, half * NG + blk] = half * NG + (blk - gl) % NG
    return order


def _s5_kernel(u_ref, w1_ref, w2_ref, mult_ref, y_ref, ucbuf, ybuf, *, nch, nlev):
    row = lax.broadcasted_iota(jnp.int32, (nch, 128), 0)
    lane_blk = lax.broadcasted_iota(jnp.int32, (nch, 128), 1) // S5_GROUP
    W, L, C, NG = S5_CHUNK * S5_GROUP, S5_CHUNK, S5_GROUP, S5_LANE_GROUPS

    def pick(slabs, shift):
        acc = slabs[(0 - shift) % NG]
        for blk in range(1, NG):
            acc = jnp.where(lane_blk == blk, slabs[(blk - shift) % NG], acc)
        return acc

    def shift_down(x, s):
        if s % 8 == 0:
            return jnp.concatenate([jnp.zeros((s, 128), F32), x[:nch - s]], axis=0)
        return jnp.where(row >= s, pltpu.roll(x, s, 0), 0.0)

    def shift_up(x, s):
        if s % 8 == 0:
            return jnp.concatenate([x[s:], jnp.zeros((s, 128), F32)], axis=0)
        return jnp.where(row < nch - s, pltpu.roll(x, nch - s, 0), 0.0)

    for half in range(L // NG):
        rolled = []
        for k in range(NG):
            slab = u_ref[pl.ds(half * NG + k, nch, stride=L), :]
            rolled.append(pltpu.roll(slab, k * C, 1) if k else slab)
        for gl in range(NG):
            ucbuf[gl, :, half * 128:(half + 1) * 128] = pick(rolled, gl).astype(BF16)

    first_group = lax.broadcasted_iota(jnp.int32, (nch, 128), 1) < S5_STATE

    for m in range(NG // 2):
        r0 = _dot(ucbuf[2 * m], w1_ref[2 * m])
        r1 = _dot(ucbuf[2 * m + 1], w1_ref[2 * m + 1])
        xf_re = jnp.where(first_group, r0[:, W:W + 128], r1[:, W + 128:W + 256])
        xf_im = jnp.where(first_group, r0[:, W + 128:W + 256], r1[:, W:W + 128])
        xb_re = jnp.where(first_group, r0[:, W + 256:W + 384], r1[:, W + 384:W + 512])
        xb_im = jnp.where(first_group, r0[:, W + 384:W + 512], r1[:, W + 256:W + 384])
        mult = mult_ref[m]
        for j in range(nlev):
            s = 1 << j
            a_re, a_im = mult[j:j + 1, 0:128], mult[j:j + 1, 128:256]
            s_re, s_im = shift_down(xf_re, s), shift_down(xf_im, s)
            xf_re, xf_im = xf_re + (a_re * s_re - a_im * s_im), xf_im + (a_re * s_im + a_im * s_re)
            a_re, a_im = mult[j:j + 1, 256:384], mult[j:j + 1, 384:512]
            s_re, s_im = shift_up(xb_re, s), shift_up(xb_im, s)
            xb_re, xb_im = xb_re + (a_re * s_re - a_im * s_im), xb_im + (a_re * s_im + a_im * s_re)
        xs = jnp.concatenate([shift_down(xf_re, 1), shift_down(xf_im, 1), shift_up(xb_re, 1), shift_up(xb_im, 1)],
                             axis=1).astype(BF16)
        y = _dot(xs, w2_ref[m])
        ybuf[2 * m] = r0[:, 0:W] + y[:, 0:W]
        ybuf[2 * m + 1] = r1[:, 0:W] + y[:, W:]

    for half in range(L // NG):
        for k in range(NG):
            slab = pick([ybuf[gl, :, half * 128:(half + 1) * 128] for gl in range(NG)], k)
            y_ref[pl.ds(half * NG + k, nch, stride=L), :] = pltpu.roll(slab, (NG - k) * C, 1) if k else slab


def _s5(u, w1, w2, mult):
    B, T, D = u.shape
    nch, W, NG = T // S5_CHUNK, S5_CHUNK * S5_GROUP, S5_LANE_GROUPS
    nlev = int(math.log2(nch))
    assert 1 << nlev == nch and nlev <= S5_LEVELS
    tok = pl.BlockSpec((None, T, 128), lambda b, j: (b, 0, j))
    return pl.pallas_call(
        functools.partial(_s5_kernel, nch=nch, nlev=nlev),
        grid=(B, D // 128),
        in_specs=[tok,
                  pl.BlockSpec((NG, W, 3 * W), lambda b, j: (j, 0, 0)),
                  pl.BlockSpec((NG // 2, 2 * W, 2 * W), lambda b, j: (j, 0, 0)),
                  pl.BlockSpec((NG // 2, S5_LEVELS, 2 * W), lambda b, j: (j, 0, 0))],
        out_specs=tok,
        out_shape=jax.ShapeDtypeStruct((B, T, D), F32),
        scratch_shapes=[pltpu.VMEM((NG, nch, W), BF16), pltpu.VMEM((NG, nch, W), F32)],
        compiler_params=_params(("parallel", "parallel"), 32),
        name="s5_scan",
    )(u, w1, w2, mult)


def _attn_bias_table():
    i = np.arange(ATTN_QB)[:, None]
    j = np.arange(ATTN_KW)[None, :]
    rel = j - ATTN_R - i
    slopes = 2.0 ** (-8.0 * np.arange(1, N_HEADS + 1, dtype=np.float64) / N_HEADS)
    tab = np.empty((N_HEADS // 2, len(DILATED_CONFIGS), 4, 2, ATTN_QB, ATTN_KW), np.float32)
    for bi, (_, d) in enumerate(DILATED_CONFIGS):
        for ty in range(4):
            valid = np.abs(rel) <= ATTN_R
            if ty & 1:
                valid = valid & (j >= ATTN_R)
            if ty & 2:
                valid = valid & (j < ATTN_QB + ATTN_R)
            for h in range(N_HEADS):
                dist = (np.abs(rel) * d).astype(np.float32)
                alibi = (-np.float32(slopes[h])) * dist
                bias = np.where(valid, alibi, np.float32(NEG_BIG))
                if ty == 3:
                    bias = np.concatenate([bias[:, ATTN_R:ATTN_R + ATTN_QB],
                                           np.full((ATTN_QB, 2 * ATTN_R), NEG_BIG, np.float32)], axis=1)
                tab[h // 2, bi, ty, h % 2] = bias
    return tab.reshape(N_HEADS // 2, len(DILATED_CONFIGS) * 4, 2 * ATTN_QB, ATTN_KW)


def _attn_kernel(q_ref, k_ref, v_ref, bias_ref, o_ref, kp, vp, acc_o, acc_m, acc_l, *, T):
    zeros = jnp.zeros((ATTN_PAD, 128), F32)
    for buf, src in ((kp, k_ref), (vp, v_ref)):
        buf[0:ATTN_PAD, :] = zeros
        buf[ATTN_PAD + T:ATTN_PAD + T + ATTN_PAD, :] = zeros
        buf[ATTN_PAD:ATTN_PAD + T, :] = src[...]
    first_head = lax.broadcasted_iota(jnp.int32, (ATTN_QB, 128), 1) < HEAD_DIM
    ones = jnp.ones((ATTN_KW, 128), BF16)
    n_branch = len(DILATED_CONFIGS)

    def block_softmax(bi, d, nblk, idx):
        r = idx % d
        blk = idx // d
        q_start = r + d * ATTN_QB * blk
        rows = pl.ds(q_start, ATTN_QB, stride=d)
        qf = q_ref[rows, :]
        q2 = jnp.concatenate([jnp.where(first_head, qf, 0.0), jnp.where(first_head, 0.0, qf)], axis=0).astype(BF16)
        if nblk == 1:
            keys = pl.ds(ATTN_PAD + q_start, ATTN_QB, stride=d)
            bias = bias_ref[bi * 4 + 3, :, 0:ATTN_QB]
        else:
            keys = pl.ds(ATTN_PAD + q_start - ATTN_R * d, ATTN_KW, stride=d)
            bias = bias_ref[bi * 4 + jnp.where(blk == 0, 1, 0) + jnp.where(blk == nblk - 1, 2, 0)]
        kw = kp[keys, :].astype(BF16)
        s = lax.dot_general(q2, kw, (((1,), (1,)), ((), ())), preferred_element_type=F32)
        s = s + bias
        m = jnp.max(s, axis=-1, keepdims=True)
        p = jnp.exp(s - m).astype(BF16)
        ov = _dot(p, jnp.concatenate([vp[keys, :].astype(BF16), ones[0:kw.shape[0]]], axis=1))
        o_new = jnp.where(first_head, ov[0:ATTN_QB, 0:128], ov[ATTN_QB:, 0:128])
        l_new = jnp.where(first_head, ov[0:ATTN_QB, 128:], ov[ATTN_QB:, 128:])
        m_new = jnp.where(first_head, m[0:ATTN_QB], m[ATTN_QB:])
        return rows, o_new, m_new, l_new

    for bi, (_, d) in enumerate(DILATED_CONFIGS):
        nblk = T // d // ATTN_QB

        def group(it, carry, bi=bi, d=d, nblk=nblk):
            new = [block_softmax(bi, d, nblk, it * ATTN_UNROLL + u) for u in range(ATTN_UNROLL)]
            if bi == 0:
                for rows, o_new, m_new, l_new in new:
                    acc_o[rows, :] = o_new
                    acc_m[rows, :] = m_new
                    acc_l[rows, :] = l_new
                return carry
            old = [(acc_o[rows, :], acc_m[rows, :], acc_l[rows, :]) for rows, _, _, _ in new]
            merged = []
            for (rows, o_new, m_new, l_new), (o_old, m_old, l_old) in zip(new, old):
                e = jnp.exp(-jnp.abs(m_old - m_new))
                keep = m_old >= m_new
                a = jnp.where(keep, 1.0, e)
                b = jnp.where(keep, e, 1.0)
                merged.append((rows, a * o_old + b * o_new, jnp.maximum(m_old, m_new), a * l_old + b * l_new))
            for rows, o_tot, m_tot, l_tot in merged:
                if bi == n_branch - 1:
                    o_ref[rows, :] = o_tot / l_tot
                else:
                    acc_o[rows, :] = o_tot
                    acc_m[rows, :] = m_tot
                    acc_l[rows, :] = l_tot
            return carry

        lax.fori_loop(0, d * nblk // ATTN_UNROLL, group, 0)


def _attention(q, k, v, bias):
    B, T, _ = q.shape
    assert T % (ATTN_QB * DILATED_CONFIGS[-1][1]) == 0
    pair = pl.BlockSpec((None, T, 128), lambda b, p: (b, 0, p))
    nb = bias.shape[1]
    return pl.pallas_call(
        functools.partial(_attn_kernel, T=T),
        grid=(B, N_HEADS // 2),
        in_specs=[pair, pair, pair, pl.BlockSpec((None, nb, 2 * ATTN_QB, ATTN_KW), lambda b, p: (p, 0, 0, 0))],
        out_specs=pair,
        out_shape=jax.ShapeDtypeStruct((B, T, D_B), F32),
        scratch_shapes=[pltpu.VMEM((T + 2 * ATTN_PAD, 128), F32), pltpu.VMEM((T + 2 * ATTN_PAD, 128), F32),
                        pltpu.VMEM((T, 128), F32), pltpu.VMEM((T, 128), F32), pltpu.VMEM((T, 128), F32)],
        compiler_params=_params(("parallel", "arbitrary"), 48),
        name="dilated_attn",
    )(q, k, v, bias)


def _ab_out_kernel(x_ref, ya_ref, yb_ref, mod_ref, wglu_ref, bglu_ref, wo_ref, o_ref):
    for r0 in range(0, x_ref.shape[0], ROW_BLOCK):
        rows = slice(r0, r0 + ROW_BLOCK)
        y = ya_ref[rows, :]
        y = 0.5 * y * (1.0 + jnp.tanh(math.sqrt(2.0 / math.pi) * (y + 0.044715 * (y * y * y))))
        y = y * _sigmoid(_dot(y.astype(BF16), wglu_ref[...]) + bglu_ref[...])
        out = _dot(y.astype(BF16), wo_ref[0:D_A, :]) + _dot(yb_ref[rows, :].astype(BF16), wo_ref[D_A:, :])
        o_ref[rows, :] = x_ref[rows, :] + mod_ref[2:3, :] * out


def _ab_out(x, ya, yb, mod, w_glu, b_glu, w_out, tm=512):
    B, T, D = x.shape
    half = pl.BlockSpec((None, tm, D_A), lambda b, i: (b, i, 0))
    full = pl.BlockSpec((None, tm, D), lambda b, i: (b, i, 0))
    return pl.pallas_call(
        _ab_out_kernel,
        grid=(B, T // tm),
        in_specs=[full, half, half, pl.BlockSpec((None, 6, D), lambda b, i: (b, 0, 0)),
                  _const_spec((D_A, D_A)), _const_spec((1, D_A)), _const_spec((D, D))],
        out_specs=full,
        out_shape=jax.ShapeDtypeStruct((B, T, D), F32),
        compiler_params=_params(("parallel", "parallel"), 48),
        name="ab_out",
    )(x, ya, yb, mod, w_glu, b_glu, w_out)


def _halo_specs(tm, T, D):
    nh = T // HALO
    per = tm // HALO
    main = pl.BlockSpec((None, tm, D), lambda b, i: (b, i, 0))
    prev = pl.BlockSpec((None, HALO, D), lambda b, i: (b, jnp.maximum(i * per - 1, 0), 0))
    nxt = pl.BlockSpec((None, HALO, D), lambda b, i: (b, jnp.minimum((i + 1) * per, nh - 1), 0))
    return main, prev, nxt


def _ffn_kernel(x_ref, xp_ref, xn_ref, mod_ref, g_ref, wg_ref, wu_ref, cw_ref, cb_ref, wd_ref, o_ref, gbuf, abuf, *, tm):
    x = x_ref[...]
    g, shift, scale = g_ref[...], mod_ref[3:4, :], mod_ref[4:5, :]
    has_prev = jnp.where(pl.program_id(1) > 0, 1.0, 0.0)
    has_next = jnp.where(pl.program_id(1) < pl.num_programs(1) - 1, 1.0, 0.0)
    h = _norm_mod(x, g, shift, scale)
    he = jnp.concatenate([_norm_mod(xp_ref[...], g, shift, scale) * has_prev, h,
                          _norm_mod(xn_ref[...], g, shift, scale) * has_next], axis=0).astype(BF16)
    h = he[HALO:HALO + tm]
    for c in range(D_FF // FFN_CHUNK):
        cols = slice(c * FFN_CHUNK, (c + 1) * FFN_CHUNK)
        gbuf[:, cols] = _dot(he, wg_ref[:, cols])
        gate = (gbuf[pl.ds(HALO - 1, tm), cols] * cw_ref[0:1, cols] + gbuf[pl.ds(HALO, tm), cols] * cw_ref[1:2, cols]
                + gbuf[pl.ds(HALO + 1, tm), cols] * cw_ref[2:3, cols] + cb_ref[:, cols])
        up = _dot(h, wu_ref[:, cols])
        abuf[:, cols] = ((gate * _sigmoid(gate)) * up).astype(BF16)
    o_ref[...] = x + mod_ref[5:6, :] * _dot(abuf[...], wd_ref[...])


def _ffn(x, mod, g, w_gate, w_up, conv_w, conv_b, w_down, tm=512):
    B, T, D = x.shape
    main, prev, nxt = _halo_specs(tm, T, D)
    return pl.pallas_call(
        functools.partial(_ffn_kernel, tm=tm),
        grid=(B, T // tm),
        in_specs=[main, prev, nxt, pl.BlockSpec((None, 6, D), lambda b, i: (b, 0, 0)), _const_spec((1, D)),
                  _const_spec((D, D_FF)), _const_spec((D, D_FF)), _const_spec((3, D_FF)), _const_spec((1, D_FF)),
                  _const_spec((D_FF, D))],
        out_specs=main,
        out_shape=jax.ShapeDtypeStruct((B, T, D), F32),
        scratch_shapes=[pltpu.VMEM((tm + 2 * HALO, D_FF), F32), pltpu.VMEM((tm, D_FF), BF16)],
        compiler_params=_params(("parallel", "arbitrary"), 56),
        name="conv_ffn",
    )(x, x, x, mod, g, w_gate, w_up, conv_w, conv_b, w_down)


def _cd_in_kernel(x_ref, xp_ref, xn_ref, mod_ref, g_ref, w_ref, cbd_ref, sbd_ref, sw_ref, ab_ref, yd_ref, cbuf, *, tm):
    g, shift, scale = g_ref[...], mod_ref[0:1, :], mod_ref[1:2, :]
    has_prev = jnp.where(pl.program_id(1) > 0, 1.0, 0.0)
    has_next = jnp.where(pl.program_id(1) < pl.num_programs(1) - 1, 1.0, 0.0)
    he = jnp.concatenate([_norm_mod(xp_ref[...], g, shift, scale) * has_prev, _norm_mod(x_ref[...], g, shift, scale),
                          _norm_mod(xn_ref[...], g, shift, scale) * has_next], axis=0).astype(BF16)
    h = he[HALO:HALO + tm]
    for c0 in range(0, D_D, CD_CHUNK):
        cols = slice(c0, c0 + CD_CHUNK)
        hs = _dot(he, w_ref[:, D_C + c0:D_C + c0 + CD_CHUNK])
        gc = _dot(he, w_ref[:, D_C + 2 * D_D + c0:D_C + 2 * D_D + c0 + CD_CHUNK])
        cbuf[:, cols] = gc * hs
        conv = (cbuf[pl.ds(HALO - 1, tm), cols] * sw_ref[0:1, cols] + cbuf[pl.ds(HALO, tm), cols] * sw_ref[1:2, cols]
                + cbuf[pl.ds(HALO + 1, tm), cols] * sw_ref[2:3, cols])
        yd_ref[:, cols] = _dot(h, w_ref[:, D_C + D_D + c0:D_C + D_D + c0 + CD_CHUNK]) * conv
    uc = _dot(h, w_ref[:, 0:D_C]).astype(BF16)
    ab_ref[0] = _dot(uc, cbd_ref[...]).astype(BF16)
    ab_ref[1] = _dot(uc, sbd_ref[...]).astype(BF16)


def _cd_in(x, mod, g, w_in, cbd, sbd, sconv_w, tm=512):
    B, T, D = x.shape
    main, prev, nxt = _halo_specs(tm, T, D)
    return pl.pallas_call(
        functools.partial(_cd_in_kernel, tm=tm),
        grid=(B, T // tm),
        in_specs=[main, prev, nxt, pl.BlockSpec((None, 6, D), lambda b, i: (b, 0, 0)), _const_spec((1, D)),
                  _const_spec((D, w_in.shape[1])), _const_spec((D_C, D_C)), _const_spec((D_C, D_C)),
                  _const_spec((3, D_D))],
        out_specs=[pl.BlockSpec((None, 2, tm, D_C), lambda b, i: (b, 0, i, 0)),
                   pl.BlockSpec((None, tm, D_D), lambda b, i: (b, i, 0))],
        out_shape=[jax.ShapeDtypeStruct((B, 2, T, D_C), BF16), jax.ShapeDtypeStruct((B, T, D_D), F32)],
        scratch_shapes=[pltpu.VMEM((tm + 2 * HALO, D_D), F32)],
        compiler_params=_params(("parallel", "arbitrary"), 48),
        name="cd_in",
    )(x, x, x, mod, g, w_in, cbd, sbd, sconv_w)


def _cd_out_kernel(dft_ref, ab_ref, yd_ref, x_ref, mod_ref, wo_ref, o_ref, fold, *, T, tf):
    half = T // 2

    @pl.when(pl.program_id(1) == 0)
    def _():
        r = lax.broadcasted_iota(jnp.int32, (tf, tf + HALO), 0)
        c = lax.broadcasted_iota(jnp.int32, (tf, tf + HALO), 1)
        mirror = jnp.where(c == tf - r, 1.0, 0.0).astype(BF16)
        first_row = lax.broadcasted_iota(jnp.int32, (tf, D_C), 0) == 0
        for part, sign in ((0, 1.0), (1, -1.0)):
            for s0 in range(0, half, tf):
                nxt = ab_ref[part, T - s0:T - s0 + HALO, :] if s0 else jnp.zeros((HALO, D_C), BF16)
                src = jnp.concatenate([ab_ref[part, T - s0 - tf:T - s0, :], nxt], axis=0)
                folded = ab_ref[part, s0:s0 + tf, :].astype(F32) + sign * _dot(mirror, src)
                if part == 1 and s0 == 0:
                    folded = jnp.where(first_row, ab_ref[0, half:half + HALO, :].astype(F32)[0:1, :], folded)
                fold[part * half + s0:part * half + s0 + tf, :] = folded.astype(BF16)

    for r0 in range(0, x_ref.shape[0], ROW_BLOCK):
        rows = slice(r0, r0 + ROW_BLOCK)
        yc = _dot(dft_ref[rows, :], fold[...])
        out = _dot(yc.astype(BF16), wo_ref[0:D_C, :]) + _dot(yd_ref[rows, :].astype(BF16), wo_ref[D_C:, :])
        o_ref[rows, :] = x_ref[rows, :] + mod_ref[2:3, :] * out


def _cd_out(dft, ab, yd, x, mod, w_out, tm=512):
    B, T, D = x.shape
    return pl.pallas_call(
        functools.partial(_cd_out_kernel, T=T, tf=256),
        grid=(B, T // tm),
        in_specs=[pl.BlockSpec((tm, T), lambda b, i: (i, 0)),
                  pl.BlockSpec((None, 2, T, D_C), lambda b, i: (b, 0, 0, 0)),
                  pl.BlockSpec((None, tm, D_D), lambda b, i: (b, i, 0)),
                  pl.BlockSpec((None, tm, D), lambda b, i: (b, i, 0)),
                  pl.BlockSpec((None, 6, D), lambda b, i: (b, 0, 0)), _const_spec((D, D))],
        out_specs=pl.BlockSpec((None, tm, D), lambda b, i: (b, i, 0)),
        out_shape=jax.ShapeDtypeStruct((B, T, D), F32),
        scratch_shapes=[pltpu.VMEM((T, D_C), BF16)],
        compiler_params=_params(("parallel", "arbitrary"), 56),
        name="cd_out",
    )(dft, ab, yd, x, mod, w_out)


def _seq_dft_matrix(T):
    rows = 64
    s = jnp.arange(T // 2, dtype=jnp.int32)[None, :]
    angle = lambda t: ((t[:, None] * s) % T).astype(F32) * (2.0 * math.pi / T)
    ang_a = angle(jnp.arange(T // rows, dtype=jnp.int32) * rows)[:, None, :]
    ang_b = angle(jnp.arange(rows, dtype=jnp.int32))[None, :, :]
    scale = 1.0 / math.sqrt(T)
    cos_a, sin_a, cos_b, sin_b = jnp.cos(ang_a) * scale, jnp.sin(ang_a) * scale, jnp.cos(ang_b), jnp.sin(ang_b)
    cos_ts = (cos_a * cos_b - sin_a * sin_b).reshape(T, T // 2)
    sin_ts = (sin_a * cos_b + cos_a * sin_b).reshape(T, T // 2)
    nyquist = jnp.where(jnp.arange(T) % 2 == 0, scale, -scale)[:, None]
    return jnp.concatenate([cos_ts, jnp.where(s == 0, nyquist, -sin_ts)], axis=1).astype(BF16)


def _channel_dft_matrices():
    c = np.arange(D_C)
    ang = 2.0 * np.pi * ((c[:, None] % FNET_GROUP_DIM) * (c[None, :] % FNET_GROUP_DIM) % FNET_GROUP_DIM) / FNET_GROUP_DIM
    same = (c[:, None] // FNET_GROUP_DIM) == (c[None, :] // FNET_GROUP_DIM)
    scale = 1.0 / math.sqrt(FNET_GROUP_DIM)
    cbd = np.where(same, np.cos(ang) * scale, 0.0).astype(np.float32)
    sbd = np.where(same, np.sin(ang) * scale, 0.0).astype(np.float32)
    return jnp.asarray(cbd).astype(BF16), jnp.asarray(sbd).astype(BF16)


def _trunk(x, mods, wts):
    B, T, D = x.shape
    nch = T // S5_CHUNK
    mod = mods[0]
    u, q, k, v = _ab_in(x, mod, wts["norm_mix_g"][0], wts["ab_w_in"], wts["hsum"], wts["q_g"], wts["k_g"])
    ya = _s5(u, wts["s5_w1"], wts["s5_w2"], wts["s5_mult"])
    yb = _attention(q, k, v, wts["attn_bias"])
    x = _ab_out(x, ya, yb, mod, wts["s5_w_glu"], wts["s5_b_glu"], wts["ab_w_out"])
    x = _ffn(x, mod, wts["norm_ffn_g"][0], *wts["ffn"][0])
    mod = mods[1]
    ab, yd = _cd_in(x, mod, wts["norm_mix_g"][1], wts["cd_w_in"], wts["cbd"], wts["sbd"], wts["sconv_w"])
    x = _cd_out(_seq_dft_matrix(T), ab, yd, x, mod, wts["cd_w_out"])
    x = _ffn(x, mod, wts["norm_ffn_g"][1], *wts["ffn"][1])
    return x


def kernel(x_prompt, x_sample, c_prompt, c_sample, ada_w, ada_b, norm_mix_g, norm_ffn_g, ffn_w_gate, ffn_w_up, ffn_conv_w, ffn_conv_b, ffn_w_down, ab_w_in, ab_w_out, s5_lam_re_f, s5_lam_im_f, s5_log_dt_f, s5_lam_re_b, s5_lam_im_b, s5_log_dt_b, s5_b_re, s5_b_im, s5_c_re, s5_c_im, s5_d, s5_w_glu, s5_b_glu, q_norm_g, k_norm_g, cd_w_in, cd_w_out, sconv_w):
    depth = ada_w.shape[0]
    assert depth == 2 and ab_w_in.shape[0] == 1 and cd_w_in.shape[0] == 1
    bp = x_prompt.shape[0]
    mod_all = _ada(jnp.concatenate([c_prompt, c_sample], axis=0), ada_w, ada_b)
    s5_w1, s5_w2, s5_mult = _s5_tables(s5_lam_re_f[0], s5_lam_im_f[0], s5_log_dt_f[0], s5_lam_re_b[0],
                                       s5_lam_im_b[0], s5_log_dt_b[0], s5_b_re[0], s5_b_im[0],
                                       s5_c_re[0], s5_c_im[0], s5_d[0])
    head = np.arange(D_B) // HEAD_DIM
    cbd, sbd = _channel_dft_matrices()
    wts = dict(
        norm_mix_g=norm_mix_g.reshape(depth, 1, D_MODEL), norm_ffn_g=norm_ffn_g.reshape(depth, 1, D_MODEL),
        ab_w_in=ab_w_in[0].astype(BF16), ab_w_out=ab_w_out[0].astype(BF16),
        hsum=jnp.asarray(head[:, None] == head[None, :], BF16),
        q_g=jnp.tile(q_norm_g[0], N_HEADS).reshape(1, D_B), k_g=jnp.tile(k_norm_g[0], N_HEADS).reshape(1, D_B),
        s5_w1=s5_w1, s5_w2=s5_w2, s5_mult=s5_mult,
        s5_w_glu=s5_w_glu[0].astype(BF16), s5_b_glu=s5_b_glu[0].reshape(1, D_A),
        attn_bias=jnp.asarray(_attn_bias_table()),
        ffn=[(ffn_w_gate[l].astype(BF16), ffn_w_up[l].astype(BF16), ffn_conv_w[l], ffn_conv_b[l].reshape(1, D_FF),
              ffn_w_down[l].astype(BF16)) for l in range(depth)],
        cd_w_in=cd_w_in[0].astype(BF16), cd_w_out=cd_w_out[0].astype(BF16), cbd=cbd, sbd=sbd, sconv_w=sconv_w[0],
    )
    outs = []
    for x, rows in ((x_prompt, slice(0, bp)), (x_sample, slice(bp, None))):
        mods = [mod_all[l, rows].reshape(x.shape[0], 6, D_MODEL) for l in range(depth)]
        outs.append(_trunk(x, mods, wts))
    return tuple(outs)
```

```python
import functools
import math

import jax
import jax.numpy as jnp
import numpy as np
from jax import lax
from jax.experimental import pallas as pl
from jax.experimental.pallas import tpu as pltpu

F32 = jnp.float32
BF16 = jnp.bfloat16

D_MODEL = 1024
D_A = 512
S5_GROUP = 16
S5_GROUPS = 32
S5_STATE = 64
S5_CHUNK = 16
S5_LEVELS = 8
S5_LANE_GROUPS = 128 // S5_GROUP
D_B = 512
HEAD_DIM = 64
N_HEADS = 8
DILATED_CONFIGS = ((128, 1), (512, 4), (2048, 16))
ATTN_R = 64
ATTN_QB = 128
ATTN_KW = ATTN_QB + 2 * ATTN_R
ATTN_PAD = ATTN_R * 16
ATTN_UNROLL = 8
NEG_BIG = -1e30
LOG2_E = 1.4426950408889634
D_C = 512
FNET_GROUP_DIM = 128
D_D = 512
D_FF = 2816
FFN_CHUNK = 256
CD_CHUNK = 256
EPS = 1e-6
ROW_BLOCK = 256
TILE_ROWS = 8
HALO = 16
MIB = 2 ** 20


def _params(sem, vmem_mib):
    return pltpu.CompilerParams(dimension_semantics=sem, vmem_limit_bytes=vmem_mib * MIB)


def _const_spec(shape):
    nd = len(shape)
    return pl.BlockSpec(shape, lambda *_: (0,) * nd, pipeline_mode=pl.Buffered(1))


def _dot(a, b):
    return jnp.dot(a, b, preferred_element_type=F32)


def _norm_mod(x, g, shift, scale):
    ms = jnp.mean(x * x, axis=-1, keepdims=True)
    y = x * lax.rsqrt(ms + EPS) * g
    return y * (1.0 + scale) + shift


def _sigmoid(x):
    return 1.0 / (1.0 + jnp.exp(-x))


def _ada_kernel(c_ref, w_ref, b_ref, o_ref):
    c = c_ref[...]
    cond = (c * _sigmoid(c)).astype(BF16)
    o_ref[...] = _dot(cond, w_ref[...].astype(BF16)) + b_ref[...]


def _ada(c_all, ada_w, ada_b):
    depth, d, n = ada_w.shape
    rows = c_all.shape[0]
    tn = 1536
    return pl.pallas_call(
        _ada_kernel,
        grid=(depth, n // tn),
        in_specs=[pl.BlockSpec((rows, d), lambda l, j: (0, 0)),
                  pl.BlockSpec((None, d, tn), lambda l, j: (l, 0, j)),
                  pl.BlockSpec((None, 1, tn), lambda l, j: (l, 0, j))],
        out_specs=pl.BlockSpec((None, rows, tn), lambda l, j: (l, 0, j)),
        out_shape=jax.ShapeDtypeStruct((depth, rows, n), F32),
        compiler_params=_params(("parallel", "parallel"), 32),
        name="ada_mod",
    )(c_all, ada_w, ada_b.reshape(depth, 1, n))


def _ab_in_kernel(x_ref, mod_ref, g_ref, w_ref, hsum_ref, qg_ref, kg_ref, u_ref, q_ref, k_ref, v_ref):
    hsum = hsum_ref[...]

    def head_norm(a, g):
        ms = _dot((a * a).astype(BF16), hsum) * (1.0 / HEAD_DIM)
        return a * lax.rsqrt(ms + EPS) * g

    for r0 in range(0, x_ref.shape[0], ROW_BLOCK):
        rows = slice(r0, r0 + ROW_BLOCK)
        h = _norm_mod(x_ref[rows, :], g_ref[...], mod_ref[0:1, :], mod_ref[1:2, :])
        z = _dot(h.astype(BF16), w_ref[...])
        u_ref[rows, :] = z[:, 0:D_A]
        q = head_norm(z[:, D_A:D_A + D_B], qg_ref[...]) * (LOG2_E / math.sqrt(HEAD_DIM))
        k = head_norm(z[:, D_A + D_B:D_A + 2 * D_B], kg_ref[...])
        half_rows = slice(r0 // 2, (r0 + ROW_BLOCK) // 2)
        for ref, val in ((q_ref, q), (k_ref, k), (v_ref, z[:, D_A + 2 * D_B:])):
            for parity in range(2):
                ref[parity, half_rows, :] = _tiles_of_parity(val, parity)


def _tiles_of_parity(rows, parity):
    return jnp.concatenate([rows[t * TILE_ROWS:(t + 1) * TILE_ROWS]
                            for t in range(parity, rows.shape[0] // TILE_ROWS, 2)], axis=0)


def _interleave_tiles(even, odd):
    pieces = []
    for t in range(even.shape[0] // TILE_ROWS):
        pieces += [even[t * TILE_ROWS:(t + 1) * TILE_ROWS], odd[t * TILE_ROWS:(t + 1) * TILE_ROWS]]
    return jnp.concatenate(pieces, axis=0)


def _ab_in(x, mod, g, w_in, hsum, qg, kg, tm=512):
    B, T, D = x.shape
    n = w_in.shape[1]
    tok = pl.BlockSpec((None, tm, D_B), lambda b, i: (b, i, 0))
    split = pl.BlockSpec((None, 2, tm // 2, D_B), lambda b, i: (b, 0, i, 0))
    out = jax.ShapeDtypeStruct((B, T, D_B), F32)
    out_split = jax.ShapeDtypeStruct((B, 2, T // 2, D_B), F32)
    return pl.pallas_call(
        _ab_in_kernel,
        grid=(B, T // tm),
        in_specs=[pl.BlockSpec((None, tm, D), lambda b, i: (b, i, 0)),
                  pl.BlockSpec((None, 6, D), lambda b, i: (b, 0, 0)),
                  _const_spec((1, D)), _const_spec((D, n)), _const_spec((D_B, D_B)),
                  _const_spec((1, D_B)), _const_spec((1, D_B))],
        out_specs=[tok, split, split, split],
        out_shape=[out, out_split, out_split, out_split],
        compiler_params=_params(("parallel", "parallel"), 48),
        name="ab_in",
    )(x, mod, g, w_in, hsum, qg, kg)


def _cmul(a, b):
    return a[0] * b[0] - a[1] * b[1], a[0] * b[1] + a[1] * b[0]


def _s5_param_kernel(lam_ref, dt_ref, b_re_ref, b_im_ref, ct_re_ref, ct_im_ref, c_re_ref, c_im_ref, d_ref,
                     wf_re_ref, wf_im_ref, wb_re_ref, wb_im_ref, kf_ref, kb_ref,
                     caf_re_ref, caf_im_ref, cab_re_ref, cab_im_ref, ap_ref):
    P, W = S5_STATE, S5_CHUNK * S5_GROUP
    kidx = lax.broadcasted_iota(jnp.int32, (P, W), 1) // S5_GROUP
    lane_ap = lax.broadcasted_iota(jnp.int32, (P, 128), 1)
    ap = jnp.zeros((P, 128), F32)
    c_re, c_im = c_re_ref[...], c_im_ref[...]
    outs = ((wf_re_ref, wf_im_ref, kf_ref, caf_re_ref, caf_im_ref),
            (wb_re_ref, wb_im_ref, kb_ref, cab_re_ref, cab_im_ref))
    for direction in range(2):
        lam_re = lam_ref[:, 2 * direction:2 * direction + 1]
        lam_im = lam_ref[:, 2 * direction + 1:2 * direction + 2]
        dt = jnp.exp(dt_ref[:, direction:direction + 1])
        mag = jnp.exp(lam_re * dt)
        a_re = mag * jnp.cos(lam_im * dt)
        a_im = mag * jnp.sin(lam_im * dt)
        den = lam_re * lam_re + lam_im * lam_im
        coef_re = ((a_re - 1.0) * lam_re + a_im * lam_im) / den
        coef_im = (a_im * lam_re - (a_re - 1.0) * lam_im) / den
        bb = _cmul((coef_re, coef_im), (b_re_ref[...], b_im_ref[...]))
        a_pow = (a_re, a_im)
        pw = (jnp.ones((P, W), F32), jnp.zeros((P, W), F32))
        for j in range(4):
            bit = ((kidx >> j) & 1) == 1
            pw = _cmul(pw, (jnp.where(bit, a_pow[0], 1.0), jnp.where(bit, a_pow[1], 0.0)))
            a_pow = _cmul(a_pow, a_pow)
        for j in range(S5_LEVELS):
            base = 4 * j + 2 * direction
            ap = jnp.where(lane_ap == base, a_pow[0], ap)
            ap = jnp.where(lane_ap == base + 1, a_pow[1], ap)
            a_pow = _cmul(a_pow, a_pow)
        w_re, w_im = _cmul(pw, bb)
        ca_re, ca_im = _cmul(_cmul(pw, (a_re, a_im)), (ct_re_ref[...], ct_im_ref[...]))
        hp = lax.Precision.HIGHEST
        kmat = (jnp.dot(c_re, w_re, precision=hp, preferred_element_type=F32)
                - jnp.dot(c_im, w_im, precision=hp, preferred_element_type=F32))
        o_w_re, o_w_im, o_k, o_ca_re, o_ca_im = outs[direction]
        o_w_re[...] = w_re
        o_w_im[...] = w_im
        o_k[...] = kmat
        o_ca_re[...] = ca_re
        o_ca_im[...] = -ca_im
    row = lax.broadcasted_iota(jnp.int32, (S5_GROUP, W), 0)
    lane = lax.broadcasted_iota(jnp.int32, (S5_GROUP, W), 1)
    lag0 = kb_ref[...] + jnp.where(row == lane, d_ref[...], 0.0)
    kf_ref[...] = kf_ref[...] + jnp.where(lane < S5_GROUP, lag0, 0.0)
    ap_ref[...] = ap


def _s5_tables(lam_re_f, lam_im_f, log_dt_f, lam_re_b, lam_im_b, log_dt_b, b_re, b_im, c_re, c_im, d_skip):
    G, P, C, L, W = S5_GROUPS, S5_STATE, S5_GROUP, S5_CHUNK, S5_CHUNK * S5_GROUP
    lam = jnp.stack([lam_re_f, lam_im_f, lam_re_b, lam_im_b], axis=-1)
    dts = jnp.stack([log_dt_f, log_dt_b], axis=-1).reshape(G, 1, 2)
    tile_k = lambda a: jnp.tile(a, (1, 1, L))
    ct = lambda a: tile_k(jnp.swapaxes(a, 1, 2))
    grp = lambda r, c: pl.BlockSpec((None, r, c), lambda g: (g, 0, 0))
    pw_out = jax.ShapeDtypeStruct((G, P, W), F32)
    k_out = jax.ShapeDtypeStruct((G, C, W), F32)
    wf_re, wf_im, wb_re, wb_im, kf, kb, caf_re, caf_im, cab_re, cab_im, ap = pl.pallas_call(
        _s5_param_kernel,
        grid=(G,),
        in_specs=[grp(P, 4), grp(1, 2), grp(P, W), grp(P, W), grp(P, W), grp(P, W), grp(C, P), grp(C, P), grp(C, 1)],
        out_specs=[grp(P, W)] * 4 + [grp(C, W)] * 2 + [grp(P, W)] * 4 + [grp(P, 128)],
        out_shape=[pw_out] * 4 + [k_out] * 2 + [pw_out] * 4 + [jax.ShapeDtypeStruct((G, P, 128), F32)],
        compiler_params=_params(("parallel",), 32),
        name="s5_params",
    )(lam, dts, tile_k(b_re), tile_k(b_im), ct(c_re), ct(c_im), c_re, c_im, d_skip.reshape(G, C, 1))

    NG, J, hp = S5_LANE_GROUPS, G // S5_LANE_GROUPS, lax.Precision.HIGHEST
    order = _s5_token_order()
    place = (order[:, :, None] == np.arange(L)).astype(np.float32)
    place_rev = (order[:, :, None] == L - 1 - np.arange(L)).astype(np.float32)
    lag = order[:, None, :] - order[:, :, None] + (L - 1)
    lag_hot = (lag[..., None] == np.arange(2 * L - 1)).astype(np.float32)
    lag_tab = jnp.concatenate([kb.reshape(G, C, L, C)[:, :, :0:-1], kf.reshape(G, C, L, C)], axis=2)
    toep = jnp.einsum("xstl,jxoli->jxsito", lag_hot, lag_tab.reshape(J, NG, C, 2 * L - 1, C), precision=hp)
    toep = toep.reshape(G, W, W)

    def slots(a, hot):
        return jnp.einsum("xqk,jxpkc->jxpqc", hot, a.reshape(J, NG, P, L, C), precision=hp).reshape(G, P, W)

    t_ = lambda a: jnp.swapaxes(a, 1, 2)
    sf_re, sf_im = t_(slots(wf_re, place_rev)), t_(slots(wf_im, place_rev))
    sb_re, sb_im = t_(slots(wb_re, place)), t_(slots(wb_im, place))
    w1 = jnp.concatenate([toep, sf_re, sf_im, sf_im, sf_re, sb_re, sb_im, sb_im, sb_re], axis=2)

    def pair_rows(a):
        a = a.reshape(G // 2, 2, P, W)
        zero = jnp.zeros_like(a[:, 0])
        return jnp.concatenate([jnp.concatenate([a[:, 0], zero], axis=2),
                                jnp.concatenate([zero, a[:, 1]], axis=2)], axis=1)

    w2 = jnp.concatenate([pair_rows(slots(caf_re, place)), pair_rows(slots(caf_im, place)),
                          pair_rows(slots(cab_re, place_rev)), pair_rows(slots(cab_im, place_rev))], axis=1)
    apj = ap[:, :, :4 * S5_LEVELS].reshape(G // 2, 2, P, S5_LEVELS, 4).transpose(0, 3, 4, 1, 2)
    mult = apj.reshape(G // 2, S5_LEVELS, 4 * 2 * P)
    return w1.astype(BF16), w2.astype(BF16), mult


def _s5_token_order():
    NG = S5_LANE_GROUPS
    order = np.empty((NG, S5_CHUNK), np.int64)
    for gl in range(NG):
        for half in range(S5_CHUNK // NG):
            for blk in range(NG):
                order[gl, half * NG + blk] = half * NG + (blk - gl) % NG
    return order


def _s5_kernel(u_ref, w1_ref, w2_ref, mult_ref, y_ref, ucbuf, ybuf, *, nch, nlev):
    row = lax.broadcasted_iota(jnp.int32, (nch, 128), 0)
    lane_blk = lax.broadcasted_iota(jnp.int32, (nch, 128), 1) // S5_GROUP
    W, L, C, NG = S5_CHUNK * S5_GROUP, S5_CHUNK, S5_GROUP, S5_LANE_GROUPS

    def pick(slabs, shift):
        acc = slabs[(0 - shift) % NG]
        for blk in range(1, NG):
            acc = jnp.where(lane_blk == blk, slabs[(blk - shift) % NG], acc)
        return acc

    def shift_down(x, s):
        if s % 8 == 0:
            return jnp.concatenate([jnp.zeros((s, 128), F32), x[:nch - s]], axis=0)
        return jnp.where(row >= s, pltpu.roll(x, s, 0), 0.0)

    def shift_up(x, s):
        if s % 8 == 0:
            return jnp.concatenate([x[s:], jnp.zeros((s, 128), F32)], axis=0)
        return jnp.where(row < nch - s, pltpu.roll(x, nch - s, 0), 0.0)

    for half in range(L // NG):
        rolled = []
        for k in range(NG):
            slab = u_ref[pl.ds(half * NG + k, nch, stride=L), :]
            rolled.append(pltpu.roll(slab, k * C, 1) if k else slab)
        for gl in range(NG):
            ucbuf[gl, :, half * 128:(half + 1) * 128] = pick(rolled, gl).astype(BF16)

    first_group = lax.broadcasted_iota(jnp.int32, (nch, 128), 1) < S5_STATE

    for m in range(NG // 2):
        r0 = _dot(ucbuf[2 * m], w1_ref[2 * m])
        r1 = _dot(ucbuf[2 * m + 1], w1_ref[2 * m + 1])
        xf_re = jnp.where(first_group, r0[:, W:W + 128], r1[:, W + 128:W + 256])
        xf_im = jnp.where(first_group, r0[:, W + 128:W + 256], r1[:, W:W + 128])
        xb_re = jnp.where(first_group, r0[:, W + 256:W + 384], r1[:, W + 384:W + 512])
        xb_im = jnp.where(first_group, r0[:, W + 384:W + 512], r1[:, W + 256:W + 384])
        mult = mult_ref[m]
        for j in range(nlev):
            s = 1 << j
            a_re, a_im = mult[j:j + 1, 0:128], mult[j:j + 1, 128:256]
            s_re, s_im = shift_down(xf_re, s), shift_down(xf_im, s)
            xf_re, xf_im = xf_re + (a_re * s_re - a_im * s_im), xf_im + (a_re * s_im + a_im * s_re)
            a_re, a_im = mult[j:j + 1, 256:384], mult[j:j + 1, 384:512]
            s_re, s_im = shift_up(xb_re, s), shift_up(xb_im, s)
            xb_re, xb_im = xb_re + (a_re * s_re - a_im * s_im), xb_im + (a_re * s_im + a_im * s_re)
        xs = jnp.concatenate([shift_down(xf_re, 1), shift_down(xf_im, 1), shift_up(xb_re, 1), shift_up(xb_im, 1)],
                             axis=1).astype(BF16)
        y = _dot(xs, w2_ref[m])
        ybuf[2 * m] = r0[:, 0:W] + y[:, 0:W]
        ybuf[2 * m + 1] = r1[:, 0:W] + y[:, W:]

    for half in range(L // NG):
        for k in range(NG):
            slab = pick([ybuf[gl, :, half * 128:(half + 1) * 128] for gl in range(NG)], k)
            y_ref[pl.ds(half * NG + k, nch, stride=L), :] = pltpu.roll(slab, (NG - k) * C, 1) if k else slab


def _s5(u, w1, w2, mult):
    B, T, D = u.shape
    nch, W, NG = T // S5_CHUNK, S5_CHUNK * S5_GROUP, S5_LANE_GROUPS
    nlev = int(math.log2(nch))
    assert 1 << nlev == nch and nlev <= S5_LEVELS
    tok = pl.BlockSpec((None, T, 128), lambda b, j: (b, 0, j))
    return pl.pallas_call(
        functools.partial(_s5_kernel, nch=nch, nlev=nlev),
        grid=(B, D // 128),
        in_specs=[tok,
                  pl.BlockSpec((NG, W, 3 * W), lambda b, j: (j, 0, 0)),
                  pl.BlockSpec((NG // 2, 2 * W, 2 * W), lambda b, j: (j, 0, 0)),
                  pl.BlockSpec((NG // 2, S5_LEVELS, 2 * W), lambda b, j: (j, 0, 0))],
        out_specs=tok,
        out_shape=jax.ShapeDtypeStruct((B, T, D), F32),
        scratch_shapes=[pltpu.VMEM((NG, nch, W), BF16), pltpu.VMEM((NG, nch, W), F32)],
        compiler_params=_params(("parallel", "parallel"), 32),
        name="s5_scan",
    )(u, w1, w2, mult)


def _attn_block_positions(d):
    a, c = np.arange(ATTN_QB), np.arange(ATTN_KW)
    if d == 1:
        token = lambda row, slab: 16 * ((row % slab) // TILE_ROWS) + TILE_ROWS * (row // slab) + row % TILE_ROWS
        return token(a, ATTN_QB // 2), token(c, ATTN_KW // 2) - ATTN_R
    if d == 4:
        return 4 * (a % (ATTN_QB // 4)) + a // (ATTN_QB // 4), 4 * (c % (ATTN_KW // 4)) + c // (ATTN_KW // 4) - ATTN_R
    return a, c - ATTN_R


def _attn_bias_table():
    slopes = 2.0 ** (-8.0 * np.arange(1, N_HEADS + 1, dtype=np.float64) / N_HEADS)
    tab = np.empty((N_HEADS // 2, len(DILATED_CONFIGS), 4, 2, ATTN_QB, ATTN_KW), np.float32)
    for bi, (_, d) in enumerate(DILATED_CONFIGS):
        pos_q, pos_k = _attn_block_positions(d)
        rel = pos_k[None, :] - pos_q[:, None]
        for ty in range(4):
            valid = np.abs(rel) <= ATTN_R
            if ty & 1:
                valid = valid & (pos_k[None, :] >= 0)
            if ty & 2:
                valid = valid & (pos_k[None, :] < ATTN_QB)
            for h in range(N_HEADS):
                dist = (np.abs(rel) * d).astype(np.float32)
                alibi = (-np.float32(slopes[h] * LOG2_E)) * dist
                bias = np.where(valid, alibi, np.float32(NEG_BIG))
                if ty == 3:
                    bias = np.concatenate([bias[:, ATTN_R:ATTN_R + ATTN_QB],
                                           np.full((ATTN_QB, 2 * ATTN_R), NEG_BIG, np.float32)], axis=1)
                tab[h // 2, bi, ty, h % 2] = bias
    return tab.reshape(N_HEADS // 2, len(DILATED_CONFIGS) * 4, 2 * ATTN_QB, ATTN_KW)


def _attn_kernel(q_ref, k_ref, v_ref, bias_ref, o_ref, kp, vp, acc_o, acc_m, acc_l, *, T):
    half, pad = T // 2, ATTN_PAD // 2
    zeros = jnp.zeros((pad, 128), F32)
    for buf, src in ((kp, k_ref), (vp, v_ref)):
        for parity in range(2):
            buf[parity, 0:pad, :] = zeros
            buf[parity, pad + half:pad + half + pad, :] = zeros
            buf[parity, pad:pad + half, :] = src[parity]
    first_head = lax.broadcasted_iota(jnp.int32, (ATTN_QB, 128), 1) < HEAD_DIM
    ones = jnp.ones((ATTN_KW, 128), BF16)
    n_branch = len(DILATED_CONFIGS)

    def block_slabs(d, nblk, it, u):
        idx = it * ATTN_UNROLL + u
        tile = lambda n: pl.multiple_of(TILE_ROWS * n, TILE_ROWS)
        if d == 1:
            blk, nq, nr = idx, ATTN_QB // 2, ATTN_R // 2
            q = [(p, pl.ds(tile(nq // TILE_ROWS * blk), nq), nq) for p in range(2)]
            k = [(p, pl.ds(pad - nr + tile(nq // TILE_ROWS * blk), 2 * nq), 2 * nq) for p in range(2)]
        elif d == 4:
            r, blk, nq, nr = u % d, idx // d, ATTN_QB // 4, ATTN_R // 4
            sub = lambda ph: r + 4 * (ph % 2)
            q = [(ph // 2, pl.ds(tile(nq * blk) + sub(ph), nq, stride=TILE_ROWS), nq) for ph in range(4)]
            k = [(ph // 2, pl.ds(pad - TILE_ROWS * nr + tile(nq * blk) + sub(ph), 2 * nq, stride=TILE_ROWS), 2 * nq)
                 for ph in range(4)]
        else:
            assert d == 2 * TILE_ROWS and ATTN_UNROLL % TILE_ROWS == 0
            sub, parity, blk = u % TILE_ROWS, (idx // TILE_ROWS) % 2, idx // d
            q = [(parity, pl.ds(tile(ATTN_QB * blk) + sub, ATTN_QB, stride=TILE_ROWS), ATTN_QB)]
            back, nk = (0, ATTN_QB) if nblk == 1 else (TILE_ROWS * ATTN_R, ATTN_KW)
            k = [(parity, pl.ds(pad - back + tile(ATTN_QB * blk) + sub, nk, stride=TILE_ROWS), nk)]
        return q, k, blk

    def gather(ref, slabs):
        return jnp.concatenate([ref[p, rows, :] for p, rows, _ in slabs], axis=0)

    def scatter(ref, slabs, val):
        start = 0
        for p, rows, n in slabs:
            ref[p, rows, :] = val[start:start + n]
            start += n

    def block_softmax(bi, d, nblk, it, u):
        rows, keys, blk = block_slabs(d, nblk, it, u)
        qf = gather(q_ref, rows)
        q2 = jnp.concatenate([jnp.where(first_head, qf, 0.0), jnp.where(first_head, 0.0, qf)], axis=0).astype(BF16)
        if nblk == 1:
            bias = bias_ref[bi * 4 + 3, :, 0:ATTN_QB]
        else:
            bias = bias_ref[bi * 4 + jnp.where(blk == 0, 1, 0) + jnp.where(blk == nblk - 1, 2, 0)]
        kw = gather(kp, keys).astype(BF16)
        s = lax.dot_general(q2, kw, (((1,), (1,)), ((), ())), preferred_element_type=F32)
        s = s + bias
        m = jnp.max(s, axis=-1, keepdims=True)
        p = jnp.exp2(s - m).astype(BF16)
        ov = _dot(p, jnp.concatenate([gather(vp, keys).astype(BF16), ones[0:kw.shape[0]]], axis=1))
        o_new = jnp.where(first_head, ov[0:ATTN_QB, 0:128], ov[ATTN_QB:, 0:128])
        l_new = jnp.where(first_head, ov[0:ATTN_QB, 128:], ov[ATTN_QB:, 128:])
        m_new = jnp.where(first_head, m[0:ATTN_QB], m[ATTN_QB:])
        return rows, o_new, m_new, l_new

    for bi, (_, d) in enumerate(DILATED_CONFIGS):
        nblk = T // d // ATTN_QB

        def group(it, carry, bi=bi, d=d, nblk=nblk):
            new = [block_softmax(bi, d, nblk, it, u) for u in range(ATTN_UNROLL)]
            if bi == 0:
                for rows, o_new, m_new, l_new in new:
                    scatter(acc_o, rows, o_new)
                    scatter(acc_m, rows, m_new)
                    scatter(acc_l, rows, l_new)
                return carry
            old = [(gather(acc_o, rows), gather(acc_m, rows), gather(acc_l, rows)) for rows, _, _, _ in new]
            merged = []
            for (rows, o_new, m_new, l_new), (o_old, m_old, l_old) in zip(new, old):
                e = jnp.exp2(-jnp.abs(m_old - m_new))
                keep = m_old >= m_new
                a = jnp.where(keep, 1.0, e)
                b = jnp.where(keep, e, 1.0)
                merged.append((rows, a * o_old + b * o_new, jnp.maximum(m_old, m_new), a * l_old + b * l_new))
            for rows, o_tot, m_tot, l_tot in merged:
                if bi == n_branch - 1:
                    scatter(o_ref, rows, o_tot / l_tot)
                else:
                    scatter(acc_o, rows, o_tot)
                    scatter(acc_m, rows, m_tot)
                    scatter(acc_l, rows, l_tot)
            return carry

        lax.fori_loop(0, d * nblk // ATTN_UNROLL, group, 0)


def _attention(q, k, v, bias):
    B, _, half, _ = q.shape
    T = 2 * half
    assert T % (ATTN_QB * DILATED_CONFIGS[-1][1]) == 0
    pair = pl.BlockSpec((None, 2, half, 128), lambda b, p: (b, 0, 0, p))
    nb = bias.shape[1]
    acc = pltpu.VMEM((2, half, 128), F32)
    padded = pltpu.VMEM((2, half + ATTN_PAD, 128), F32)
    return pl.pallas_call(
        functools.partial(_attn_kernel, T=T),
        grid=(B, N_HEADS // 2),
        in_specs=[pair, pair, pair, pl.BlockSpec((None, nb, 2 * ATTN_QB, ATTN_KW), lambda b, p: (p, 0, 0, 0))],
        out_specs=pair,
        out_shape=jax.ShapeDtypeStruct((B, 2, half, D_B), F32),
        scratch_shapes=[padded, padded, acc, acc, acc],
        compiler_params=_params(("parallel", "arbitrary"), 48),
        name="dilated_attn",
    )(q, k, v, bias)


def _ab_out_kernel(x_ref, ya_ref, yb_ref, mod_ref, wglu_ref, bglu_ref, wo_ref, o_ref):
    for r0 in range(0, x_ref.shape[0], ROW_BLOCK):
        rows = slice(r0, r0 + ROW_BLOCK)
        y = ya_ref[rows, :]
        y = 0.5 * y * (1.0 + jnp.tanh(math.sqrt(2.0 / math.pi) * (y + 0.044715 * (y * y * y))))
        y = y * _sigmoid(_dot(y.astype(BF16), wglu_ref[...]) + bglu_ref[...])
        half_rows = slice(r0 // 2, (r0 + ROW_BLOCK) // 2)
        yb = _interleave_tiles(yb_ref[0, half_rows, :], yb_ref[1, half_rows, :])
        out = _dot(y.astype(BF16), wo_ref[0:D_A, :]) + _dot(yb.astype(BF16), wo_ref[D_A:, :])
        o_ref[rows, :] = x_ref[rows, :] + mod_ref[2:3, :] * out


def _ab_out(x, ya, yb, mod, w_glu, b_glu, w_out, tm=512):
    B, T, D = x.shape
    half = pl.BlockSpec((None, tm, D_A), lambda b, i: (b, i, 0))
    split = pl.BlockSpec((None, 2, tm // 2, D_B), lambda b, i: (b, 0, i, 0))
    full = pl.BlockSpec((None, tm, D), lambda b, i: (b, i, 0))
    return pl.pallas_call(
        _ab_out_kernel,
        grid=(B, T // tm),
        in_specs=[full, half, split, pl.BlockSpec((None, 6, D), lambda b, i: (b, 0, 0)),
                  _const_spec((D_A, D_A)), _const_spec((1, D_A)), _const_spec((D, D))],
        out_specs=full,
        out_shape=jax.ShapeDtypeStruct((B, T, D), F32),
        compiler_params=_params(("parallel", "parallel"), 48),
        name="ab_out",
    )(x, ya, yb, mod, w_glu, b_glu, w_out)


def _halo_specs(tm, T, D):
    nh = T // HALO
    per = tm // HALO
    main = pl.BlockSpec((None, tm, D), lambda b, i: (b, i, 0))
    prev = pl.BlockSpec((None, HALO, D), lambda b, i: (b, jnp.maximum(i * per - 1, 0), 0))
    nxt = pl.BlockSpec((None, HALO, D), lambda b, i: (b, jnp.minimum((i + 1) * per, nh - 1), 0))
    return main, prev, nxt


def _ffn_kernel(x_ref, xp_ref, xn_ref, mod_ref, g_ref, wg_ref, wu_ref, cw_ref, cb_ref, wd_ref, o_ref, gbuf, abuf, *, tm):
    x = x_ref[...]
    g, shift, scale = g_ref[...], mod_ref[3:4, :], mod_ref[4:5, :]
    has_prev = jnp.where(pl.program_id(1) > 0, 1.0, 0.0)
    has_next = jnp.where(pl.program_id(1) < pl.num_programs(1) - 1, 1.0, 0.0)
    h = _norm_mod(x, g, shift, scale)
    he = jnp.concatenate([_norm_mod(xp_ref[...], g, shift, scale) * has_prev, h,
                          _norm_mod(xn_ref[...], g, shift, scale) * has_next], axis=0).astype(BF16)
    h = he[HALO:HALO + tm]
    for c in range(D_FF // FFN_CHUNK):
        cols = slice(c * FFN_CHUNK, (c + 1) * FFN_CHUNK)
        gbuf[:, cols] = _dot(he, wg_ref[:, cols])
        gate = (gbuf[pl.ds(HALO - 1, tm), cols] * cw_ref[0:1, cols] + gbuf[pl.ds(HALO, tm), cols] * cw_ref[1:2, cols]
                + gbuf[pl.ds(HALO + 1, tm), cols] * cw_ref[2:3, cols] + cb_ref[:, cols])
        up = _dot(h, wu_ref[:, cols])
        abuf[:, cols] = ((gate * _sigmoid(gate)) * up).astype(BF16)
    o_ref[...] = x + mod_ref[5:6, :] * _dot(abuf[...], wd_ref[...])


def _ffn(x, mod, g, w_gate, w_up, conv_w, conv_b, w_down, tm=512):
    B, T, D = x.shape
    main, prev, nxt = _halo_specs(tm, T, D)
    return pl.pallas_call(
        functools.partial(_ffn_kernel, tm=tm),
        grid=(B, T // tm),
        in_specs=[main, prev, nxt, pl.BlockSpec((None, 6, D), lambda b, i: (b, 0, 0)), _const_spec((1, D)),
                  _const_spec((D, D_FF)), _const_spec((D, D_FF)), _const_spec((3, D_FF)), _const_spec((1, D_FF)),
                  _const_spec((D_FF, D))],
        out_specs=main,
        out_shape=jax.ShapeDtypeStruct((B, T, D), F32),
        scratch_shapes=[pltpu.VMEM((tm + 2 * HALO, D_FF), F32), pltpu.VMEM((tm, D_FF), BF16)],
        compiler_params=_params(("parallel", "arbitrary"), 56),
        name="conv_ffn",
    )(x, x, x, mod, g, w_gate, w_up, conv_w, conv_b, w_down)


def _cd_in_kernel(x_ref, xp_ref, xn_ref, mod_ref, g_ref, w_ref, cbd_ref, sbd_ref, sw_ref, ab_ref, yd_ref, cbuf, *, tm):
    g, shift, scale = g_ref[...], mod_ref[0:1, :], mod_ref[1:2, :]
    has_prev = jnp.where(pl.program_id(1) > 0, 1.0, 0.0)
    has_next = jnp.where(pl.program_id(1) < pl.num_programs(1) - 1, 1.0, 0.0)
    he = jnp.concatenate([_norm_mod(xp_ref[...], g, shift, scale) * has_prev, _norm_mod(x_ref[...], g, shift, scale),
                          _norm_mod(xn_ref[...], g, shift, scale) * has_next], axis=0).astype(BF16)
    h = he[HALO:HALO + tm]
    for c0 in range(0, D_D, CD_CHUNK):
        cols = slice(c0, c0 + CD_CHUNK)
        hs = _dot(he, w_ref[:, D_C + c0:D_C + c0 + CD_CHUNK])
        gc = _dot(he, w_ref[:, D_C + 2 * D_D + c0:D_C + 2 * D_D + c0 + CD_CHUNK])
        cbuf[:, cols] = gc * hs
        conv = (cbuf[pl.ds(HALO - 1, tm), cols] * sw_ref[0:1, cols] + cbuf[pl.ds(HALO, tm), cols] * sw_ref[1:2, cols]
                + cbuf[pl.ds(HALO + 1, tm), cols] * sw_ref[2:3, cols])
        yd_ref[:, cols] = _dot(h, w_ref[:, D_C + D_D + c0:D_C + D_D + c0 + CD_CHUNK]) * conv
    uc = _dot(h, w_ref[:, 0:D_C]).astype(BF16)
    ab_ref[0] = _dot(uc, cbd_ref[...]).astype(BF16)
    ab_ref[1] = _dot(uc, sbd_ref[...]).astype(BF16)


def _cd_in(x, mod, g, w_in, cbd, sbd, sconv_w, tm=512):
    B, T, D = x.shape
    main, prev, nxt = _halo_specs(tm, T, D)
    return pl.pallas_call(
        functools.partial(_cd_in_kernel, tm=tm),
        grid=(B, T // tm),
        in_specs=[main, prev, nxt, pl.BlockSpec((None, 6, D), lambda b, i: (b, 0, 0)), _const_spec((1, D)),
                  _const_spec((D, w_in.shape[1])), _const_spec((D_C, D_C)), _const_spec((D_C, D_C)),
                  _const_spec((3, D_D))],
        out_specs=[pl.BlockSpec((None, 2, tm, D_C), lambda b, i: (b, 0, i, 0)),
                   pl.BlockSpec((None, tm, D_D), lambda b, i: (b, i, 0))],
        out_shape=[jax.ShapeDtypeStruct((B, 2, T, D_C), BF16), jax.ShapeDtypeStruct((B, T, D_D), F32)],
        scratch_shapes=[pltpu.VMEM((tm + 2 * HALO, D_D), F32)],
        compiler_params=_params(("parallel", "arbitrary"), 48),
        name="cd_in",
    )(x, x, x, mod, g, w_in, cbd, sbd, sconv_w)


def _cd_out_kernel(dft_ref, ab_ref, yd_ref, x_ref, mod_ref, wo_ref, o_ref, fold, *, T, tf):
    half = T // 2

    @pl.when(pl.program_id(1) == 0)
    def _():
        r = lax.broadcasted_iota(jnp.int32, (tf, tf + HALO), 0)
        c = lax.broadcasted_iota(jnp.int32, (tf, tf + HALO), 1)
        mirror = jnp.where(c == tf - r, 1.0, 0.0).astype(BF16)
        first_row = lax.broadcasted_iota(jnp.int32, (tf, D_C), 0) == 0
        for part, sign in ((0, 1.0), (1, -1.0)):
            for s0 in range(0, half, tf):
                nxt = ab_ref[part, T - s0:T - s0 + HALO, :] if s0 else jnp.zeros((HALO, D_C), BF16)
                src = jnp.concatenate([ab_ref[part, T - s0 - tf:T - s0, :], nxt], axis=0)
                folded = ab_ref[part, s0:s0 + tf, :].astype(F32) + sign * _dot(mirror, src)
                if part == 1 and s0 == 0:
                    folded = jnp.where(first_row, ab_ref[0, half:half + HALO, :].astype(F32)[0:1, :], folded)
                fold[part * half + s0:part * half + s0 + tf, :] = folded.astype(BF16)

    for r0 in range(0, x_ref.shape[0], ROW_BLOCK):
        rows = slice(r0, r0 + ROW_BLOCK)
        yc = _dot(dft_ref[rows, :], fold[...])
        out = _dot(yc.astype(BF16), wo_ref[0:D_C, :]) + _dot(yd_ref[rows, :].astype(BF16), wo_ref[D_C:, :])
        o_ref[rows, :] = x_ref[rows, :] + mod_ref[2:3, :] * out


def _cd_out(dft, ab, yd, x, mod, w_out, tm=512):
    B, T, D = x.shape
    return pl.pallas_call(
        functools.partial(_cd_out_kernel, T=T, tf=256),
        grid=(B, T // tm),
        in_specs=[pl.BlockSpec((tm, T), lambda b, i: (i, 0)),
                  pl.BlockSpec((None, 2, T, D_C), lambda b, i: (b, 0, 0, 0)),
                  pl.BlockSpec((None, tm, D_D), lambda b, i: (b, i, 0)),
                  pl.BlockSpec((None, tm, D), lambda b, i: (b, i, 0)),
                  pl.BlockSpec((None, 6, D), lambda b, i: (b, 0, 0)), _const_spec((D, D))],
        out_specs=pl.BlockSpec((None, tm, D), lambda b, i: (b, i, 0)),
        out_shape=jax.ShapeDtypeStruct((B, T, D), F32),
        scratch_shapes=[pltpu.VMEM((T, D_C), BF16)],
        compiler_params=_params(("parallel", "arbitrary"), 56),
        name="cd_out",
    )(dft, ab, yd, x, mod, w_out)


def _seq_dft_matrix(T):
    rows = 64
    s = jnp.arange(T // 2, dtype=jnp.int32)[None, :]
    angle = lambda t: ((t[:, None] * s) % T).astype(F32) * (2.0 * math.pi / T)
    ang_a = angle(jnp.arange(T // rows, dtype=jnp.int32) * rows)[:, None, :]
    ang_b = angle(jnp.arange(rows, dtype=jnp.int32))[None, :, :]
    scale = 1.0 / math.sqrt(T)
    cos_a, sin_a, cos_b, sin_b = jnp.cos(ang_a) * scale, jnp.sin(ang_a) * scale, jnp.cos(ang_b), jnp.sin(ang_b)
    cos_ts = (cos_a * cos_b - sin_a * sin_b).reshape(T, T // 2)
    sin_ts = (sin_a * cos_b + cos_a * sin_b).reshape(T, T // 2)
    nyquist = jnp.where(jnp.arange(T) % 2 == 0, scale, -scale)[:, None]
    return jnp.concatenate([cos_ts, jnp.where(s == 0, nyquist, -sin_ts)], axis=1).astype(BF16)


def _channel_dft_matrices():
    c = np.arange(D_C)
    ang = 2.0 * np.pi * ((c[:, None] % FNET_GROUP_DIM) * (c[None, :] % FNET_GROUP_DIM) % FNET_GROUP_DIM) / FNET_GROUP_DIM
    same = (c[:, None] // FNET_GROUP_DIM) == (c[None, :] // FNET_GROUP_DIM)
    scale = 1.0 / math.sqrt(FNET_GROUP_DIM)
    cbd = np.where(same, np.cos(ang) * scale, 0.0).astype(np.float32)
    sbd = np.where(same, np.sin(ang) * scale, 0.0).astype(np.float32)
    return jnp.asarray(cbd).astype(BF16), jnp.asarray(sbd).astype(BF16)


def _trunk(x, mods, wts):
    B, T, D = x.shape
    nch = T // S5_CHUNK
    mod = mods[0]
    u, q, k, v = _ab_in(x, mod, wts["norm_mix_g"][0], wts["ab_w_in"], wts["hsum"], wts["q_g"], wts["k_g"])
    ya = _s5(u, wts["s5_w1"], wts["s5_w2"], wts["s5_mult"])
    yb = _attention(q, k, v, wts["attn_bias"])
    x = _ab_out(x, ya, yb, mod, wts["s5_w_glu"], wts["s5_b_glu"], wts["ab_w_out"])
    x = _ffn(x, mod, wts["norm_ffn_g"][0], *wts["ffn"][0])
    mod = mods[1]
    ab, yd = _cd_in(x, mod, wts["norm_mix_g"][1], wts["cd_w_in"], wts["cbd"], wts["sbd"], wts["sconv_w"])
    x = _cd_out(_seq_dft_matrix(T), ab, yd, x, mod, wts["cd_w_out"])
    x = _ffn(x, mod, wts["norm_ffn_g"][1], *wts["ffn"][1])
    return x


def kernel(x_prompt, x_sample, c_prompt, c_sample, ada_w, ada_b, norm_mix_g, norm_ffn_g, ffn_w_gate, ffn_w_up, ffn_conv_w, ffn_conv_b, ffn_w_down, ab_w_in, ab_w_out, s5_lam_re_f, s5_lam_im_f, s5_log_dt_f, s5_lam_re_b, s5_lam_im_b, s5_log_dt_b, s5_b_re, s5_b_im, s5_c_re, s5_c_im, s5_d, s5_w_glu, s5_b_glu, q_norm_g, k_norm_g, cd_w_in, cd_w_out, sconv_w):
    depth = ada_w.shape[0]
    assert depth == 2 and ab_w_in.shape[0] == 1 and cd_w_in.shape[0] == 1
    bp = x_prompt.shape[0]
    mod_all = _ada(jnp.concatenate([c_prompt, c_sample], axis=0), ada_w, ada_b)
    s5_w1, s5_w2, s5_mult = _s5_tables(s5_lam_re_f[0], s5_lam_im_f[0], s5_log_dt_f[0], s5_lam_re_b[0],
                                       s5_lam_im_b[0], s5_log_dt_b[0], s5_b_re[0], s5_b_im[0],
                                       s5_c_re[0], s5_c_im[0], s5_d[0])
    head = np.arange(D_B) // HEAD_DIM
    cbd, sbd = _channel_dft_matrices()
    wts = dict(
        norm_mix_g=norm_mix_g.reshape(depth, 1, D_MODEL), norm_ffn_g=norm_ffn_g.reshape(depth, 1, D_MODEL),
        ab_w_in=ab_w_in[0].astype(BF16), ab_w_out=ab_w_out[0].astype(BF16),
        hsum=jnp.asarray(head[:, None] == head[None, :], BF16),
        q_g=jnp.tile(q_norm_g[0], N_HEADS).reshape(1, D_B), k_g=jnp.tile(k_norm_g[0], N_HEADS).reshape(1, D_B),
        s5_w1=s5_w1, s5_w2=s5_w2, s5_mult=s5_mult,
        s5_w_glu=s5_w_glu[0].astype(BF16), s5_b_glu=s5_b_glu[0].reshape(1, D_A),
        attn_bias=jnp.asarray(_attn_bias_table()),
        ffn=[(ffn_w_gate[l].astype(BF16), ffn_w_up[l].astype(BF16), ffn_conv_w[l], ffn_conv_b[l].reshape(1, D_FF),
              ffn_w_down[l].astype(BF16)) for l in range(depth)],
        cd_w_in=cd_w_in[0].astype(BF16), cd_w_out=cd_w_out[0].astype(BF16), cbd=cbd, sbd=sbd, sconv_w=sconv_w[0],
    )
    outs = []
    for x, rows in ((x_prompt, slice(0, bp)), (x_sample, slice(bp, None))):
        mods = [mod_all[l, rows].reshape(x.shape[0], 6, D_MODEL) for l in range(depth)]
        outs.append(_trunk(x, mods, wts))
    return tuple(outs)
```

```python
import functools
import math

import jax
import jax.numpy as jnp
import numpy as np
from jax import lax
from jax.experimental import pallas as pl
from jax.experimental.pallas import tpu as pltpu

F32 = jnp.float32
BF16 = jnp.bfloat16

D_MODEL = 1024
D_A = 512
S5_GROUP = 16
S5_GROUPS = 32
S5_STATE = 64
S5_CHUNK = 16
S5_LEVELS = 8
S5_LANE_GROUPS = 128 // S5_GROUP
D_B = 512
HEAD_DIM = 64
N_HEADS = 8
DILATED_CONFIGS = ((128, 1), (512, 4), (2048, 16))
ATTN_R = 64
ATTN_QB = 128
ATTN_KW = ATTN_QB + 2 * ATTN_R
ATTN_PAD = ATTN_R * 16
ATTN_UNROLL = {1: 16, 4: 16, 16: 16}
NEG_BIG = -1e30
LOG2_E = 1.4426950408889634
D_C = 512
FNET_GROUP_DIM = 128
D_D = 512
D_FF = 2816
FFN_CHUNK = 256
CD_CHUNK = 256
EPS = 1e-6
ROW_BLOCK = 256
TILE_ROWS = 8
HALO = 16
MIB = 2 ** 20


def _params(sem, vmem_mib):
    return pltpu.CompilerParams(dimension_semantics=sem, vmem_limit_bytes=vmem_mib * MIB)


def _const_spec(shape):
    nd = len(shape)
    return pl.BlockSpec(shape, lambda *_: (0,) * nd, pipeline_mode=pl.Buffered(1))


def _dot(a, b):
    return jnp.dot(a, b, preferred_element_type=F32)


def _norm_mod(x, g, shift, scale):
    ms = jnp.mean(x * x, axis=-1, keepdims=True)
    y = x * lax.rsqrt(ms + EPS) * g
    return y * (1.0 + scale) + shift


def _sigmoid(x):
    return 1.0 / (1.0 + jnp.exp(-x))


def _ada_kernel(c_ref, w_ref, b_ref, o_ref):
    c = c_ref[...]
    cond = (c * _sigmoid(c)).astype(BF16)
    o_ref[...] = _dot(cond, w_ref[...].astype(BF16)) + b_ref[...]


def _ada(c_all, ada_w, ada_b):
    depth, d, n = ada_w.shape
    rows = c_all.shape[0]
    tn = 1536
    return pl.pallas_call(
        _ada_kernel,
        grid=(depth, n // tn),
        in_specs=[pl.BlockSpec((rows, d), lambda l, j: (0, 0)),
                  pl.BlockSpec((None, d, tn), lambda l, j: (l, 0, j)),
                  pl.BlockSpec((None, 1, tn), lambda l, j: (l, 0, j))],
        out_specs=pl.BlockSpec((None, rows, tn), lambda l, j: (l, 0, j)),
        out_shape=jax.ShapeDtypeStruct((depth, rows, n), F32),
        compiler_params=_params(("parallel", "parallel"), 32),
        name="ada_mod",
    )(c_all, ada_w, ada_b.reshape(depth, 1, n))


def _ab_in_kernel(x_ref, mod_ref, g_ref, w_ref, hsum_ref, qg_ref, kg_ref, u_ref, q_ref, k_ref, v_ref):
    hsum = hsum_ref[...]

    def head_norm(a, g):
        ms = _dot((a * a).astype(BF16), hsum) * (1.0 / HEAD_DIM)
        return a * lax.rsqrt(ms + EPS) * g

    for r0 in range(0, x_ref.shape[0], ROW_BLOCK):
        rows = slice(r0, r0 + ROW_BLOCK)
        h = _norm_mod(x_ref[rows, :], g_ref[...], mod_ref[0:1, :], mod_ref[1:2, :])
        z = _dot(h.astype(BF16), w_ref[...])
        u_ref[rows, :] = z[:, 0:D_A]
        q = head_norm(z[:, D_A:D_A + D_B], qg_ref[...]) * (LOG2_E / math.sqrt(HEAD_DIM))
        k = head_norm(z[:, D_A + D_B:D_A + 2 * D_B], kg_ref[...])
        half_rows = slice(r0 // 2, (r0 + ROW_BLOCK) // 2)
        for ref, val in ((q_ref, q), (k_ref, k), (v_ref, z[:, D_A + 2 * D_B:])):
            for parity in range(2):
                ref[parity, half_rows, :] = _tiles_of_parity(val, parity)


def _tiles_of_parity(rows, parity):
    return jnp.concatenate([rows[t * TILE_ROWS:(t + 1) * TILE_ROWS]
                            for t in range(parity, rows.shape[0] // TILE_ROWS, 2)], axis=0)


def _interleave_tiles(even, odd):
    pieces = []
    for t in range(even.shape[0] // TILE_ROWS):
        pieces += [even[t * TILE_ROWS:(t + 1) * TILE_ROWS], odd[t * TILE_ROWS:(t + 1) * TILE_ROWS]]
    return jnp.concatenate(pieces, axis=0)


def _ab_in(x, mod, g, w_in, hsum, qg, kg, tm=512):
    B, T, D = x.shape
    n = w_in.shape[1]
    tok = pl.BlockSpec((None, tm, D_B), lambda b, i: (b, i, 0))
    split = pl.BlockSpec((None, 2, tm // 2, D_B), lambda b, i: (b, 0, i, 0))
    out = jax.ShapeDtypeStruct((B, T, D_B), F32)
    out_split = jax.ShapeDtypeStruct((B, 2, T // 2, D_B), F32)
    return pl.pallas_call(
        _ab_in_kernel,
        grid=(B, T // tm),
        in_specs=[pl.BlockSpec((None, tm, D), lambda b, i: (b, i, 0)),
                  pl.BlockSpec((None, 6, D), lambda b, i: (b, 0, 0)),
                  _const_spec((1, D)), _const_spec((D, n)), _const_spec((D_B, D_B)),
                  _const_spec((1, D_B)), _const_spec((1, D_B))],
        out_specs=[tok, split, split, split],
        out_shape=[out, out_split, out_split, out_split],
        compiler_params=_params(("parallel", "parallel"), 48),
        name="ab_in",
    )(x, mod, g, w_in, hsum, qg, kg)


def _cmul(a, b):
    return a[0] * b[0] - a[1] * b[1], a[0] * b[1] + a[1] * b[0]


def _s5_param_kernel(lam_ref, dt_ref, b_re_ref, b_im_ref, ct_re_ref, ct_im_ref, c_re_ref, c_im_ref, d_ref,
                     wf_re_ref, wf_im_ref, wb_re_ref, wb_im_ref, kf_ref, kb_ref,
                     caf_re_ref, caf_im_ref, cab_re_ref, cab_im_ref, ap_ref):
    P, W = S5_STATE, S5_CHUNK * S5_GROUP
    kidx = lax.broadcasted_iota(jnp.int32, (P, W), 1) // S5_GROUP
    lane_ap = lax.broadcasted_iota(jnp.int32, (P, 128), 1)
    ap = jnp.zeros((P, 128), F32)
    c_re, c_im = c_re_ref[...], c_im_ref[...]
    outs = ((wf_re_ref, wf_im_ref, kf_ref, caf_re_ref, caf_im_ref),
            (wb_re_ref, wb_im_ref, kb_ref, cab_re_ref, cab_im_ref))
    for direction in range(2):
        lam_re = lam_ref[:, 2 * direction:2 * direction + 1]
        lam_im = lam_ref[:, 2 * direction + 1:2 * direction + 2]
        dt = jnp.exp(dt_ref[:, direction:direction + 1])
        mag = jnp.exp(lam_re * dt)
        a_re = mag * jnp.cos(lam_im * dt)
        a_im = mag * jnp.sin(lam_im * dt)
        den = lam_re * lam_re + lam_im * lam_im
        coef_re = ((a_re - 1.0) * lam_re + a_im * lam_im) / den
        coef_im = (a_im * lam_re - (a_re - 1.0) * lam_im) / den
        bb = _cmul((coef_re, coef_im), (b_re_ref[...], b_im_ref[...]))
        a_pow = (a_re, a_im)
        pw = (jnp.ones((P, W), F32), jnp.zeros((P, W), F32))
        for j in range(4):
            bit = ((kidx >> j) & 1) == 1
            pw = _cmul(pw, (jnp.where(bit, a_pow[0], 1.0), jnp.where(bit, a_pow[1], 0.0)))
            a_pow = _cmul(a_pow, a_pow)
        for j in range(S5_LEVELS):
            base = 4 * j + 2 * direction
            ap = jnp.where(lane_ap == base, a_pow[0], ap)
            ap = jnp.where(lane_ap == base + 1, a_pow[1], ap)
            a_pow = _cmul(a_pow, a_pow)
        w_re, w_im = _cmul(pw, bb)
        ca_re, ca_im = _cmul(_cmul(pw, (a_re, a_im)), (ct_re_ref[...], ct_im_ref[...]))
        hp = lax.Precision.HIGHEST
        kmat = (jnp.dot(c_re, w_re, precision=hp, preferred_element_type=F32)
                - jnp.dot(c_im, w_im, precision=hp, preferred_element_type=F32))
        o_w_re, o_w_im, o_k, o_ca_re, o_ca_im = outs[direction]
        o_w_re[...] = w_re
        o_w_im[...] = w_im
        o_k[...] = kmat
        o_ca_re[...] = ca_re
        o_ca_im[...] = -ca_im
    row = lax.broadcasted_iota(jnp.int32, (S5_GROUP, W), 0)
    lane = lax.broadcasted_iota(jnp.int32, (S5_GROUP, W), 1)
    lag0 = kb_ref[...] + jnp.where(row == lane, d_ref[...], 0.0)
    kf_ref[...] = kf_ref[...] + jnp.where(lane < S5_GROUP, lag0, 0.0)
    ap_ref[...] = ap


def _s5_tables(lam_re_f, lam_im_f, log_dt_f, lam_re_b, lam_im_b, log_dt_b, b_re, b_im, c_re, c_im, d_skip):
    G, P, C, L, W = S5_GROUPS, S5_STATE, S5_GROUP, S5_CHUNK, S5_CHUNK * S5_GROUP
    lam = jnp.stack([lam_re_f, lam_im_f, lam_re_b, lam_im_b], axis=-1)
    dts = jnp.stack([log_dt_f, log_dt_b], axis=-1).reshape(G, 1, 2)
    tile_k = lambda a: jnp.tile(a, (1, 1, L))
    ct = lambda a: tile_k(jnp.swapaxes(a, 1, 2))
    grp = lambda r, c: pl.BlockSpec((None, r, c), lambda g: (g, 0, 0))
    pw_out = jax.ShapeDtypeStruct((G, P, W), F32)
    k_out = jax.ShapeDtypeStruct((G, C, W), F32)
    wf_re, wf_im, wb_re, wb_im, kf, kb, caf_re, caf_im, cab_re, cab_im, ap = pl.pallas_call(
        _s5_param_kernel,
        grid=(G,),
        in_specs=[grp(P, 4), grp(1, 2), grp(P, W), grp(P, W), grp(P, W), grp(P, W), grp(C, P), grp(C, P), grp(C, 1)],
        out_specs=[grp(P, W)] * 4 + [grp(C, W)] * 2 + [grp(P, W)] * 4 + [grp(P, 128)],
        out_shape=[pw_out] * 4 + [k_out] * 2 + [pw_out] * 4 + [jax.ShapeDtypeStruct((G, P, 128), F32)],
        compiler_params=_params(("parallel",), 32),
        name="s5_params",
    )(lam, dts, tile_k(b_re), tile_k(b_im), ct(c_re), ct(c_im), c_re, c_im, d_skip.reshape(G, C, 1))

    NG, J, hp = S5_LANE_GROUPS, G // S5_LANE_GROUPS, lax.Precision.HIGHEST
    order = _s5_token_order()
    place = (order[:, :, None] == np.arange(L)).astype(np.float32)
    place_rev = (order[:, :, None] == L - 1 - np.arange(L)).astype(np.float32)
    lag = order[:, None, :] - order[:, :, None] + (L - 1)
    lag_hot = (lag[..., None] == np.arange(2 * L - 1)).astype(np.float32)
    lag_tab = jnp.concatenate([kb.reshape(G, C, L, C)[:, :, :0:-1], kf.reshape(G, C, L, C)], axis=2)
    toep = jnp.einsum("xstl,jxoli->jxsito", lag_hot, lag_tab.reshape(J, NG, C, 2 * L - 1, C), precision=hp)
    toep = toep.reshape(G, W, W)

    def slots(a, hot):
        return jnp.einsum("xqk,jxpkc->jxpqc", hot, a.reshape(J, NG, P, L, C), precision=hp).reshape(G, P, W)

    t_ = lambda a: jnp.swapaxes(a, 1, 2)
    sf_re, sf_im = t_(slots(wf_re, place_rev)), t_(slots(wf_im, place_rev))
    sb_re, sb_im = t_(slots(wb_re, place)), t_(slots(wb_im, place))
    w1 = jnp.concatenate([toep, sf_re, sf_im, sf_im, sf_re, sb_re, sb_im, sb_im, sb_re], axis=2)

    def pair_rows(a):
        a = a.reshape(G // 2, 2, P, W)
        zero = jnp.zeros_like(a[:, 0])
        return jnp.concatenate([jnp.concatenate([a[:, 0], zero], axis=2),
                                jnp.concatenate([zero, a[:, 1]], axis=2)], axis=1)

    w2 = jnp.concatenate([pair_rows(slots(caf_re, place)), pair_rows(slots(caf_im, place)),
                          pair_rows(slots(cab_re, place_rev)), pair_rows(slots(cab_im, place_rev))], axis=1)
    apj = ap[:, :, :4 * S5_LEVELS].reshape(G // 2, 2, P, S5_LEVELS, 4).transpose(0, 3, 4, 1, 2)
    mult = apj.reshape(G // 2, S5_LEVELS, 4 * 2 * P)
    return w1.astype(BF16), w2.astype(BF16), mult


def _s5_token_order():
    NG = S5_LANE_GROUPS
    order = np.empty((NG, S5_CHUNK), np.int64)
    for gl in range(NG):
        for half in range(S5_CHUNK // NG):
            for blk in range(NG):
                order[gl, half * NG + blk] = half * NG + (blk - gl) % NG
    return order


def _s5_kernel(u_ref, w1_ref, w2_ref, mult_ref, y_ref, ucbuf, ybuf, *, nch, nlev):
    row = lax.broadcasted_iota(jnp.int32, (nch, 128), 0)
    lane_blk = lax.broadcasted_iota(jnp.int32, (nch, 128), 1) // S5_GROUP
    W, L, C, NG = S5_CHUNK * S5_GROUP, S5_CHUNK, S5_GROUP, S5_LANE_GROUPS

    def pick(slabs, shift):
        acc = slabs[(0 - shift) % NG]
        for blk in range(1, NG):
            acc = jnp.where(lane_blk == blk, slabs[(blk - shift) % NG], acc)
        return acc

    def shift_down(x, s):
        if s % 8 == 0:
            return jnp.concatenate([jnp.zeros((s, 128), F32), x[:nch - s]], axis=0)
        return jnp.where(row >= s, pltpu.roll(x, s, 0), 0.0)

    def shift_up(x, s):
        if s % 8 == 0:
            return jnp.concatenate([x[s:], jnp.zeros((s, 128), F32)], axis=0)
        return jnp.where(row < nch - s, pltpu.roll(x, nch - s, 0), 0.0)

    for half in range(L // NG):
        rolled = []
        for k in range(NG):
            slab = u_ref[pl.ds(half * NG + k, nch, stride=L), :]
            rolled.append(pltpu.roll(slab, k * C, 1) if k else slab)
        for gl in range(NG):
            ucbuf[gl, :, half * 128:(half + 1) * 128] = pick(rolled, gl).astype(BF16)

    first_group = lax.broadcasted_iota(jnp.int32, (nch, 128), 1) < S5_STATE

    for m in range(NG // 2):
        r0 = _dot(ucbuf[2 * m], w1_ref[2 * m])
        r1 = _dot(ucbuf[2 * m + 1], w1_ref[2 * m + 1])
        xf_re = jnp.where(first_group, r0[:, W:W + 128], r1[:, W + 128:W + 256])
        xf_im = jnp.where(first_group, r0[:, W + 128:W + 256], r1[:, W:W + 128])
        xb_re = jnp.where(first_group, r0[:, W + 256:W + 384], r1[:, W + 384:W + 512])
        xb_im = jnp.where(first_group, r0[:, W + 384:W + 512], r1[:, W + 256:W + 384])
        mult = mult_ref[m]
        for j in range(nlev):
            s = 1 << j
            a_re, a_im = mult[j:j + 1, 0:128], mult[j:j + 1, 128:256]
            s_re, s_im = shift_down(xf_re, s), shift_down(xf_im, s)
            xf_re, xf_im = xf_re + (a_re * s_re - a_im * s_im), xf_im + (a_re * s_im + a_im * s_re)
            a_re, a_im = mult[j:j + 1, 256:384], mult[j:j + 1, 384:512]
            s_re, s_im = shift_up(xb_re, s), shift_up(xb_im, s)
            xb_re, xb_im = xb_re + (a_re * s_re - a_im * s_im), xb_im + (a_re * s_im + a_im * s_re)
        xs = jnp.concatenate([shift_down(xf_re, 1), shift_down(xf_im, 1), shift_up(xb_re, 1), shift_up(xb_im, 1)],
                             axis=1).astype(BF16)
        y = _dot(xs, w2_ref[m])
        ybuf[2 * m] = r0[:, 0:W] + y[:, 0:W]
        ybuf[2 * m + 1] = r1[:, 0:W] + y[:, W:]

    for half in range(L // NG):
        for k in range(NG):
            slab = pick([ybuf[gl, :, half * 128:(half + 1) * 128] for gl in range(NG)], k)
            y_ref[pl.ds(half * NG + k, nch, stride=L), :] = pltpu.roll(slab, (NG - k) * C, 1) if k else slab


def _s5(u, w1, w2, mult):
    B, T, D = u.shape
    nch, W, NG = T // S5_CHUNK, S5_CHUNK * S5_GROUP, S5_LANE_GROUPS
    nlev = int(math.log2(nch))
    assert 1 << nlev == nch and nlev <= S5_LEVELS
    tok = pl.BlockSpec((None, T, 128), lambda b, j: (b, 0, j))
    return pl.pallas_call(
        functools.partial(_s5_kernel, nch=nch, nlev=nlev),
        grid=(B, D // 128),
        in_specs=[tok,
                  pl.BlockSpec((NG, W, 3 * W), lambda b, j: (j, 0, 0)),
                  pl.BlockSpec((NG // 2, 2 * W, 2 * W), lambda b, j: (j, 0, 0)),
                  pl.BlockSpec((NG // 2, S5_LEVELS, 2 * W), lambda b, j: (j, 0, 0))],
        out_specs=tok,
        out_shape=jax.ShapeDtypeStruct((B, T, D), F32),
        scratch_shapes=[pltpu.VMEM((NG, nch, W), BF16), pltpu.VMEM((NG, nch, W), F32)],
        compiler_params=_params(("parallel", "parallel"), 32),
        name="s5_scan",
    )(u, w1, w2, mult)


def _attn_block_positions(d):
    a, c = np.arange(ATTN_QB), np.arange(ATTN_KW)
    if d == 1:
        token = lambda row, slab: 16 * ((row % slab) // TILE_ROWS) + TILE_ROWS * (row // slab) + row % TILE_ROWS
        return token(a, ATTN_QB // 2), token(c, ATTN_KW // 2) - ATTN_R
    if d == 4:
        return 4 * (a % (ATTN_QB // 4)) + a // (ATTN_QB // 4), 4 * (c % (ATTN_KW // 4)) + c // (ATTN_KW // 4) - ATTN_R
    return a, c - ATTN_R


def _attn_bias_table():
    slopes = 2.0 ** (-8.0 * np.arange(1, N_HEADS + 1, dtype=np.float64) / N_HEADS)
    tab = np.empty((N_HEADS // 2, len(DILATED_CONFIGS), 4, 2, ATTN_QB, ATTN_KW), np.float32)
    for bi, (_, d) in enumerate(DILATED_CONFIGS):
        pos_q, pos_k = _attn_block_positions(d)
        rel = pos_k[None, :] - pos_q[:, None]
        for ty in range(4):
            valid = np.abs(rel) <= ATTN_R
            if ty & 1:
                valid = valid & (pos_k[None, :] >= 0)
            if ty & 2:
                valid = valid & (pos_k[None, :] < ATTN_QB)
            for h in range(N_HEADS):
                dist = (np.abs(rel) * d).astype(np.float32)
                alibi = (-np.float32(slopes[h] * LOG2_E)) * dist
                bias = np.where(valid, alibi, np.float32(NEG_BIG))
                if ty == 3:
                    bias = np.concatenate([bias[:, ATTN_R:ATTN_R + ATTN_QB],
                                           np.full((ATTN_QB, 2 * ATTN_R), NEG_BIG, np.float32)], axis=1)
                tab[h // 2, bi, ty, h % 2] = bias
    return tab.reshape(N_HEADS // 2, len(DILATED_CONFIGS) * 4, 2 * ATTN_QB, ATTN_KW)


def _attn_kernel(q_ref, k_ref, v_ref, bias_ref, o_ref, kp, vp, acc_o, acc_m, acc_l, *, T):
    half, pad = T // 2, ATTN_PAD // 2
    zeros = jnp.zeros((pad, 128), F32)
    for buf, src in ((kp, k_ref), (vp, v_ref)):
        for parity in range(2):
            buf[parity, 0:pad, :] = zeros
            buf[parity, pad + half:pad + half + pad, :] = zeros
            buf[parity, pad:pad + half, :] = src[parity]
    first_head = lax.broadcasted_iota(jnp.int32, (ATTN_QB, 128), 1) < HEAD_DIM
    ones = jnp.ones((ATTN_KW, 128), BF16)
    n_branch = len(DILATED_CONFIGS)

    def block_slabs(d, nblk, it, u):
        idx = it * ATTN_UNROLL[d] + u
        tile = lambda n: pl.multiple_of(TILE_ROWS * n, TILE_ROWS)
        if d == 1:
            blk, nq, nr = idx, ATTN_QB // 2, ATTN_R // 2
            q = [(p, pl.ds(tile(nq // TILE_ROWS * blk), nq), nq) for p in range(2)]
            k = [(p, pl.ds(pad - nr + tile(nq // TILE_ROWS * blk), 2 * nq), 2 * nq) for p in range(2)]
        elif d == 4:
            r, blk, nq, nr = u % d, idx // d, ATTN_QB // 4, ATTN_R // 4
            sub = lambda ph: r + 4 * (ph % 2)
            q = [(ph // 2, pl.ds(tile(nq * blk) + sub(ph), nq, stride=TILE_ROWS), nq) for ph in range(4)]
            k = [(ph // 2, pl.ds(pad - TILE_ROWS * nr + tile(nq * blk) + sub(ph), 2 * nq, stride=TILE_ROWS), 2 * nq)
                 for ph in range(4)]
        else:
            assert d == 2 * TILE_ROWS and ATTN_UNROLL[d] % TILE_ROWS == 0
            sub, parity, blk = u % TILE_ROWS, (idx // TILE_ROWS) % 2, idx // d
            q = [(parity, pl.ds(tile(ATTN_QB * blk) + sub, ATTN_QB, stride=TILE_ROWS), ATTN_QB)]
            back, nk = (0, ATTN_QB) if nblk == 1 else (TILE_ROWS * ATTN_R, ATTN_KW)
            k = [(parity, pl.ds(pad - back + tile(ATTN_QB * blk) + sub, nk, stride=TILE_ROWS), nk)]
        return q, k, blk

    def gather(ref, slabs):
        return jnp.concatenate([ref[p, rows, :] for p, rows, _ in slabs], axis=0)

    def scatter(ref, slabs, val):
        start = 0
        for p, rows, n in slabs:
            ref[p, rows, :] = val[start:start + n]
            start += n

    def block_softmax(bi, d, nblk, it, u):
        rows, keys, blk = block_slabs(d, nblk, it, u)
        qf = gather(q_ref, rows)
        q2 = jnp.concatenate([jnp.where(first_head, qf, 0.0), jnp.where(first_head, 0.0, qf)], axis=0).astype(BF16)
        if nblk == 1:
            bias = bias_ref[bi * 4 + 3, :, 0:ATTN_QB]
        else:
            bias = bias_ref[bi * 4 + jnp.where(blk == 0, 1, 0) + jnp.where(blk == nblk - 1, 2, 0)]
        kw = gather(kp, keys).astype(BF16)
        s = lax.dot_general(q2, kw, (((1,), (1,)), ((), ())), preferred_element_type=F32)
        s = s + bias
        m = jnp.max(s, axis=-1, keepdims=True)
        p = jnp.exp2(s - m).astype(BF16)
        ov = _dot(p, jnp.concatenate([gather(vp, keys).astype(BF16), ones[0:kw.shape[0]]], axis=1))
        o_new = jnp.where(first_head, ov[0:ATTN_QB, 0:128], ov[ATTN_QB:, 0:128])
        l_new = jnp.where(first_head, ov[0:ATTN_QB, 128:], ov[ATTN_QB:, 128:])
        m_new = jnp.where(first_head, m[0:ATTN_QB], m[ATTN_QB:])
        return rows, o_new, m_new, l_new

    for bi, (_, d) in enumerate(DILATED_CONFIGS):
        nblk = T // d // ATTN_QB

        def group(it, carry, bi=bi, d=d, nblk=nblk):
            new = [block_softmax(bi, d, nblk, it, u) for u in range(ATTN_UNROLL[d])]
            if bi == 0:
                for rows, o_new, m_new, l_new in new:
                    scatter(acc_o, rows, o_new)
                    scatter(acc_m, rows, m_new)
                    scatter(acc_l, rows, l_new)
                return carry
            old = [(gather(acc_o, rows), gather(acc_m, rows), gather(acc_l, rows)) for rows, _, _, _ in new]
            merged = []
            for (rows, o_new, m_new, l_new), (o_old, m_old, l_old) in zip(new, old):
                e = jnp.exp2(-jnp.abs(m_old - m_new))
                keep = m_old >= m_new
                a = jnp.where(keep, 1.0, e)
                b = jnp.where(keep, e, 1.0)
                merged.append((rows, a * o_old + b * o_new, jnp.maximum(m_old, m_new), a * l_old + b * l_new))
            for rows, o_tot, m_tot, l_tot in merged:
                if bi == n_branch - 1:
                    scatter(o_ref, rows, o_tot / l_tot)
                else:
                    scatter(acc_o, rows, o_tot)
                    scatter(acc_m, rows, m_tot)
                    scatter(acc_l, rows, l_tot)
            return carry

        lax.fori_loop(0, d * nblk // ATTN_UNROLL[d], group, 0)


def _attention(q, k, v, bias):
    B, _, half, _ = q.shape
    T = 2 * half
    assert T % (ATTN_QB * DILATED_CONFIGS[-1][1]) == 0
    pair = pl.BlockSpec((None, 2, half, 128), lambda b, p: (b, 0, 0, p))
    nb = bias.shape[1]
    acc = pltpu.VMEM((2, half, 128), F32)
    padded = pltpu.VMEM((2, half + ATTN_PAD, 128), F32)
    return pl.pallas_call(
        functools.partial(_attn_kernel, T=T),
        grid=(B, N_HEADS // 2),
        in_specs=[pair, pair, pair, pl.BlockSpec((None, nb, 2 * ATTN_QB, ATTN_KW), lambda b, p: (p, 0, 0, 0))],
        out_specs=pair,
        out_shape=jax.ShapeDtypeStruct((B, 2, half, D_B), F32),
        scratch_shapes=[padded, padded, acc, acc, acc],
        compiler_params=_params(("parallel", "arbitrary"), 48),
        name="dilated_attn",
    )(q, k, v, bias)


def _ab_out_kernel(x_ref, ya_ref, yb_ref, mod_ref, wglu_ref, bglu_ref, wo_ref, o_ref):
    for r0 in range(0, x_ref.shape[0], ROW_BLOCK):
        rows = slice(r0, r0 + ROW_BLOCK)
        y = ya_ref[rows, :]
        y = 0.5 * y * (1.0 + jnp.tanh(math.sqrt(2.0 / math.pi) * (y + 0.044715 * (y * y * y))))
        y = y * _sigmoid(_dot(y.astype(BF16), wglu_ref[...]) + bglu_ref[...])
        half_rows = slice(r0 // 2, (r0 + ROW_BLOCK) // 2)
        yb = _interleave_tiles(yb_ref[0, half_rows, :], yb_ref[1, half_rows, :])
        out = _dot(y.astype(BF16), wo_ref[0:D_A, :]) + _dot(yb.astype(BF16), wo_ref[D_A:, :])
        o_ref[rows, :] = x_ref[rows, :] + mod_ref[2:3, :] * out


def _ab_out(x, ya, yb, mod, w_glu, b_glu, w_out, tm=512):
    B, T, D = x.shape
    half = pl.BlockSpec((None, tm, D_A), lambda b, i: (b, i, 0))
    split = pl.BlockSpec((None, 2, tm // 2, D_B), lambda b, i: (b, 0, i, 0))
    full = pl.BlockSpec((None, tm, D), lambda b, i: (b, i, 0))
    return pl.pallas_call(
        _ab_out_kernel,
        grid=(B, T // tm),
        in_specs=[full, half, split, pl.BlockSpec((None, 6, D), lambda b, i: (b, 0, 0)),
                  _const_spec((D_A, D_A)), _const_spec((1, D_A)), _const_spec((D, D))],
        out_specs=full,
        out_shape=jax.ShapeDtypeStruct((B, T, D), F32),
        compiler_params=_params(("parallel", "parallel"), 48),
        name="ab_out",
    )(x, ya, yb, mod, w_glu, b_glu, w_out)


def _halo_specs(tm, T, D):
    nh = T // HALO
    per = tm // HALO
    main = pl.BlockSpec((None, tm, D), lambda b, i: (b, i, 0))
    prev = pl.BlockSpec((None, HALO, D), lambda b, i: (b, jnp.maximum(i * per - 1, 0), 0))
    nxt = pl.BlockSpec((None, HALO, D), lambda b, i: (b, jnp.minimum((i + 1) * per, nh - 1), 0))
    return main, prev, nxt


def _ffn_kernel(x_ref, xp_ref, xn_ref, mod_ref, g_ref, wg_ref, wu_ref, cw_ref, cb_ref, wd_ref, o_ref, gbuf, abuf, *, tm):
    x = x_ref[...]
    g, shift, scale = g_ref[...], mod_ref[3:4, :], mod_ref[4:5, :]
    has_prev = jnp.where(pl.program_id(1) > 0, 1.0, 0.0)
    has_next = jnp.where(pl.program_id(1) < pl.num_programs(1) - 1, 1.0, 0.0)
    h = _norm_mod(x, g, shift, scale)
    he = jnp.concatenate([_norm_mod(xp_ref[...], g, shift, scale) * has_prev, h,
                          _norm_mod(xn_ref[...], g, shift, scale) * has_next], axis=0).astype(BF16)
    h = he[HALO:HALO + tm]
    for c in range(D_FF // FFN_CHUNK):
        cols = slice(c * FFN_CHUNK, (c + 1) * FFN_CHUNK)
        gbuf[:, cols] = _dot(he, wg_ref[:, cols])
        gate = (gbuf[pl.ds(HALO - 1, tm), cols] * cw_ref[0:1, cols] + gbuf[pl.ds(HALO, tm), cols] * cw_ref[1:2, cols]
                + gbuf[pl.ds(HALO + 1, tm), cols] * cw_ref[2:3, cols] + cb_ref[:, cols])
        up = _dot(h, wu_ref[:, cols])
        abuf[:, cols] = ((gate * _sigmoid(gate)) * up).astype(BF16)
    o_ref[...] = x + mod_ref[5:6, :] * _dot(abuf[...], wd_ref[...])


def _ffn(x, mod, g, w_gate, w_up, conv_w, conv_b, w_down, tm=512):
    B, T, D = x.shape
    main, prev, nxt = _halo_specs(tm, T, D)
    return pl.pallas_call(
        functools.partial(_ffn_kernel, tm=tm),
        grid=(B, T // tm),
        in_specs=[main, prev, nxt, pl.BlockSpec((None, 6, D), lambda b, i: (b, 0, 0)), _const_spec((1, D)),
                  _const_spec((D, D_FF)), _const_spec((D, D_FF)), _const_spec((3, D_FF)), _const_spec((1, D_FF)),
                  _const_spec((D_FF, D))],
        out_specs=main,
        out_shape=jax.ShapeDtypeStruct((B, T, D), F32),
        scratch_shapes=[pltpu.VMEM((tm + 2 * HALO, D_FF), F32), pltpu.VMEM((tm, D_FF), BF16)],
        compiler_params=_params(("parallel", "arbitrary"), 56),
        name="conv_ffn",
    )(x, x, x, mod, g, w_gate, w_up, conv_w, conv_b, w_down)


def _cd_in_kernel(x_ref, xp_ref, xn_ref, mod_ref, g_ref, w_ref, cbd_ref, sbd_ref, sw_ref, ab_ref, yd_ref, cbuf, *, tm):
    g, shift, scale = g_ref[...], mod_ref[0:1, :], mod_ref[1:2, :]
    has_prev = jnp.where(pl.program_id(1) > 0, 1.0, 0.0)
    has_next = jnp.where(pl.program_id(1) < pl.num_programs(1) - 1, 1.0, 0.0)
    he = jnp.concatenate([_norm_mod(xp_ref[...], g, shift, scale) * has_prev, _norm_mod(x_ref[...], g, shift, scale),
                          _norm_mod(xn_ref[...], g, shift, scale) * has_next], axis=0).astype(BF16)
    h = he[HALO:HALO + tm]
    for c0 in range(0, D_D, CD_CHUNK):
        cols = slice(c0, c0 + CD_CHUNK)
        hs = _dot(he, w_ref[:, D_C + c0:D_C + c0 + CD_CHUNK])
        gc = _dot(he, w_ref[:, D_C + 2 * D_D + c0:D_C + 2 * D_D + c0 + CD_CHUNK])
        cbuf[:, cols] = gc * hs
        conv = (cbuf[pl.ds(HALO - 1, tm), cols] * sw_ref[0:1, cols] + cbuf[pl.ds(HALO, tm), cols] * sw_ref[1:2, cols]
                + cbuf[pl.ds(HALO + 1, tm), cols] * sw_ref[2:3, cols])
        yd_ref[:, cols] = _dot(h, w_ref[:, D_C + D_D + c0:D_C + D_D + c0 + CD_CHUNK]) * conv
    uc = _dot(h, w_ref[:, 0:D_C]).astype(BF16)
    ab_ref[0] = _dot(uc, cbd_ref[...]).astype(BF16)
    ab_ref[1] = _dot(uc, sbd_ref[...]).astype(BF16)


def _cd_in(x, mod, g, w_in, cbd, sbd, sconv_w, tm=512):
    B, T, D = x.shape
    main, prev, nxt = _halo_specs(tm, T, D)
    return pl.pallas_call(
        functools.partial(_cd_in_kernel, tm=tm),
        grid=(B, T // tm),
        in_specs=[main, prev, nxt, pl.BlockSpec((None, 6, D), lambda b, i: (b, 0, 0)), _const_spec((1, D)),
                  _const_spec((D, w_in.shape[1])), _const_spec((D_C, D_C)), _const_spec((D_C, D_C)),
                  _const_spec((3, D_D))],
        out_specs=[pl.BlockSpec((None, 2, tm, D_C), lambda b, i: (b, 0, i, 0)),
                   pl.BlockSpec((None, tm, D_D), lambda b, i: (b, i, 0))],
        out_shape=[jax.ShapeDtypeStruct((B, 2, T, D_C), BF16), jax.ShapeDtypeStruct((B, T, D_D), F32)],
        scratch_shapes=[pltpu.VMEM((tm + 2 * HALO, D_D), F32)],
        compiler_params=_params(("parallel", "arbitrary"), 48),
        name="cd_in",
    )(x, x, x, mod, g, w_in, cbd, sbd, sconv_w)


def _cd_out_kernel(dft_ref, ab_ref, yd_ref, x_ref, mod_ref, wo_ref, o_ref, fold, *, T, tf):
    half = T // 2

    @pl.when(pl.program_id(1) == 0)
    def _():
        r = lax.broadcasted_iota(jnp.int32, (tf, tf + HALO), 0)
        c = lax.broadcasted_iota(jnp.int32, (tf, tf + HALO), 1)
        mirror = jnp.where(c == tf - r, 1.0, 0.0).astype(BF16)
        first_row = lax.broadcasted_iota(jnp.int32, (tf, D_C), 0) == 0
        for part, sign in ((0, 1.0), (1, -1.0)):
            for s0 in range(0, half, tf):
                nxt = ab_ref[part, T - s0:T - s0 + HALO, :] if s0 else jnp.zeros((HALO, D_C), BF16)
                src = jnp.concatenate([ab_ref[part, T - s0 - tf:T - s0, :], nxt], axis=0)
                folded = ab_ref[part, s0:s0 + tf, :].astype(F32) + sign * _dot(mirror, src)
                if part == 1 and s0 == 0:
                    folded = jnp.where(first_row, ab_ref[0, half:half + HALO, :].astype(F32)[0:1, :], folded)
                fold[part * half + s0:part * half + s0 + tf, :] = folded.astype(BF16)

    for r0 in range(0, x_ref.shape[0], ROW_BLOCK):
        rows = slice(r0, r0 + ROW_BLOCK)
        yc = _dot(dft_ref[rows, :], fold[...])
        out = _dot(yc.astype(BF16), wo_ref[0:D_C, :]) + _dot(yd_ref[rows, :].astype(BF16), wo_ref[D_C:, :])
        o_ref[rows, :] = x_ref[rows, :] + mod_ref[2:3, :] * out


def _cd_out(dft, ab, yd, x, mod, w_out, tm=512):
    B, T, D = x.shape
    return pl.pallas_call(
        functools.partial(_cd_out_kernel, T=T, tf=256),
        grid=(B, T // tm),
        in_specs=[pl.BlockSpec((tm, T), lambda b, i: (i, 0)),
                  pl.BlockSpec((None, 2, T, D_C), lambda b, i: (b, 0, 0, 0)),
                  pl.BlockSpec((None, tm, D_D), lambda b, i: (b, i, 0)),
                  pl.BlockSpec((None, tm, D), lambda b, i: (b, i, 0)),
                  pl.BlockSpec((None, 6, D), lambda b, i: (b, 0, 0)), _const_spec((D, D))],
        out_specs=pl.BlockSpec((None, tm, D), lambda b, i: (b, i, 0)),
        out_shape=jax.ShapeDtypeStruct((B, T, D), F32),
        scratch_shapes=[pltpu.VMEM((T, D_C), BF16)],
        compiler_params=_params(("parallel", "arbitrary"), 56),
        name="cd_out",
    )(dft, ab, yd, x, mod, w_out)


def _seq_dft_matrix(T):
    rows = 64
    s = jnp.arange(T // 2, dtype=jnp.int32)[None, :]
    angle = lambda t: ((t[:, None] * s) % T).astype(F32) * (2.0 * math.pi / T)
    ang_a = angle(jnp.arange(T // rows, dtype=jnp.int32) * rows)[:, None, :]
    ang_b = angle(jnp.arange(rows, dtype=jnp.int32))[None, :, :]
    scale = 1.0 / math.sqrt(T)
    cos_a, sin_a, cos_b, sin_b = jnp.cos(ang_a) * scale, jnp.sin(ang_a) * scale, jnp.cos(ang_b), jnp.sin(ang_b)
    cos_ts = (cos_a * cos_b - sin_a * sin_b).reshape(T, T // 2)
    sin_ts = (sin_a * cos_b + cos_a * sin_b).reshape(T, T // 2)
    nyquist = jnp.where(jnp.arange(T) % 2 == 0, scale, -scale)[:, None]
    return jnp.concatenate([cos_ts, jnp.where(s == 0, nyquist, -sin_ts)], axis=1).astype(BF16)


def _channel_dft_matrices():
    c = np.arange(D_C)
    ang = 2.0 * np.pi * ((c[:, None] % FNET_GROUP_DIM) * (c[None, :] % FNET_GROUP_DIM) % FNET_GROUP_DIM) / FNET_GROUP_DIM
    same = (c[:, None] // FNET_GROUP_DIM) == (c[None, :] // FNET_GROUP_DIM)
    scale = 1.0 / math.sqrt(FNET_GROUP_DIM)
    cbd = np.where(same, np.cos(ang) * scale, 0.0).astype(np.float32)
    sbd = np.where(same, np.sin(ang) * scale, 0.0).astype(np.float32)
    return jnp.asarray(cbd).astype(BF16), jnp.asarray(sbd).astype(BF16)


def _trunk(x, mods, wts):
    B, T, D = x.shape
    nch = T // S5_CHUNK
    mod = mods[0]
    u, q, k, v = _ab_in(x, mod, wts["norm_mix_g"][0], wts["ab_w_in"], wts["hsum"], wts["q_g"], wts["k_g"])
    ya = _s5(u, wts["s5_w1"], wts["s5_w2"], wts["s5_mult"])
    yb = _attention(q, k, v, wts["attn_bias"])
    x = _ab_out(x, ya, yb, mod, wts["s5_w_glu"], wts["s5_b_glu"], wts["ab_w_out"])
    x = _ffn(x, mod, wts["norm_ffn_g"][0], *wts["ffn"][0])
    mod = mods[1]
    ab, yd = _cd_in(x, mod, wts["norm_mix_g"][1], wts["cd_w_in"], wts["cbd"], wts["sbd"], wts["sconv_w"])
    x = _cd_out(_seq_dft_matrix(T), ab, yd, x, mod, wts["cd_w_out"])
    x = _ffn(x, mod, wts["norm_ffn_g"][1], *wts["ffn"][1])
    return x


def kernel(x_prompt, x_sample, c_prompt, c_sample, ada_w, ada_b, norm_mix_g, norm_ffn_g, ffn_w_gate, ffn_w_up, ffn_conv_w, ffn_conv_b, ffn_w_down, ab_w_in, ab_w_out, s5_lam_re_f, s5_lam_im_f, s5_log_dt_f, s5_lam_re_b, s5_lam_im_b, s5_log_dt_b, s5_b_re, s5_b_im, s5_c_re, s5_c_im, s5_d, s5_w_glu, s5_b_glu, q_norm_g, k_norm_g, cd_w_in, cd_w_out, sconv_w):
    depth = ada_w.shape[0]
    assert depth == 2 and ab_w_in.shape[0] == 1 and cd_w_in.shape[0] == 1
    bp = x_prompt.shape[0]
    mod_all = _ada(jnp.concatenate([c_prompt, c_sample], axis=0), ada_w, ada_b)
    s5_w1, s5_w2, s5_mult = _s5_tables(s5_lam_re_f[0], s5_lam_im_f[0], s5_log_dt_f[0], s5_lam_re_b[0],
                                       s5_lam_im_b[0], s5_log_dt_b[0], s5_b_re[0], s5_b_im[0],
                                       s5_c_re[0], s5_c_im[0], s5_d[0])
    head = np.arange(D_B) // HEAD_DIM
    cbd, sbd = _channel_dft_matrices()
    wts = dict(
        norm_mix_g=norm_mix_g.reshape(depth, 1, D_MODEL), norm_ffn_g=norm_ffn_g.reshape(depth, 1, D_MODEL),
        ab_w_in=ab_w_in[0].astype(BF16), ab_w_out=ab_w_out[0].astype(BF16),
        hsum=jnp.asarray(head[:, None] == head[None, :], BF16),
        q_g=jnp.tile(q_norm_g[0], N_HEADS).reshape(1, D_B), k_g=jnp.tile(k_norm_g[0], N_HEADS).reshape(1, D_B),
        s5_w1=s5_w1, s5_w2=s5_w2, s5_mult=s5_mult,
        s5_w_glu=s5_w_glu[0].astype(BF16), s5_b_glu=s5_b_glu[0].reshape(1, D_A),
        attn_bias=jnp.asarray(_attn_bias_table()),
        ffn=[(ffn_w_gate[l].astype(BF16), ffn_w_up[l].astype(BF16), ffn_conv_w[l], ffn_conv_b[l].reshape(1, D_FF),
              ffn_w_down[l].astype(BF16)) for l in range(depth)],
        cd_w_in=cd_w_in[0].astype(BF16), cd_w_out=cd_w_out[0].astype(BF16), cbd=cbd, sbd=sbd, sconv_w=sconv_w[0],
    )
    outs = []
    for x, rows in ((x_prompt, slice(0, bp)), (x_sample, slice(bp, None))):
        mods = [mod_all[l, rows].reshape(x.shape[0], 6, D_MODEL) for l in range(depth)]
        outs.append(_trunk(x, mods, wts))
    return tuple(outs)
```

```python
import functools
import math

import jax
import jax.numpy as jnp
import numpy as np
from jax import lax
from jax.experimental import pallas as pl
from jax.experimental.pallas import tpu as pltpu

F32 = jnp.float32
BF16 = jnp.bfloat16

D_MODEL = 1024
D_A = 512
S5_GROUP = 16
S5_GROUPS = 32
S5_STATE = 64
S5_CHUNK = 16
S5_LEVELS = 8
S5_LANE_GROUPS = 128 // S5_GROUP
D_B = 512
HEAD_DIM = 64
N_HEADS = 8
DILATED_CONFIGS = ((128, 1), (512, 4), (2048, 16))
ATTN_R = 64
ATTN_QB = 128
ATTN_KW = ATTN_QB + 2 * ATTN_R
ATTN_PAD = ATTN_R * 16
ATTN_UNROLL = {1: 16, 4: 16, 16: 16}
NEG_BIG = -1e30
LOG2_E = 1.4426950408889634
D_C = 512
FNET_GROUP_DIM = 128
D_D = 512
D_FF = 2816
FFN_CHUNK = 256
CD_CHUNK = 256
EPS = 1e-6
ROW_BLOCK = 256
TILE_ROWS = 8
HALO = 16
MIB = 2 ** 20


def _params(sem, vmem_mib):
    return pltpu.CompilerParams(dimension_semantics=sem, vmem_limit_bytes=vmem_mib * MIB)


def _const_spec(shape):
    nd = len(shape)
    return pl.BlockSpec(shape, lambda *_: (0,) * nd, pipeline_mode=pl.Buffered(1))


def _dot(a, b):
    return jnp.dot(a, b, preferred_element_type=F32)


def _norm_mod(x, g, shift, scale):
    ms = jnp.mean(x * x, axis=-1, keepdims=True)
    y = x * lax.rsqrt(ms + EPS) * g
    return y * (1.0 + scale) + shift


def _sigmoid(x):
    return 1.0 / (1.0 + jnp.exp(-x))


def _ada_kernel(c_ref, w_ref, b_ref, o_ref):
    c = c_ref[...]
    cond = (c * _sigmoid(c)).astype(BF16)
    o_ref[...] = _dot(cond, w_ref[...].astype(BF16)) + b_ref[...]


def _ada(c_all, ada_w, ada_b):
    depth, d, n = ada_w.shape
    rows = c_all.shape[0]
    tn = 1536
    return pl.pallas_call(
        _ada_kernel,
        grid=(depth, n // tn),
        in_specs=[pl.BlockSpec((rows, d), lambda l, j: (0, 0)),
                  pl.BlockSpec((None, d, tn), lambda l, j: (l, 0, j)),
                  pl.BlockSpec((None, 1, tn), lambda l, j: (l, 0, j))],
        out_specs=pl.BlockSpec((None, rows, tn), lambda l, j: (l, 0, j)),
        out_shape=jax.ShapeDtypeStruct((depth, rows, n), F32),
        compiler_params=_params(("parallel", "parallel"), 32),
        name="ada_mod",
    )(c_all, ada_w, ada_b.reshape(depth, 1, n))


def _ab_in_kernel(x_ref, mod_ref, g_ref, w_ref, hsum_ref, qg_ref, kg_ref, u_ref, q_ref, k_ref, v_ref):
    hsum = hsum_ref[...]

    def head_norm(a, g):
        ms = _dot((a * a).astype(BF16), hsum) * (1.0 / HEAD_DIM)
        return a * lax.rsqrt(ms + EPS) * g

    for r0 in range(0, x_ref.shape[0], ROW_BLOCK):
        rows = slice(r0, r0 + ROW_BLOCK)
        h = _norm_mod(x_ref[rows, :], g_ref[...], mod_ref[0:1, :], mod_ref[1:2, :])
        z = _dot(h.astype(BF16), w_ref[...])
        u_ref[rows, :] = z[:, 0:D_A]
        q = head_norm(z[:, D_A:D_A + D_B], qg_ref[...]) * (LOG2_E / math.sqrt(HEAD_DIM))
        k = head_norm(z[:, D_A + D_B:D_A + 2 * D_B], kg_ref[...])
        half_rows = slice(r0 // 2, (r0 + ROW_BLOCK) // 2)
        for ref, val in ((q_ref, q), (k_ref, k), (v_ref, z[:, D_A + 2 * D_B:])):
            for parity in range(2):
                ref[parity, half_rows, :] = _tiles_of_parity(val, parity)


def _tiles_of_parity(rows, parity):
    return jnp.concatenate([rows[t * TILE_ROWS:(t + 1) * TILE_ROWS]
                            for t in range(parity, rows.shape[0] // TILE_ROWS, 2)], axis=0)


def _interleave_tiles(even, odd):
    pieces = []
    for t in range(even.shape[0] // TILE_ROWS):
        pieces += [even[t * TILE_ROWS:(t + 1) * TILE_ROWS], odd[t * TILE_ROWS:(t + 1) * TILE_ROWS]]
    return jnp.concatenate(pieces, axis=0)


def _ab_in(x, mod, g, w_in, hsum, qg, kg, tm=512):
    B, T, D = x.shape
    n = w_in.shape[1]
    tok = pl.BlockSpec((None, tm, D_B), lambda b, i: (b, i, 0))
    split = pl.BlockSpec((None, 2, tm // 2, D_B), lambda b, i: (b, 0, i, 0))
    out = jax.ShapeDtypeStruct((B, T, D_B), F32)
    out_split = jax.ShapeDtypeStruct((B, 2, T // 2, D_B), F32)
    return pl.pallas_call(
        _ab_in_kernel,
        grid=(B, T // tm),
        in_specs=[pl.BlockSpec((None, tm, D), lambda b, i: (b, i, 0)),
                  pl.BlockSpec((None, 6, D), lambda b, i: (b, 0, 0)),
                  _const_spec((1, D)), _const_spec((D, n)), _const_spec((D_B, D_B)),
                  _const_spec((1, D_B)), _const_spec((1, D_B))],
        out_specs=[tok, split, split, split],
        out_shape=[out, out_split, out_split, out_split],
        compiler_params=_params(("parallel", "parallel"), 48),
        name="ab_in",
    )(x, mod, g, w_in, hsum, qg, kg)


def _cmul(a, b):
    return a[0] * b[0] - a[1] * b[1], a[0] * b[1] + a[1] * b[0]


def _s5_param_kernel(lam_ref, dt_ref, b_re_ref, b_im_ref, ct_re_ref, ct_im_ref, c_re_ref, c_im_ref, d_ref,
                     wf_re_ref, wf_im_ref, wb_re_ref, wb_im_ref, kf_ref, kb_ref,
                     caf_re_ref, caf_im_ref, cab_re_ref, cab_im_ref, ap_ref):
    P, W = S5_STATE, S5_CHUNK * S5_GROUP
    kidx = lax.broadcasted_iota(jnp.int32, (P, W), 1) // S5_GROUP
    lane_ap = lax.broadcasted_iota(jnp.int32, (P, 128), 1)
    ap = jnp.zeros((P, 128), F32)
    c_re, c_im = c_re_ref[...], c_im_ref[...]
    outs = ((wf_re_ref, wf_im_ref, kf_ref, caf_re_ref, caf_im_ref),
            (wb_re_ref, wb_im_ref, kb_ref, cab_re_ref, cab_im_ref))
    for direction in range(2):
        lam_re = lam_ref[:, 2 * direction:2 * direction + 1]
        lam_im = lam_ref[:, 2 * direction + 1:2 * direction + 2]
        dt = jnp.exp(dt_ref[:, direction:direction + 1])
        mag = jnp.exp(lam_re * dt)
        a_re = mag * jnp.cos(lam_im * dt)
        a_im = mag * jnp.sin(lam_im * dt)
        den = lam_re * lam_re + lam_im * lam_im
        coef_re = ((a_re - 1.0) * lam_re + a_im * lam_im) / den
        coef_im = (a_im * lam_re - (a_re - 1.0) * lam_im) / den
        bb = _cmul((coef_re, coef_im), (b_re_ref[...], b_im_ref[...]))
        a_pow = (a_re, a_im)
        pw = (jnp.ones((P, W), F32), jnp.zeros((P, W), F32))
        for j in range(4):
            bit = ((kidx >> j) & 1) == 1
            pw = _cmul(pw, (jnp.where(bit, a_pow[0], 1.0), jnp.where(bit, a_pow[1], 0.0)))
            a_pow = _cmul(a_pow, a_pow)
        for j in range(S5_LEVELS):
            base = 4 * j + 2 * direction
            ap = jnp.where(lane_ap == base, a_pow[0], ap)
            ap = jnp.where(lane_ap == base + 1, a_pow[1], ap)
            a_pow = _cmul(a_pow, a_pow)
        w_re, w_im = _cmul(pw, bb)
        ca_re, ca_im = _cmul(_cmul(pw, (a_re, a_im)), (ct_re_ref[...], ct_im_ref[...]))
        hp = lax.Precision.HIGHEST
        kmat = (jnp.dot(c_re, w_re, precision=hp, preferred_element_type=F32)
                - jnp.dot(c_im, w_im, precision=hp, preferred_element_type=F32))
        o_w_re, o_w_im, o_k, o_ca_re, o_ca_im = outs[direction]
        o_w_re[...] = w_re
        o_w_im[...] = w_im
        o_k[...] = kmat
        o_ca_re[...] = ca_re
        o_ca_im[...] = -ca_im
    row = lax.broadcasted_iota(jnp.int32, (S5_GROUP, W), 0)
    lane = lax.broadcasted_iota(jnp.int32, (S5_GROUP, W), 1)
    lag0 = kb_ref[...] + jnp.where(row == lane, d_ref[...], 0.0)
    kf_ref[...] = kf_ref[...] + jnp.where(lane < S5_GROUP, lag0, 0.0)
    ap_ref[...] = ap


def _s5_tables(lam_re_f, lam_im_f, log_dt_f, lam_re_b, lam_im_b, log_dt_b, b_re, b_im, c_re, c_im, d_skip):
    G, P, C, L, W = S5_GROUPS, S5_STATE, S5_GROUP, S5_CHUNK, S5_CHUNK * S5_GROUP
    lam = jnp.stack([lam_re_f, lam_im_f, lam_re_b, lam_im_b], axis=-1)
    dts = jnp.stack([log_dt_f, log_dt_b], axis=-1).reshape(G, 1, 2)
    tile_k = lambda a: jnp.tile(a, (1, 1, L))
    ct = lambda a: tile_k(jnp.swapaxes(a, 1, 2))
    grp = lambda r, c: pl.BlockSpec((None, r, c), lambda g: (g, 0, 0))
    pw_out = jax.ShapeDtypeStruct((G, P, W), F32)
    k_out = jax.ShapeDtypeStruct((G, C, W), F32)
    wf_re, wf_im, wb_re, wb_im, kf, kb, caf_re, caf_im, cab_re, cab_im, ap = pl.pallas_call(
        _s5_param_kernel,
        grid=(G,),
        in_specs=[grp(P, 4), grp(1, 2), grp(P, W), grp(P, W), grp(P, W), grp(P, W), grp(C, P), grp(C, P), grp(C, 1)],
        out_specs=[grp(P, W)] * 4 + [grp(C, W)] * 2 + [grp(P, W)] * 4 + [grp(P, 128)],
        out_shape=[pw_out] * 4 + [k_out] * 2 + [pw_out] * 4 + [jax.ShapeDtypeStruct((G, P, 128), F32)],
        compiler_params=_params(("parallel",), 32),
        name="s5_params",
    )(lam, dts, tile_k(b_re), tile_k(b_im), ct(c_re), ct(c_im), c_re, c_im, d_skip.reshape(G, C, 1))

    NG, J, hp = S5_LANE_GROUPS, G // S5_LANE_GROUPS, lax.Precision.HIGHEST
    order = _s5_token_order()
    place = (order[:, :, None] == np.arange(L)).astype(np.float32)
    place_rev = (order[:, :, None] == L - 1 - np.arange(L)).astype(np.float32)
    lag = order[:, None, :] - order[:, :, None] + (L - 1)
    lag_hot = (lag[..., None] == np.arange(2 * L - 1)).astype(np.float32)
    lag_tab = jnp.concatenate([kb.reshape(G, C, L, C)[:, :, :0:-1], kf.reshape(G, C, L, C)], axis=2)
    toep = jnp.einsum("xstl,jxoli->jxsito", lag_hot, lag_tab.reshape(J, NG, C, 2 * L - 1, C), precision=hp)
    toep = toep.reshape(G, W, W)

    def slots(a, hot):
        return jnp.einsum("xqk,jxpkc->jxpqc", hot, a.reshape(J, NG, P, L, C), precision=hp).reshape(G, P, W)

    t_ = lambda a: jnp.swapaxes(a, 1, 2)
    sf_re, sf_im = t_(slots(wf_re, place_rev)), t_(slots(wf_im, place_rev))
    sb_re, sb_im = t_(slots(wb_re, place)), t_(slots(wb_im, place))
    w1 = jnp.concatenate([toep, sf_re, sf_im, sf_im, sf_re, sb_re, sb_im, sb_im, sb_re], axis=2)

    def pair_rows(a):
        a = a.reshape(G // 2, 2, P, W)
        zero = jnp.zeros_like(a[:, 0])
        return jnp.concatenate([jnp.concatenate([a[:, 0], zero], axis=2),
                                jnp.concatenate([zero, a[:, 1]], axis=2)], axis=1)

    w2 = jnp.concatenate([pair_rows(slots(caf_re, place)), pair_rows(slots(caf_im, place)),
                          pair_rows(slots(cab_re, place_rev)), pair_rows(slots(cab_im, place_rev))], axis=1)
    apj = ap[:, :, :4 * S5_LEVELS].reshape(G // 2, 2, P, S5_LEVELS, 4).transpose(0, 3, 4, 1, 2)
    mult = apj.reshape(G // 2, S5_LEVELS, 4 * 2 * P)
    return w1.astype(BF16), w2.astype(BF16), mult


def _s5_token_order():
    NG = S5_LANE_GROUPS
    order = np.empty((NG, S5_CHUNK), np.int64)
    for gl in range(NG):
        for half in range(S5_CHUNK // NG):
            for blk in range(NG):
                order[gl, half * NG + blk] = half * NG + (blk - gl) % NG
    return order


def _s5_kernel(u_ref, w1_ref, w2_ref, mult_ref, y_ref, ucbuf, ybuf, *, nch, nlev):
    row = lax.broadcasted_iota(jnp.int32, (nch, 128), 0)
    lane_blk = lax.broadcasted_iota(jnp.int32, (nch, 128), 1) // S5_GROUP
    W, L, C, NG = S5_CHUNK * S5_GROUP, S5_CHUNK, S5_GROUP, S5_LANE_GROUPS

    def pick(slabs, shift):
        acc = slabs[(0 - shift) % NG]
        for blk in range(1, NG):
            acc = jnp.where(lane_blk == blk, slabs[(blk - shift) % NG], acc)
        return acc

    def shift_down(x, s):
        if s % 8 == 0:
            return jnp.concatenate([jnp.zeros((s, 128), F32), x[:nch - s]], axis=0)
        return jnp.where(row >= s, pltpu.roll(x, s, 0), 0.0)

    def shift_up(x, s):
        if s % 8 == 0:
            return jnp.concatenate([x[s:], jnp.zeros((s, 128), F32)], axis=0)
        return jnp.where(row < nch - s, pltpu.roll(x, nch - s, 0), 0.0)

    for half in range(L // NG):
        rolled = []
        for k in range(NG):
            slab = u_ref[pl.ds(half * NG + k, nch, stride=L), :]
            rolled.append(pltpu.roll(slab, k * C, 1) if k else slab)
        for gl in range(NG):
            ucbuf[gl, :, half * 128:(half + 1) * 128] = pick(rolled, gl).astype(BF16)

    first_group = lax.broadcasted_iota(jnp.int32, (nch, 128), 1) < S5_STATE

    for m in range(NG // 2):
        r0 = _dot(ucbuf[2 * m], w1_ref[2 * m])
        r1 = _dot(ucbuf[2 * m + 1], w1_ref[2 * m + 1])
        xf_re = jnp.where(first_group, r0[:, W:W + 128], r1[:, W + 128:W + 256])
        xf_im = jnp.where(first_group, r0[:, W + 128:W + 256], r1[:, W:W + 128])
        xb_re = jnp.where(first_group, r0[:, W + 256:W + 384], r1[:, W + 384:W + 512])
        xb_im = jnp.where(first_group, r0[:, W + 384:W + 512], r1[:, W + 256:W + 384])
        mult = mult_ref[m]
        for j in range(nlev):
            s = 1 << j
            a_re, a_im = mult[j:j + 1, 0:128], mult[j:j + 1, 128:256]
            s_re, s_im = shift_down(xf_re, s), shift_down(xf_im, s)
            xf_re, xf_im = xf_re + (a_re * s_re - a_im * s_im), xf_im + (a_re * s_im + a_im * s_re)
            a_re, a_im = mult[j:j + 1, 256:384], mult[j:j + 1, 384:512]
            s_re, s_im = shift_up(xb_re, s), shift_up(xb_im, s)
            xb_re, xb_im = xb_re + (a_re * s_re - a_im * s_im), xb_im + (a_re * s_im + a_im * s_re)
        xs = jnp.concatenate([shift_down(xf_re, 1), shift_down(xf_im, 1), shift_up(xb_re, 1), shift_up(xb_im, 1)],
                             axis=1).astype(BF16)
        y = _dot(xs, w2_ref[m])
        ybuf[2 * m] = r0[:, 0:W] + y[:, 0:W]
        ybuf[2 * m + 1] = r1[:, 0:W] + y[:, W:]

    for half in range(L // NG):
        for k in range(NG):
            slab = pick([ybuf[gl, :, half * 128:(half + 1) * 128] for gl in range(NG)], k)
            y_ref[pl.ds(half * NG + k, nch, stride=L), :] = pltpu.roll(slab, (NG - k) * C, 1) if k else slab


def _s5(u, w1, w2, mult):
    B, T, D = u.shape
    nch, W, NG = T // S5_CHUNK, S5_CHUNK * S5_GROUP, S5_LANE_GROUPS
    nlev = int(math.log2(nch))
    assert 1 << nlev == nch and nlev <= S5_LEVELS
    tok = pl.BlockSpec((None, T, 128), lambda b, j: (b, 0, j))
    return pl.pallas_call(
        functools.partial(_s5_kernel, nch=nch, nlev=nlev),
        grid=(B, D // 128),
        in_specs=[tok,
                  pl.BlockSpec((NG, W, 3 * W), lambda b, j: (j, 0, 0)),
                  pl.BlockSpec((NG // 2, 2 * W, 2 * W), lambda b, j: (j, 0, 0)),
                  pl.BlockSpec((NG // 2, S5_LEVELS, 2 * W), lambda b, j: (j, 0, 0))],
        out_specs=tok,
        out_shape=jax.ShapeDtypeStruct((B, T, D), F32),
        scratch_shapes=[pltpu.VMEM((NG, nch, W), BF16), pltpu.VMEM((NG, nch, W), F32)],
        compiler_params=_params(("parallel", "parallel"), 32),
        name="s5_scan",
    )(u, w1, w2, mult)


def _attn_block_positions(d):
    a, c = np.arange(ATTN_QB), np.arange(ATTN_KW)
    if d == 1:
        token = lambda row, slab: 16 * ((row % slab) // TILE_ROWS) + TILE_ROWS * (row // slab) + row % TILE_ROWS
        return token(a, ATTN_QB // 2), token(c, ATTN_KW // 2) - ATTN_R
    if d == 4:
        pos = lambda row, slab: 4 * ((row % slab) // 2) + 2 * (row // slab) + row % 2
        return pos(a, ATTN_QB // 2), pos(c, ATTN_KW // 2) - ATTN_R
    return a, c - ATTN_R


def _attn_bias_table():
    slopes = 2.0 ** (-8.0 * np.arange(1, N_HEADS + 1, dtype=np.float64) / N_HEADS)
    tab = np.empty((N_HEADS // 2, len(DILATED_CONFIGS), 4, 2, ATTN_QB, ATTN_KW), np.float32)
    for bi, (_, d) in enumerate(DILATED_CONFIGS):
        pos_q, pos_k = _attn_block_positions(d)
        rel = pos_k[None, :] - pos_q[:, None]
        for ty in range(4):
            valid = np.abs(rel) <= ATTN_R
            if ty & 1:
                valid = valid & (pos_k[None, :] >= 0)
            if ty & 2:
                valid = valid & (pos_k[None, :] < ATTN_QB)
            for h in range(N_HEADS):
                dist = (np.abs(rel) * d).astype(np.float32)
                alibi = (-np.float32(slopes[h] * LOG2_E)) * dist
                bias = np.where(valid, alibi, np.float32(NEG_BIG))
                if ty == 3:
                    bias = np.concatenate([bias[:, ATTN_R:ATTN_R + ATTN_QB],
                                           np.full((ATTN_QB, 2 * ATTN_R), NEG_BIG, np.float32)], axis=1)
                tab[h // 2, bi, ty, h % 2] = bias
    return tab.reshape(N_HEADS // 2, len(DILATED_CONFIGS) * 4, 2 * ATTN_QB, ATTN_KW)


def _attn_kernel(q_ref, k_ref, v_ref, bias_ref, o_ref, kp, vp, acc_o, acc_m, acc_l, *, T):
    half, pad = T // 2, ATTN_PAD // 2
    zeros = jnp.zeros((pad, 128), F32)
    for buf, src in ((kp, k_ref), (vp, v_ref)):
        for parity in range(2):
            buf[parity, 0:pad, :] = zeros
            buf[parity, pad + half:pad + half + pad, :] = zeros
            buf[parity, pad:pad + half, :] = src[parity]
    first_head = lax.broadcasted_iota(jnp.int32, (ATTN_QB, 128), 1) < HEAD_DIM
    ones = jnp.ones((ATTN_KW, 128), BF16)
    n_branch = len(DILATED_CONFIGS)

    def block_slabs(d, nblk, it, u):
        idx = it * ATTN_UNROLL[d] + u
        tile = lambda n: pl.multiple_of(TILE_ROWS * n, TILE_ROWS)
        if d == 1:
            blk, nq, nr = idx, ATTN_QB // 2, ATTN_R // 2
            q = [(p, pl.ds(tile(nq // TILE_ROWS * blk), nq), nq) for p in range(2)]
            k = [(p, pl.ds(pad - nr + tile(nq // TILE_ROWS * blk), 2 * nq), 2 * nq) for p in range(2)]
        elif d == 4:
            r, blk, nq, nr = u % d, idx // d, ATTN_QB // 2, ATTN_R // 2
            q = [(p, pl.ds(tile(nq // 2 * blk) + r, nq, stride=4), nq) for p in range(2)]
            k = [(p, pl.ds(pad - 4 * nr + tile(nq // 2 * blk) + r, 2 * nq, stride=4), 2 * nq) for p in range(2)]
        else:
            assert d == 2 * TILE_ROWS and ATTN_UNROLL[d] % TILE_ROWS == 0
            sub, parity, blk = u % TILE_ROWS, (idx // TILE_ROWS) % 2, idx // d
            q = [(parity, pl.ds(tile(ATTN_QB * blk) + sub, ATTN_QB, stride=TILE_ROWS), ATTN_QB)]
            back, nk = (0, ATTN_QB) if nblk == 1 else (TILE_ROWS * ATTN_R, ATTN_KW)
            k = [(parity, pl.ds(pad - back + tile(ATTN_QB * blk) + sub, nk, stride=TILE_ROWS), nk)]
        return q, k, blk

    def gather(ref, slabs):
        return jnp.concatenate([ref[p, rows, :] for p, rows, _ in slabs], axis=0)

    def scatter(ref, slabs, val):
        start = 0
        for p, rows, n in slabs:
            ref[p, rows, :] = val[start:start + n]
            start += n

    def block_softmax(bi, d, nblk, it, u):
        rows, keys, blk = block_slabs(d, nblk, it, u)
        qf = gather(q_ref, rows)
        q2 = jnp.concatenate([jnp.where(first_head, qf, 0.0), jnp.where(first_head, 0.0, qf)], axis=0).astype(BF16)
        if nblk == 1:
            bias = bias_ref[bi * 4 + 3, :, 0:ATTN_QB]
        else:
            bias = bias_ref[bi * 4 + jnp.where(blk == 0, 1, 0) + jnp.where(blk == nblk - 1, 2, 0)]
        kw = gather(kp, keys).astype(BF16)
        s = lax.dot_general(q2, kw, (((1,), (1,)), ((), ())), preferred_element_type=F32)
        s = s + bias
        m = jnp.max(s, axis=-1, keepdims=True)
        p = jnp.exp2(s - m).astype(BF16)
        ov = _dot(p, jnp.concatenate([gather(vp, keys).astype(BF16), ones[0:kw.shape[0]]], axis=1))
        o_new = jnp.where(first_head, ov[0:ATTN_QB, 0:128], ov[ATTN_QB:, 0:128])
        l_new = jnp.where(first_head, ov[0:ATTN_QB, 128:], ov[ATTN_QB:, 128:])
        m_new = jnp.where(first_head, m[0:ATTN_QB], m[ATTN_QB:])
        return rows, o_new, m_new, l_new

    for bi, (_, d) in enumerate(DILATED_CONFIGS):
        nblk = T // d // ATTN_QB

        def group(it, carry, bi=bi, d=d, nblk=nblk):
            new = [block_softmax(bi, d, nblk, it, u) for u in range(ATTN_UNROLL[d])]
            if bi == 0:
                for rows, o_new, m_new, l_new in new:
                    scatter(acc_o, rows, o_new)
                    scatter(acc_m, rows, m_new)
                    scatter(acc_l, rows, l_new)
                return carry
            old = [(gather(acc_o, rows), gather(acc_m, rows), gather(acc_l, rows)) for rows, _, _, _ in new]
            merged = []
            for (rows, o_new, m_new, l_new), (o_old, m_old, l_old) in zip(new, old):
                e = jnp.exp2(-jnp.abs(m_old - m_new))
                keep = m_old >= m_new
                a = jnp.where(keep, 1.0, e)
                b = jnp.where(keep, e, 1.0)
                merged.append((rows, a * o_old + b * o_new, jnp.maximum(m_old, m_new), a * l_old + b * l_new))
            for rows, o_tot, m_tot, l_tot in merged:
                if bi == n_branch - 1:
                    scatter(o_ref, rows, o_tot / l_tot)
                else:
                    scatter(acc_o, rows, o_tot)
                    scatter(acc_m, rows, m_tot)
                    scatter(acc_l, rows, l_tot)
            return carry

        lax.fori_loop(0, d * nblk // ATTN_UNROLL[d], group, 0)


def _attention(q, k, v, bias):
    B, _, half, _ = q.shape
    T = 2 * half
    assert T % (ATTN_QB * DILATED_CONFIGS[-1][1]) == 0
    pair = pl.BlockSpec((None, 2, half, 128), lambda b, p: (b, 0, 0, p))
    nb = bias.shape[1]
    acc = pltpu.VMEM((2, half, 128), F32)
    padded = pltpu.VMEM((2, half + ATTN_PAD, 128), F32)
    return pl.pallas_call(
        functools.partial(_attn_kernel, T=T),
        grid=(B, N_HEADS // 2),
        in_specs=[pair, pair, pair, pl.BlockSpec((None, nb, 2 * ATTN_QB, ATTN_KW), lambda b, p: (p, 0, 0, 0))],
        out_specs=pair,
        out_shape=jax.ShapeDtypeStruct((B, 2, half, D_B), F32),
        scratch_shapes=[padded, padded, acc, acc, acc],
        compiler_params=_params(("parallel", "arbitrary"), 48),
        name="dilated_attn",
    )(q, k, v, bias)


def _ab_out_kernel(x_ref, ya_ref, yb_ref, mod_ref, wglu_ref, bglu_ref, wo_ref, o_ref):
    for r0 in range(0, x_ref.shape[0], ROW_BLOCK):
        rows = slice(r0, r0 + ROW_BLOCK)
        y = ya_ref[rows, :]
        y = 0.5 * y * (1.0 + jnp.tanh(math.sqrt(2.0 / math.pi) * (y + 0.044715 * (y * y * y))))
        y = y * _sigmoid(_dot(y.astype(BF16), wglu_ref[...]) + bglu_ref[...])
        half_rows = slice(r0 // 2, (r0 + ROW_BLOCK) // 2)
        yb = _interleave_tiles(yb_ref[0, half_rows, :], yb_ref[1, half_rows, :])
        out = _dot(y.astype(BF16), wo_ref[0:D_A, :]) + _dot(yb.astype(BF16), wo_ref[D_A:, :])
        o_ref[rows, :] = x_ref[rows, :] + mod_ref[2:3, :] * out


def _ab_out(x, ya, yb, mod, w_glu, b_glu, w_out, tm=512):
    B, T, D = x.shape
    half = pl.BlockSpec((None, tm, D_A), lambda b, i: (b, i, 0))
    split = pl.BlockSpec((None, 2, tm // 2, D_B), lambda b, i: (b, 0, i, 0))
    full = pl.BlockSpec((None, tm, D), lambda b, i: (b, i, 0))
    return pl.pallas_call(
        _ab_out_kernel,
        grid=(B, T // tm),
        in_specs=[full, half, split, pl.BlockSpec((None, 6, D), lambda b, i: (b, 0, 0)),
                  _const_spec((D_A, D_A)), _const_spec((1, D_A)), _const_spec((D, D))],
        out_specs=full,
        out_shape=jax.ShapeDtypeStruct((B, T, D), F32),
        compiler_params=_params(("parallel", "parallel"), 48),
        name="ab_out",
    )(x, ya, yb, mod, w_glu, b_glu, w_out)


def _halo_specs(tm, T, D):
    nh = T // HALO
    per = tm // HALO
    main = pl.BlockSpec((None, tm, D), lambda b, i: (b, i, 0))
    prev = pl.BlockSpec((None, HALO, D), lambda b, i: (b, jnp.maximum(i * per - 1, 0), 0))
    nxt = pl.BlockSpec((None, HALO, D), lambda b, i: (b, jnp.minimum((i + 1) * per, nh - 1), 0))
    return main, prev, nxt


def _ffn_kernel(x_ref, xp_ref, xn_ref, mod_ref, g_ref, wg_ref, wu_ref, cw_ref, cb_ref, wd_ref, o_ref, gbuf, abuf, *, tm):
    x = x_ref[...]
    g, shift, scale = g_ref[...], mod_ref[3:4, :], mod_ref[4:5, :]
    has_prev = jnp.where(pl.program_id(1) > 0, 1.0, 0.0)
    has_next = jnp.where(pl.program_id(1) < pl.num_programs(1) - 1, 1.0, 0.0)
    h = _norm_mod(x, g, shift, scale)
    he = jnp.concatenate([_norm_mod(xp_ref[...], g, shift, scale) * has_prev, h,
                          _norm_mod(xn_ref[...], g, shift, scale) * has_next], axis=0).astype(BF16)
    h = he[HALO:HALO + tm]
    for c in range(D_FF // FFN_CHUNK):
        cols = slice(c * FFN_CHUNK, (c + 1) * FFN_CHUNK)
        gbuf[:, cols] = _dot(he, wg_ref[:, cols])
        gate = (gbuf[pl.ds(HALO - 1, tm), cols] * cw_ref[0:1, cols] + gbuf[pl.ds(HALO, tm), cols] * cw_ref[1:2, cols]
                + gbuf[pl.ds(HALO + 1, tm), cols] * cw_ref[2:3, cols] + cb_ref[:, cols])
        up = _dot(h, wu_ref[:, cols])
        abuf[:, cols] = ((gate * _sigmoid(gate)) * up).astype(BF16)
    o_ref[...] = x + mod_ref[5:6, :] * _dot(abuf[...], wd_ref[...])


def _ffn(x, mod, g, w_gate, w_up, conv_w, conv_b, w_down, tm=512):
    B, T, D = x.shape
    main, prev, nxt = _halo_specs(tm, T, D)
    return pl.pallas_call(
        functools.partial(_ffn_kernel, tm=tm),
        grid=(B, T // tm),
        in_specs=[main, prev, nxt, pl.BlockSpec((None, 6, D), lambda b, i: (b, 0, 0)), _const_spec((1, D)),
                  _const_spec((D, D_FF)), _const_spec((D, D_FF)), _const_spec((3, D_FF)), _const_spec((1, D_FF)),
                  _const_spec((D_FF, D))],
        out_specs=main,
        out_shape=jax.ShapeDtypeStruct((B, T, D), F32),
        scratch_shapes=[pltpu.VMEM((tm + 2 * HALO, D_FF), F32), pltpu.VMEM((tm, D_FF), BF16)],
        compiler_params=_params(("parallel", "arbitrary"), 56),
        name="conv_ffn",
    )(x, x, x, mod, g, w_gate, w_up, conv_w, conv_b, w_down)


def _cd_in_kernel(x_ref, xp_ref, xn_ref, mod_ref, g_ref, w_ref, cbd_ref, sbd_ref, sw_ref, ab_ref, yd_ref, cbuf, *, tm):
    g, shift, scale = g_ref[...], mod_ref[0:1, :], mod_ref[1:2, :]
    has_prev = jnp.where(pl.program_id(1) > 0, 1.0, 0.0)
    has_next = jnp.where(pl.program_id(1) < pl.num_programs(1) - 1, 1.0, 0.0)
    he = jnp.concatenate([_norm_mod(xp_ref[...], g, shift, scale) * has_prev, _norm_mod(x_ref[...], g, shift, scale),
                          _norm_mod(xn_ref[...], g, shift, scale) * has_next], axis=0).astype(BF16)
    h = he[HALO:HALO + tm]
    for c0 in range(0, D_D, CD_CHUNK):
        cols = slice(c0, c0 + CD_CHUNK)
        hs = _dot(he, w_ref[:, D_C + c0:D_C + c0 + CD_CHUNK])
        gc = _dot(he, w_ref[:, D_C + 2 * D_D + c0:D_C + 2 * D_D + c0 + CD_CHUNK])
        cbuf[:, cols] = gc * hs
        conv = (cbuf[pl.ds(HALO - 1, tm), cols] * sw_ref[0:1, cols] + cbuf[pl.ds(HALO, tm), cols] * sw_ref[1:2, cols]
                + cbuf[pl.ds(HALO + 1, tm), cols] * sw_ref[2:3, cols])
        yd_ref[:, cols] = _dot(h, w_ref[:, D_C + D_D + c0:D_C + D_D + c0 + CD_CHUNK]) * conv
    uc = _dot(h, w_ref[:, 0:D_C]).astype(BF16)
    ab_ref[0] = _dot(uc, cbd_ref[...]).astype(BF16)
    ab_ref[1] = _dot(uc, sbd_ref[...]).astype(BF16)


def _cd_in(x, mod, g, w_in, cbd, sbd, sconv_w, tm=512):
    B, T, D = x.shape
    main, prev, nxt = _halo_specs(tm, T, D)
    return pl.pallas_call(
        functools.partial(_cd_in_kernel, tm=tm),
        grid=(B, T // tm),
        in_specs=[main, prev, nxt, pl.BlockSpec((None, 6, D), lambda b, i: (b, 0, 0)), _const_spec((1, D)),
                  _const_spec((D, w_in.shape[1])), _const_spec((D_C, D_C)), _const_spec((D_C, D_C)),
                  _const_spec((3, D_D))],
        out_specs=[pl.BlockSpec((None, 2, tm, D_C), lambda b, i: (b, 0, i, 0)),
                   pl.BlockSpec((None, tm, D_D), lambda b, i: (b, i, 0))],
        out_shape=[jax.ShapeDtypeStruct((B, 2, T, D_C), BF16), jax.ShapeDtypeStruct((B, T, D_D), F32)],
        scratch_shapes=[pltpu.VMEM((tm + 2 * HALO, D_D), F32)],
        compiler_params=_params(("parallel", "arbitrary"), 48),
        name="cd_in",
    )(x, x, x, mod, g, w_in, cbd, sbd, sconv_w)


def _cd_out_kernel(dft_ref, ab_ref, yd_ref, x_ref, mod_ref, wo_ref, o_ref, fold, *, T, tf):
    half = T // 2

    @pl.when(pl.program_id(1) == 0)
    def _():
        r = lax.broadcasted_iota(jnp.int32, (tf, tf + HALO), 0)
        c = lax.broadcasted_iota(jnp.int32, (tf, tf + HALO), 1)
        mirror = jnp.where(c == tf - r, 1.0, 0.0).astype(BF16)
        first_row = lax.broadcasted_iota(jnp.int32, (tf, D_C), 0) == 0
        for part, sign in ((0, 1.0), (1, -1.0)):
            for s0 in range(0, half, tf):
                nxt = ab_ref[part, T - s0:T - s0 + HALO, :] if s0 else jnp.zeros((HALO, D_C), BF16)
                src = jnp.concatenate([ab_ref[part, T - s0 - tf:T - s0, :], nxt], axis=0)
                folded = ab_ref[part, s0:s0 + tf, :].astype(F32) + sign * _dot(mirror, src)
                if part == 1 and s0 == 0:
                    folded = jnp.where(first_row, ab_ref[0, half:half + HALO, :].astype(F32)[0:1, :], folded)
                fold[part * half + s0:part * half + s0 + tf, :] = folded.astype(BF16)

    for r0 in range(0, x_ref.shape[0], ROW_BLOCK):
        rows = slice(r0, r0 + ROW_BLOCK)
        yc = _dot(dft_ref[rows, :], fold[...])
        out = _dot(yc.astype(BF16), wo_ref[0:D_C, :]) + _dot(yd_ref[rows, :].astype(BF16), wo_ref[D_C:, :])
        o_ref[rows, :] = x_ref[rows, :] + mod_ref[2:3, :] * out


def _cd_out(dft, ab, yd, x, mod, w_out, tm=512):
    B, T, D = x.shape
    return pl.pallas_call(
        functools.partial(_cd_out_kernel, T=T, tf=256),
        grid=(B, T // tm),
        in_specs=[pl.BlockSpec((tm, T), lambda b, i: (i, 0)),
                  pl.BlockSpec((None, 2, T, D_C), lambda b, i: (b, 0, 0, 0)),
                  pl.BlockSpec((None, tm, D_D), lambda b, i: (b, i, 0)),
                  pl.BlockSpec((None, tm, D), lambda b, i: (b, i, 0)),
                  pl.BlockSpec((None, 6, D), lambda b, i: (b, 0, 0)), _const_spec((D, D))],
        out_specs=pl.BlockSpec((None, tm, D), lambda b, i: (b, i, 0)),
        out_shape=jax.ShapeDtypeStruct((B, T, D), F32),
        scratch_shapes=[pltpu.VMEM((T, D_C), BF16)],
        compiler_params=_params(("parallel", "arbitrary"), 56),
        name="cd_out",
    )(dft, ab, yd, x, mod, w_out)


def _seq_dft_matrix(T):
    rows = 64
    s = jnp.arange(T // 2, dtype=jnp.int32)[None, :]
    angle = lambda t: ((t[:, None] * s) % T).astype(F32) * (2.0 * math.pi / T)
    ang_a = angle(jnp.arange(T // rows, dtype=jnp.int32) * rows)[:, None, :]
    ang_b = angle(jnp.arange(rows, dtype=jnp.int32))[None, :, :]
    scale = 1.0 / math.sqrt(T)
    cos_a, sin_a, cos_b, sin_b = jnp.cos(ang_a) * scale, jnp.sin(ang_a) * scale, jnp.cos(ang_b), jnp.sin(ang_b)
    cos_ts = (cos_a * cos_b - sin_a * sin_b).reshape(T, T // 2)
    sin_ts = (sin_a * cos_b + cos_a * sin_b).reshape(T, T // 2)
    nyquist = jnp.where(jnp.arange(T) % 2 == 0, scale, -scale)[:, None]
    return jnp.concatenate([cos_ts, jnp.where(s == 0, nyquist, -sin_ts)], axis=1).astype(BF16)


def _channel_dft_matrices():
    c = np.arange(D_C)
    ang = 2.0 * np.pi * ((c[:, None] % FNET_GROUP_DIM) * (c[None, :] % FNET_GROUP_DIM) % FNET_GROUP_DIM) / FNET_GROUP_DIM
    same = (c[:, None] // FNET_GROUP_DIM) == (c[None, :] // FNET_GROUP_DIM)
    scale = 1.0 / math.sqrt(FNET_GROUP_DIM)
    cbd = np.where(same, np.cos(ang) * scale, 0.0).astype(np.float32)
    sbd = np.where(same, np.sin(ang) * scale, 0.0).astype(np.float32)
    return jnp.asarray(cbd).astype(BF16), jnp.asarray(sbd).astype(BF16)


def _trunk(x, mods, wts):
    B, T, D = x.shape
    nch = T // S5_CHUNK
    mod = mods[0]
    u, q, k, v = _ab_in(x, mod, wts["norm_mix_g"][0], wts["ab_w_in"], wts["hsum"], wts["q_g"], wts["k_g"])
    ya = _s5(u, wts["s5_w1"], wts["s5_w2"], wts["s5_mult"])
    yb = _attention(q, k, v, wts["attn_bias"])
    x = _ab_out(x, ya, yb, mod, wts["s5_w_glu"], wts["s5_b_glu"], wts["ab_w_out"])
    x = _ffn(x, mod, wts["norm_ffn_g"][0], *wts["ffn"][0])
    mod = mods[1]
    ab, yd = _cd_in(x, mod, wts["norm_mix_g"][1], wts["cd_w_in"], wts["cbd"], wts["sbd"], wts["sconv_w"])
    x = _cd_out(_seq_dft_matrix(T), ab, yd, x, mod, wts["cd_w_out"])
    x = _ffn(x, mod, wts["norm_ffn_g"][1], *wts["ffn"][1])
    return x


def kernel(x_prompt, x_sample, c_prompt, c_sample, ada_w, ada_b, norm_mix_g, norm_ffn_g, ffn_w_gate, ffn_w_up, ffn_conv_w, ffn_conv_b, ffn_w_down, ab_w_in, ab_w_out, s5_lam_re_f, s5_lam_im_f, s5_log_dt_f, s5_lam_re_b, s5_lam_im_b, s5_log_dt_b, s5_b_re, s5_b_im, s5_c_re, s5_c_im, s5_d, s5_w_glu, s5_b_glu, q_norm_g, k_norm_g, cd_w_in, cd_w_out, sconv_w):
    depth = ada_w.shape[0]
    assert depth == 2 and ab_w_in.shape[0] == 1 and cd_w_in.shape[0] == 1
    bp = x_prompt.shape[0]
    mod_all = _ada(jnp.concatenate([c_prompt, c_sample], axis=0), ada_w, ada_b)
    s5_w1, s5_w2, s5_mult = _s5_tables(s5_lam_re_f[0], s5_lam_im_f[0], s5_log_dt_f[0], s5_lam_re_b[0],
                                       s5_lam_im_b[0], s5_log_dt_b[0], s5_b_re[0], s5_b_im[0],
                                       s5_c_re[0], s5_c_im[0], s5_d[0])
    head = np.arange(D_B) // HEAD_DIM
    cbd, sbd = _channel_dft_matrices()
    wts = dict(
        norm_mix_g=norm_mix_g.reshape(depth, 1, D_MODEL), norm_ffn_g=norm_ffn_g.reshape(depth, 1, D_MODEL),
        ab_w_in=ab_w_in[0].astype(BF16), ab_w_out=ab_w_out[0].astype(BF16),
        hsum=jnp.asarray(head[:, None] == head[None, :], BF16),
        q_g=jnp.tile(q_norm_g[0], N_HEADS).reshape(1, D_B), k_g=jnp.tile(k_norm_g[0], N_HEADS).reshape(1, D_B),
        s5_w1=s5_w1, s5_w2=s5_w2, s5_mult=s5_mult,
        s5_w_glu=s5_w_glu[0].astype(BF16), s5_b_glu=s5_b_glu[0].reshape(1, D_A),
        attn_bias=jnp.asarray(_attn_bias_table()),
        ffn=[(ffn_w_gate[l].astype(BF16), ffn_w_up[l].astype(BF16), ffn_conv_w[l], ffn_conv_b[l].reshape(1, D_FF),
              ffn_w_down[l].astype(BF16)) for l in range(depth)],
        cd_w_in=cd_w_in[0].astype(BF16), cd_w_out=cd_w_out[0].astype(BF16), cbd=cbd, sbd=sbd, sconv_w=sconv_w[0],
    )
    outs = []
    for x, rows in ((x_prompt, slice(0, bp)), (x_sample, slice(bp, None))):
        mods = [mod_all[l, rows].reshape(x.shape[0], 6, D_MODEL) for l in range(depth)]
        outs.append(_trunk(x, mods, wts))
    return tuple(outs)
```

```python
import functools
import math

import jax
import jax.numpy as jnp
import numpy as np
from jax import lax
from jax.experimental import pallas as pl
from jax.experimental.pallas import tpu as pltpu

F32 = jnp.float32
BF16 = jnp.bfloat16

D_MODEL = 1024
D_A = 512
S5_GROUP = 16
S5_GROUPS = 32
S5_STATE = 64
S5_CHUNK = 16
S5_LEVELS = 8
S5_LANE_GROUPS = 128 // S5_GROUP
D_B = 512
HEAD_DIM = 64
N_HEADS = 8
DILATED_CONFIGS = ((128, 1), (512, 4), (2048, 16))
ATTN_R = 64
ATTN_QB = 128
ATTN_KW = ATTN_QB + 2 * ATTN_R
ATTN_PAD = ATTN_R * 16
ATTN_UNROLL = {1: 16, 4: 16, 16: 16}
NEG_BIG = -1e30
LOG2_E = 1.4426950408889634
D_C = 512
FNET_GROUP_DIM = 128
D_D = 512
D_FF = 2816
FFN_CHUNK = 256
CD_CHUNK = 256
EPS = 1e-6
ROW_BLOCK = 256
TILE_ROWS = 8
MXU_TILE = 256
HALO = 16
MIB = 2 ** 20


def _params(sem, vmem_mib):
    return pltpu.CompilerParams(dimension_semantics=sem, vmem_limit_bytes=vmem_mib * MIB)


def _const_spec(shape):
    nd = len(shape)
    return pl.BlockSpec(shape, lambda *_: (0,) * nd, pipeline_mode=pl.Buffered(1))


def _dot(a, b):
    return jnp.dot(a, b, preferred_element_type=F32)


def _block_diag_dot(a, m_ref):
    n = m_ref.shape[0]
    return jnp.concatenate([_dot(a[:, c:c + MXU_TILE], m_ref[c:c + MXU_TILE, c:c + MXU_TILE])
                            for c in range(0, n, MXU_TILE)], axis=1)


def _norm_mod(x, g, shift, scale):
    ms = jnp.mean(x * x, axis=-1, keepdims=True)
    y = x * lax.rsqrt(ms + EPS) * g
    return y * (1.0 + scale) + shift


def _sigmoid(x):
    return 1.0 / (1.0 + jnp.exp(-x))


def _ada_kernel(c_ref, w_ref, b_ref, o_ref):
    c = c_ref[...]
    cond = (c * _sigmoid(c)).astype(BF16)
    o_ref[...] = _dot(cond, w_ref[...].astype(BF16)) + b_ref[...]


def _ada(c_all, ada_w, ada_b):
    depth, d, n = ada_w.shape
    rows = c_all.shape[0]
    tn = 1536
    return pl.pallas_call(
        _ada_kernel,
        grid=(depth, n // tn),
        in_specs=[pl.BlockSpec((rows, d), lambda l, j: (0, 0)),
                  pl.BlockSpec((None, d, tn), lambda l, j: (l, 0, j)),
                  pl.BlockSpec((None, 1, tn), lambda l, j: (l, 0, j))],
        out_specs=pl.BlockSpec((None, rows, tn), lambda l, j: (l, 0, j)),
        out_shape=jax.ShapeDtypeStruct((depth, rows, n), F32),
        compiler_params=_params(("parallel", "parallel"), 32),
        name="ada_mod",
    )(c_all, ada_w, ada_b.reshape(depth, 1, n))


def _ab_in_kernel(x_ref, mod_ref, g_ref, w_ref, hsum_ref, qg_ref, kg_ref, u_ref, q_ref, k_ref, v_ref):
    def head_norm(a, g):
        ms = _block_diag_dot((a * a).astype(BF16), hsum_ref) * (1.0 / HEAD_DIM)
        return a * lax.rsqrt(ms + EPS) * g

    for r0 in range(0, x_ref.shape[0], ROW_BLOCK):
        rows = slice(r0, r0 + ROW_BLOCK)
        h = _norm_mod(x_ref[rows, :], g_ref[...], mod_ref[0:1, :], mod_ref[1:2, :])
        z = _dot(h.astype(BF16), w_ref[...])
        u_ref[rows, :] = z[:, 0:D_A]
        q = head_norm(z[:, D_A:D_A + D_B], qg_ref[...]) * (LOG2_E / math.sqrt(HEAD_DIM))
        k = head_norm(z[:, D_A + D_B:D_A + 2 * D_B], kg_ref[...])
        half_rows = slice(r0 // 2, (r0 + ROW_BLOCK) // 2)
        for ref, val in ((q_ref, q), (k_ref, k), (v_ref, z[:, D_A + 2 * D_B:])):
            for parity in range(2):
                ref[parity, half_rows, :] = _tiles_of_parity(val, parity)


def _tiles_of_parity(rows, parity):
    return jnp.concatenate([rows[t * TILE_ROWS:(t + 1) * TILE_ROWS]
                            for t in range(parity, rows.shape[0] // TILE_ROWS, 2)], axis=0)


def _interleave_tiles(even, odd):
    pieces = []
    for t in range(even.shape[0] // TILE_ROWS):
        pieces += [even[t * TILE_ROWS:(t + 1) * TILE_ROWS], odd[t * TILE_ROWS:(t + 1) * TILE_ROWS]]
    return jnp.concatenate(pieces, axis=0)


def _ab_in(x, mod, g, w_in, hsum, qg, kg, tm=512):
    B, T, D = x.shape
    n = w_in.shape[1]
    tok = pl.BlockSpec((None, tm, D_B), lambda b, i: (b, i, 0))
    split = pl.BlockSpec((None, 2, tm // 2, D_B), lambda b, i: (b, 0, i, 0))
    out = jax.ShapeDtypeStruct((B, T, D_B), F32)
    out_split = jax.ShapeDtypeStruct((B, 2, T // 2, D_B), F32)
    return pl.pallas_call(
        _ab_in_kernel,
        grid=(B, T // tm),
        in_specs=[pl.BlockSpec((None, tm, D), lambda b, i: (b, i, 0)),
                  pl.BlockSpec((None, 6, D), lambda b, i: (b, 0, 0)),
                  _const_spec((1, D)), _const_spec((D, n)), _const_spec((D_B, D_B)),
                  _const_spec((1, D_B)), _const_spec((1, D_B))],
        out_specs=[tok, split, split, split],
        out_shape=[out, out_split, out_split, out_split],
        compiler_params=_params(("parallel", "parallel"), 48),
        name="ab_in",
    )(x, mod, g, w_in, hsum, qg, kg)


def _cmul(a, b):
    return a[0] * b[0] - a[1] * b[1], a[0] * b[1] + a[1] * b[0]


def _s5_param_kernel(lam_ref, dt_ref, b_re_ref, b_im_ref, ct_re_ref, ct_im_ref, c_re_ref, c_im_ref, d_ref,
                     wf_re_ref, wf_im_ref, wb_re_ref, wb_im_ref, kf_ref, kb_ref,
                     caf_re_ref, caf_im_ref, cab_re_ref, cab_im_ref, ap_ref):
    P, W = S5_STATE, S5_CHUNK * S5_GROUP
    kidx = lax.broadcasted_iota(jnp.int32, (P, W), 1) // S5_GROUP
    lane_ap = lax.broadcasted_iota(jnp.int32, (P, 128), 1)
    ap = jnp.zeros((P, 128), F32)
    c_re, c_im = c_re_ref[...], c_im_ref[...]
    outs = ((wf_re_ref, wf_im_ref, kf_ref, caf_re_ref, caf_im_ref),
            (wb_re_ref, wb_im_ref, kb_ref, cab_re_ref, cab_im_ref))
    for direction in range(2):
        lam_re = lam_ref[:, 2 * direction:2 * direction + 1]
        lam_im = lam_ref[:, 2 * direction + 1:2 * direction + 2]
        dt = jnp.exp(dt_ref[:, direction:direction + 1])
        mag = jnp.exp(lam_re * dt)
        a_re = mag * jnp.cos(lam_im * dt)
        a_im = mag * jnp.sin(lam_im * dt)
        den = lam_re * lam_re + lam_im * lam_im
        coef_re = ((a_re - 1.0) * lam_re + a_im * lam_im) / den
        coef_im = (a_im * lam_re - (a_re - 1.0) * lam_im) / den
        bb = _cmul((coef_re, coef_im), (b_re_ref[...], b_im_ref[...]))
        a_pow = (a_re, a_im)
        pw = (jnp.ones((P, W), F32), jnp.zeros((P, W), F32))
        for j in range(4):
            bit = ((kidx >> j) & 1) == 1
            pw = _cmul(pw, (jnp.where(bit, a_pow[0], 1.0), jnp.where(bit, a_pow[1], 0.0)))
            a_pow = _cmul(a_pow, a_pow)
        for j in range(S5_LEVELS):
            base = 4 * j + 2 * direction
            ap = jnp.where(lane_ap == base, a_pow[0], ap)
            ap = jnp.where(lane_ap == base + 1, a_pow[1], ap)
            a_pow = _cmul(a_pow, a_pow)
        w_re, w_im = _cmul(pw, bb)
        ca_re, ca_im = _cmul(_cmul(pw, (a_re, a_im)), (ct_re_ref[...], ct_im_ref[...]))
        hp = lax.Precision.HIGHEST
        kmat = (jnp.dot(c_re, w_re, precision=hp, preferred_element_type=F32)
                - jnp.dot(c_im, w_im, precision=hp, preferred_element_type=F32))
        o_w_re, o_w_im, o_k, o_ca_re, o_ca_im = outs[direction]
        o_w_re[...] = w_re
        o_w_im[...] = w_im
        o_k[...] = kmat
        o_ca_re[...] = ca_re
        o_ca_im[...] = -ca_im
    row = lax.broadcasted_iota(jnp.int32, (S5_GROUP, W), 0)
    lane = lax.broadcasted_iota(jnp.int32, (S5_GROUP, W), 1)
    lag0 = kb_ref[...] + jnp.where(row == lane, d_ref[...], 0.0)
    kf_ref[...] = kf_ref[...] + jnp.where(lane < S5_GROUP, lag0, 0.0)
    ap_ref[...] = ap


def _s5_tables(lam_re_f, lam_im_f, log_dt_f, lam_re_b, lam_im_b, log_dt_b, b_re, b_im, c_re, c_im, d_skip):
    G, P, C, L, W = S5_GROUPS, S5_STATE, S5_GROUP, S5_CHUNK, S5_CHUNK * S5_GROUP
    lam = jnp.stack([lam_re_f, lam_im_f, lam_re_b, lam_im_b], axis=-1)
    dts = jnp.stack([log_dt_f, log_dt_b], axis=-1).reshape(G, 1, 2)
    tile_k = lambda a: jnp.tile(a, (1, 1, L))
    ct = lambda a: tile_k(jnp.swapaxes(a, 1, 2))
    grp = lambda r, c: pl.BlockSpec((None, r, c), lambda g: (g, 0, 0))
    pw_out = jax.ShapeDtypeStruct((G, P, W), F32)
    k_out = jax.ShapeDtypeStruct((G, C, W), F32)
    wf_re, wf_im, wb_re, wb_im, kf, kb, caf_re, caf_im, cab_re, cab_im, ap = pl.pallas_call(
        _s5_param_kernel,
        grid=(G,),
        in_specs=[grp(P, 4), grp(1, 2), grp(P, W), grp(P, W), grp(P, W), grp(P, W), grp(C, P), grp(C, P), grp(C, 1)],
        out_specs=[grp(P, W)] * 4 + [grp(C, W)] * 2 + [grp(P, W)] * 4 + [grp(P, 128)],
        out_shape=[pw_out] * 4 + [k_out] * 2 + [pw_out] * 4 + [jax.ShapeDtypeStruct((G, P, 128), F32)],
        compiler_params=_params(("parallel",), 32),
        name="s5_params",
    )(lam, dts, tile_k(b_re), tile_k(b_im), ct(c_re), ct(c_im), c_re, c_im, d_skip.reshape(G, C, 1))

    NG, J, hp = S5_LANE_GROUPS, G // S5_LANE_GROUPS, lax.Precision.HIGHEST
    order = _s5_token_order()
    place = (order[:, :, None] == np.arange(L)).astype(np.float32)
    place_rev = (order[:, :, None] == L - 1 - np.arange(L)).astype(np.float32)
    lag = order[:, None, :] - order[:, :, None] + (L - 1)
    lag_hot = (lag[..., None] == np.arange(2 * L - 1)).astype(np.float32)
    lag_tab = jnp.concatenate([kb.reshape(G, C, L, C)[:, :, :0:-1], kf.reshape(G, C, L, C)], axis=2)
    toep = jnp.einsum("xstl,jxoli->jxsito", lag_hot, lag_tab.reshape(J, NG, C, 2 * L - 1, C), precision=hp)
    toep = toep.reshape(G, W, W)

    def slots(a, hot):
        return jnp.einsum("xqk,jxpkc->jxpqc", hot, a.reshape(J, NG, P, L, C), precision=hp).reshape(G, P, W)

    t_ = lambda a: jnp.swapaxes(a, 1, 2)
    sf_re, sf_im = t_(slots(wf_re, place_rev)), t_(slots(wf_im, place_rev))
    sb_re, sb_im = t_(slots(wb_re, place)), t_(slots(wb_im, place))
    w1 = jnp.concatenate([toep, sf_re, sf_im, sf_im, sf_re, sb_re, sb_im, sb_im, sb_re], axis=2)

    def pair_rows(a):
        a = a.reshape(G // 2, 2, P, W)
        zero = jnp.zeros_like(a[:, 0])
        return jnp.concatenate([jnp.concatenate([a[:, 0], zero], axis=2),
                                jnp.concatenate([zero, a[:, 1]], axis=2)], axis=1)

    w2 = jnp.concatenate([pair_rows(slots(caf_re, place)), pair_rows(slots(caf_im, place)),
                          pair_rows(slots(cab_re, place_rev)), pair_rows(slots(cab_im, place_rev))], axis=1)
    apj = ap[:, :, :4 * S5_LEVELS].reshape(G // 2, 2, P, S5_LEVELS, 4).transpose(0, 3, 4, 1, 2)
    mult = apj.reshape(G // 2, S5_LEVELS, 4 * 2 * P)
    return w1.astype(BF16), w2.astype(BF16), mult


def _s5_token_order():
    NG = S5_LANE_GROUPS
    order = np.empty((NG, S5_CHUNK), np.int64)
    for gl in range(NG):
        for half in range(S5_CHUNK // NG):
            for blk in range(NG):
                order[gl, half * NG + blk] = half * NG + (blk - gl) % NG
    return order


def _s5_kernel(u_ref, w1_ref, w2_ref, mult_ref, y_ref, ucbuf, ybuf, *, nch, nlev):
    row = lax.broadcasted_iota(jnp.int32, (nch, 128), 0)
    lane_blk = lax.broadcasted_iota(jnp.int32, (nch, 128), 1) // S5_GROUP
    W, L, C, NG = S5_CHUNK * S5_GROUP, S5_CHUNK, S5_GROUP, S5_LANE_GROUPS

    def pick(slabs, shift):
        acc = slabs[(0 - shift) % NG]
        for blk in range(1, NG):
            acc = jnp.where(lane_blk == blk, slabs[(blk - shift) % NG], acc)
        return acc

    def shift_down(x, s):
        if s % 8 == 0:
            return jnp.concatenate([jnp.zeros((s, 128), F32), x[:nch - s]], axis=0)
        return jnp.where(row >= s, pltpu.roll(x, s, 0), 0.0)

    def shift_up(x, s):
        if s % 8 == 0:
            return jnp.concatenate([x[s:], jnp.zeros((s, 128), F32)], axis=0)
        return jnp.where(row < nch - s, pltpu.roll(x, nch - s, 0), 0.0)

    for half in range(L // NG):
        rolled = []
        for k in range(NG):
            slab = u_ref[pl.ds(half * NG + k, nch, stride=L), :]
            rolled.append(pltpu.roll(slab, k * C, 1) if k else slab)
        for gl in range(NG):
            ucbuf[gl, :, half * 128:(half + 1) * 128] = pick(rolled, gl).astype(BF16)

    first_group = lax.broadcasted_iota(jnp.int32, (nch, 128), 1) < S5_STATE

    for m in range(NG // 2):
        r0 = _dot(ucbuf[2 * m], w1_ref[2 * m])
        r1 = _dot(ucbuf[2 * m + 1], w1_ref[2 * m + 1])
        xf_re = jnp.where(first_group, r0[:, W:W + 128], r1[:, W + 128:W + 256])
        xf_im = jnp.where(first_group, r0[:, W + 128:W + 256], r1[:, W:W + 128])
        xb_re = jnp.where(first_group, r0[:, W + 256:W + 384], r1[:, W + 384:W + 512])
        xb_im = jnp.where(first_group, r0[:, W + 384:W + 512], r1[:, W + 256:W + 384])
        mult = mult_ref[m]
        for j in range(nlev):
            s = 1 << j
            a_re, a_im = mult[j:j + 1, 0:128], mult[j:j + 1, 128:256]
            s_re, s_im = shift_down(xf_re, s), shift_down(xf_im, s)
            xf_re, xf_im = xf_re + (a_re * s_re - a_im * s_im), xf_im + (a_re * s_im + a_im * s_re)
            a_re, a_im = mult[j:j + 1, 256:384], mult[j:j + 1, 384:512]
            s_re, s_im = shift_up(xb_re, s), shift_up(xb_im, s)
            xb_re, xb_im = xb_re + (a_re * s_re - a_im * s_im), xb_im + (a_re * s_im + a_im * s_re)
        xs = jnp.concatenate([shift_down(xf_re, 1), shift_down(xf_im, 1), shift_up(xb_re, 1), shift_up(xb_im, 1)],
                             axis=1).astype(BF16)
        y = _dot(xs, w2_ref[m])
        ybuf[2 * m] = r0[:, 0:W] + y[:, 0:W]
        ybuf[2 * m + 1] = r1[:, 0:W] + y[:, W:]

    for half in range(L // NG):
        for k in range(NG):
            slab = pick([ybuf[gl, :, half * 128:(half + 1) * 128] for gl in range(NG)], k)
            y_ref[pl.ds(half * NG + k, nch, stride=L), :] = pltpu.roll(slab, (NG - k) * C, 1) if k else slab


def _s5(u, w1, w2, mult):
    B, T, D = u.shape
    nch, W, NG = T // S5_CHUNK, S5_CHUNK * S5_GROUP, S5_LANE_GROUPS
    nlev = int(math.log2(nch))
    assert 1 << nlev == nch and nlev <= S5_LEVELS
    tok = pl.BlockSpec((None, T, 128), lambda b, j: (b, 0, j))
    return pl.pallas_call(
        functools.partial(_s5_kernel, nch=nch, nlev=nlev),
        grid=(B, D // 128),
        in_specs=[tok,
                  pl.BlockSpec((NG, W, 3 * W), lambda b, j: (j, 0, 0)),
                  pl.BlockSpec((NG // 2, 2 * W, 2 * W), lambda b, j: (j, 0, 0)),
                  pl.BlockSpec((NG // 2, S5_LEVELS, 2 * W), lambda b, j: (j, 0, 0))],
        out_specs=tok,
        out_shape=jax.ShapeDtypeStruct((B, T, D), F32),
        scratch_shapes=[pltpu.VMEM((NG, nch, W), BF16), pltpu.VMEM((NG, nch, W), F32)],
        compiler_params=_params(("parallel", "parallel"), 32),
        name="s5_scan",
    )(u, w1, w2, mult)


def _attn_block_positions(d):
    a, c = np.arange(ATTN_QB), np.arange(ATTN_KW)
    if d == 1:
        token = lambda row, slab: 16 * ((row % slab) // TILE_ROWS) + TILE_ROWS * (row // slab) + row % TILE_ROWS
        return token(a, ATTN_QB // 2), token(c, ATTN_KW // 2) - ATTN_R
    if d == 4:
        pos = lambda row, slab: 4 * ((row % slab) // 2) + 2 * (row // slab) + row % 2
        return pos(a, ATTN_QB // 2), pos(c, ATTN_KW // 2) - ATTN_R
    return a, c - ATTN_R


def _attn_bias_table():
    slopes = 2.0 ** (-8.0 * np.arange(1, N_HEADS + 1, dtype=np.float64) / N_HEADS)
    tab = np.empty((N_HEADS // 2, len(DILATED_CONFIGS), 4, 2, ATTN_QB, ATTN_KW), np.float32)
    for bi, (_, d) in enumerate(DILATED_CONFIGS):
        pos_q, pos_k = _attn_block_positions(d)
        rel = pos_k[None, :] - pos_q[:, None]
        for ty in range(4):
            valid = np.abs(rel) <= ATTN_R
            if ty & 1:
                valid = valid & (pos_k[None, :] >= 0)
            if ty & 2:
                valid = valid & (pos_k[None, :] < ATTN_QB)
            for h in range(N_HEADS):
                dist = (np.abs(rel) * d).astype(np.float32)
                alibi = (-np.float32(slopes[h] * LOG2_E)) * dist
                bias = np.where(valid, alibi, np.float32(NEG_BIG))
                if ty == 3:
                    bias = np.concatenate([bias[:, ATTN_R:ATTN_R + ATTN_QB],
                                           np.full((ATTN_QB, 2 * ATTN_R), NEG_BIG, np.float32)], axis=1)
                tab[h // 2, bi, ty, h % 2] = bias
    return tab.reshape(N_HEADS // 2, len(DILATED_CONFIGS) * 4, 2 * ATTN_QB, ATTN_KW)


def _attn_kernel(q_ref, k_ref, v_ref, bias_ref, o_ref, kp, vp, acc_o, acc_m, acc_l, *, T):
    half, pad = T // 2, ATTN_PAD // 2
    zeros = jnp.zeros((pad, 128), F32)
    for buf, src in ((kp, k_ref), (vp, v_ref)):
        for parity in range(2):
            buf[parity, 0:pad, :] = zeros
            buf[parity, pad + half:pad + half + pad, :] = zeros
            buf[parity, pad:pad + half, :] = src[parity]
    first_head = lax.broadcasted_iota(jnp.int32, (ATTN_QB, 128), 1) < HEAD_DIM
    ones = jnp.ones((ATTN_KW, 128), BF16)
    n_branch = len(DILATED_CONFIGS)

    def block_slabs(d, nblk, it, u):
        idx = it * ATTN_UNROLL[d] + u
        tile = lambda n: pl.multiple_of(TILE_ROWS * n, TILE_ROWS)
        if d == 1:
            blk, nq, nr = idx, ATTN_QB // 2, ATTN_R // 2
            q = [(p, pl.ds(tile(nq // TILE_ROWS * blk), nq), nq) for p in range(2)]
            k = [(p, pl.ds(pad - nr + tile(nq // TILE_ROWS * blk), 2 * nq), 2 * nq) for p in range(2)]
        elif d == 4:
            r, blk, nq, nr = u % d, idx // d, ATTN_QB // 2, ATTN_R // 2
            q = [(p, pl.ds(tile(nq // 2 * blk) + r, nq, stride=4), nq) for p in range(2)]
            k = [(p, pl.ds(pad - 4 * nr + tile(nq // 2 * blk) + r, 2 * nq, stride=4), 2 * nq) for p in range(2)]
        else:
            assert d == 2 * TILE_ROWS and ATTN_UNROLL[d] % TILE_ROWS == 0
            sub, parity, blk = u % TILE_ROWS, (idx // TILE_ROWS) % 2, idx // d
            q = [(parity, pl.ds(tile(ATTN_QB * blk) + sub, ATTN_QB, stride=TILE_ROWS), ATTN_QB)]
            back, nk = (0, ATTN_QB) if nblk == 1 else (TILE_ROWS * ATTN_R, ATTN_KW)
            k = [(parity, pl.ds(pad - back + tile(ATTN_QB * blk) + sub, nk, stride=TILE_ROWS), nk)]
        return q, k, blk

    def gather(ref, slabs):
        return jnp.concatenate([ref[p, rows, :] for p, rows, _ in slabs], axis=0)

    def scatter(ref, slabs, val):
        start = 0
        for p, rows, n in slabs:
            ref[p, rows, :] = val[start:start + n]
            start += n

    def block_softmax(bi, d, nblk, it, u):
        rows, keys, blk = block_slabs(d, nblk, it, u)
        qf = gather(q_ref, rows)
        q2 = jnp.concatenate([jnp.where(first_head, qf, 0.0), jnp.where(first_head, 0.0, qf)], axis=0).astype(BF16)
        if nblk == 1:
            bias = bias_ref[bi * 4 + 3, :, 0:ATTN_QB]
        else:
            bias = bias_ref[bi * 4 + jnp.where(blk == 0, 1, 0) + jnp.where(blk == nblk - 1, 2, 0)]
        kw = gather(kp, keys).astype(BF16)
        s = lax.dot_general(q2, kw, (((1,), (1,)), ((), ())), preferred_element_type=F32)
        s = s + bias
        m = jnp.max(s, axis=-1, keepdims=True)
        p = jnp.exp2(s - m).astype(BF16)
        ov = _dot(p, jnp.concatenate([gather(vp, keys).astype(BF16), ones[0:kw.shape[0]]], axis=1))
        o_new = jnp.where(first_head, ov[0:ATTN_QB, 0:128], ov[ATTN_QB:, 0:128])
        l_new = jnp.where(first_head, ov[0:ATTN_QB, 128:], ov[ATTN_QB:, 128:])
        m_new = jnp.where(first_head, m[0:ATTN_QB], m[ATTN_QB:])
        return rows, o_new, m_new, l_new

    for bi, (_, d) in enumerate(DILATED_CONFIGS):
        nblk = T // d // ATTN_QB

        def group(it, carry, bi=bi, d=d, nblk=nblk):
            new = [block_softmax(bi, d, nblk, it, u) for u in range(ATTN_UNROLL[d])]
            if bi == 0:
                for rows, o_new, m_new, l_new in new:
                    scatter(acc_o, rows, o_new)
                    scatter(acc_m, rows, m_new)
                    scatter(acc_l, rows, l_new)
                return carry
            old = [(gather(acc_o, rows), gather(acc_m, rows), gather(acc_l, rows)) for rows, _, _, _ in new]
            merged = []
            for (rows, o_new, m_new, l_new), (o_old, m_old, l_old) in zip(new, old):
                e = jnp.exp2(-jnp.abs(m_old - m_new))
                keep = m_old >= m_new
                a = jnp.where(keep, 1.0, e)
                b = jnp.where(keep, e, 1.0)
                merged.append((rows, a * o_old + b * o_new, jnp.maximum(m_old, m_new), a * l_old + b * l_new))
            for rows, o_tot, m_tot, l_tot in merged:
                if bi == n_branch - 1:
                    scatter(o_ref, rows, o_tot / l_tot)
                else:
                    scatter(acc_o, rows, o_tot)
                    scatter(acc_m, rows, m_tot)
                    scatter(acc_l, rows, l_tot)
            return carry

        lax.fori_loop(0, d * nblk // ATTN_UNROLL[d], group, 0)


def _attention(q, k, v, bias):
    B, _, half, _ = q.shape
    T = 2 * half
    assert T % (ATTN_QB * DILATED_CONFIGS[-1][1]) == 0
    pair = pl.BlockSpec((None, 2, half, 128), lambda b, p: (b, 0, 0, p))
    nb = bias.shape[1]
    acc = pltpu.VMEM((2, half, 128), F32)
    padded = pltpu.VMEM((2, half + ATTN_PAD, 128), F32)
    return pl.pallas_call(
        functools.partial(_attn_kernel, T=T),
        grid=(B, N_HEADS // 2),
        in_specs=[pair, pair, pair, pl.BlockSpec((None, nb, 2 * ATTN_QB, ATTN_KW), lambda b, p: (p, 0, 0, 0))],
        out_specs=pair,
        out_shape=jax.ShapeDtypeStruct((B, 2, half, D_B), F32),
        scratch_shapes=[padded, padded, acc, acc, acc],
        compiler_params=_params(("parallel", "arbitrary"), 48),
        name="dilated_attn",
    )(q, k, v, bias)


def _ab_out_kernel(x_ref, ya_ref, yb_ref, mod_ref, wglu_ref, bglu_ref, wo_ref, o_ref):
    for r0 in range(0, x_ref.shape[0], ROW_BLOCK):
        rows = slice(r0, r0 + ROW_BLOCK)
        y = ya_ref[rows, :]
        y = 0.5 * y * (1.0 + jnp.tanh(math.sqrt(2.0 / math.pi) * (y + 0.044715 * (y * y * y))))
        y = y * _sigmoid(_dot(y.astype(BF16), wglu_ref[...]) + bglu_ref[...])
        half_rows = slice(r0 // 2, (r0 + ROW_BLOCK) // 2)
        yb = _interleave_tiles(yb_ref[0, half_rows, :], yb_ref[1, half_rows, :])
        out = _dot(y.astype(BF16), wo_ref[0:D_A, :]) + _dot(yb.astype(BF16), wo_ref[D_A:, :])
        o_ref[rows, :] = x_ref[rows, :] + mod_ref[2:3, :] * out


def _ab_out(x, ya, yb, mod, w_glu, b_glu, w_out, tm=512):
    B, T, D = x.shape
    half = pl.BlockSpec((None, tm, D_A), lambda b, i: (b, i, 0))
    split = pl.BlockSpec((None, 2, tm // 2, D_B), lambda b, i: (b, 0, i, 0))
    full = pl.BlockSpec((None, tm, D), lambda b, i: (b, i, 0))
    return pl.pallas_call(
        _ab_out_kernel,
        grid=(B, T // tm),
        in_specs=[full, half, split, pl.BlockSpec((None, 6, D), lambda b, i: (b, 0, 0)),
                  _const_spec((D_A, D_A)), _const_spec((1, D_A)), _const_spec((D, D))],
        out_specs=full,
        out_shape=jax.ShapeDtypeStruct((B, T, D), F32),
        compiler_params=_params(("parallel", "parallel"), 48),
        name="ab_out",
    )(x, ya, yb, mod, w_glu, b_glu, w_out)


def _halo_specs(tm, T, D):
    nh = T // HALO
    per = tm // HALO
    main = pl.BlockSpec((None, tm, D), lambda b, i: (b, i, 0))
    prev = pl.BlockSpec((None, HALO, D), lambda b, i: (b, jnp.maximum(i * per - 1, 0), 0))
    nxt = pl.BlockSpec((None, HALO, D), lambda b, i: (b, jnp.minimum((i + 1) * per, nh - 1), 0))
    return main, prev, nxt


def _ffn_kernel(x_ref, xp_ref, xn_ref, mod_ref, g_ref, wg_ref, wu_ref, cw_ref, cb_ref, wd_ref, o_ref, gbuf, abuf, *, tm):
    x = x_ref[...]
    g, shift, scale = g_ref[...], mod_ref[3:4, :], mod_ref[4:5, :]
    has_prev = jnp.where(pl.program_id(1) > 0, 1.0, 0.0)
    has_next = jnp.where(pl.program_id(1) < pl.num_programs(1) - 1, 1.0, 0.0)
    h = _norm_mod(x, g, shift, scale)
    he = jnp.concatenate([_norm_mod(xp_ref[...], g, shift, scale) * has_prev, h,
                          _norm_mod(xn_ref[...], g, shift, scale) * has_next], axis=0).astype(BF16)
    h = he[HALO:HALO + tm]
    for c in range(D_FF // FFN_CHUNK):
        cols = slice(c * FFN_CHUNK, (c + 1) * FFN_CHUNK)
        gbuf[:, cols] = _dot(he, wg_ref[:, cols])
        gate = (gbuf[pl.ds(HALO - 1, tm), cols] * cw_ref[0:1, cols] + gbuf[pl.ds(HALO, tm), cols] * cw_ref[1:2, cols]
                + gbuf[pl.ds(HALO + 1, tm), cols] * cw_ref[2:3, cols] + cb_ref[:, cols])
        up = _dot(h, wu_ref[:, cols])
        abuf[:, cols] = ((gate * _sigmoid(gate)) * up).astype(BF16)
    o_ref[...] = x + mod_ref[5:6, :] * _dot(abuf[...], wd_ref[...])


def _ffn(x, mod, g, w_gate, w_up, conv_w, conv_b, w_down, tm=512):
    B, T, D = x.shape
    main, prev, nxt = _halo_specs(tm, T, D)
    return pl.pallas_call(
        functools.partial(_ffn_kernel, tm=tm),
        grid=(B, T // tm),
        in_specs=[main, prev, nxt, pl.BlockSpec((None, 6, D), lambda b, i: (b, 0, 0)), _const_spec((1, D)),
                  _const_spec((D, D_FF)), _const_spec((D, D_FF)), _const_spec((3, D_FF)), _const_spec((1, D_FF)),
                  _const_spec((D_FF, D))],
        out_specs=main,
        out_shape=jax.ShapeDtypeStruct((B, T, D), F32),
        scratch_shapes=[pltpu.VMEM((tm + 2 * HALO, D_FF), F32), pltpu.VMEM((tm, D_FF), BF16)],
        compiler_params=_params(("parallel", "arbitrary"), 56),
        name="conv_ffn",
    )(x, x, x, mod, g, w_gate, w_up, conv_w, conv_b, w_down)


def _cd_in_kernel(x_ref, xp_ref, xn_ref, mod_ref, g_ref, w_ref, cbd_ref, sbd_ref, sw_ref, ab_ref, yd_ref, cbuf, *, tm):
    g, shift, scale = g_ref[...], mod_ref[0:1, :], mod_ref[1:2, :]
    has_prev = jnp.where(pl.program_id(1) > 0, 1.0, 0.0)
    has_next = jnp.where(pl.program_id(1) < pl.num_programs(1) - 1, 1.0, 0.0)
    he = jnp.concatenate([_norm_mod(xp_ref[...], g, shift, scale) * has_prev, _norm_mod(x_ref[...], g, shift, scale),
                          _norm_mod(xn_ref[...], g, shift, scale) * has_next], axis=0).astype(BF16)
    h = he[HALO:HALO + tm]
    for c0 in range(0, D_D, CD_CHUNK):
        cols = slice(c0, c0 + CD_CHUNK)
        hs = _dot(he, w_ref[:, D_C + c0:D_C + c0 + CD_CHUNK])
        gc = _dot(he, w_ref[:, D_C + 2 * D_D + c0:D_C + 2 * D_D + c0 + CD_CHUNK])
        cbuf[:, cols] = gc * hs
        conv = (cbuf[pl.ds(HALO - 1, tm), cols] * sw_ref[0:1, cols] + cbuf[pl.ds(HALO, tm), cols] * sw_ref[1:2, cols]
                + cbuf[pl.ds(HALO + 1, tm), cols] * sw_ref[2:3, cols])
        yd_ref[:, cols] = _dot(h, w_ref[:, D_C + D_D + c0:D_C + D_D + c0 + CD_CHUNK]) * conv
    uc = _dot(h, w_ref[:, 0:D_C]).astype(BF16)
    ab_ref[0] = _block_diag_dot(uc, cbd_ref).astype(BF16)
    ab_ref[1] = _block_diag_dot(uc, sbd_ref).astype(BF16)


def _cd_in(x, mod, g, w_in, cbd, sbd, sconv_w, tm=512):
    B, T, D = x.shape
    main, prev, nxt = _halo_specs(tm, T, D)
    return pl.pallas_call(
        functools.partial(_cd_in_kernel, tm=tm),
        grid=(B, T // tm),
        in_specs=[main, prev, nxt, pl.BlockSpec((None, 6, D), lambda b, i: (b, 0, 0)), _const_spec((1, D)),
                  _const_spec((D, w_in.shape[1])), _const_spec((D_C, D_C)), _const_spec((D_C, D_C)),
                  _const_spec((3, D_D))],
        out_specs=[pl.BlockSpec((None, 2, tm, D_C), lambda b, i: (b, 0, i, 0)),
                   pl.BlockSpec((None, tm, D_D), lambda b, i: (b, i, 0))],
        out_shape=[jax.ShapeDtypeStruct((B, 2, T, D_C), BF16), jax.ShapeDtypeStruct((B, T, D_D), F32)],
        scratch_shapes=[pltpu.VMEM((tm + 2 * HALO, D_D), F32)],
        compiler_params=_params(("parallel", "arbitrary"), 48),
        name="cd_in",
    )(x, x, x, mod, g, w_in, cbd, sbd, sconv_w)


def _cd_out_kernel(dft_ref, ab_ref, yd_ref, x_ref, mod_ref, wo_ref, o_ref, fold, *, T, tf):
    half = T // 2

    @pl.when(pl.program_id(1) == 0)
    def _():
        r = lax.broadcasted_iota(jnp.int32, (tf, tf + HALO), 0)
        c = lax.broadcasted_iota(jnp.int32, (tf, tf + HALO), 1)
        mirror = jnp.where(c == tf - r, 1.0, 0.0).astype(BF16)
        first_row = lax.broadcasted_iota(jnp.int32, (tf, D_C), 0) == 0
        for part, sign in ((0, 1.0), (1, -1.0)):
            for s0 in range(0, half, tf):
                nxt = ab_ref[part, T - s0:T - s0 + HALO, :] if s0 else jnp.zeros((HALO, D_C), BF16)
                src = jnp.concatenate([ab_ref[part, T - s0 - tf:T - s0, :], nxt], axis=0)
                folded = ab_ref[part, s0:s0 + tf, :].astype(F32) + sign * _dot(mirror, src)
                if part == 1 and s0 == 0:
                    folded = jnp.where(first_row, ab_ref[0, half:half + HALO, :].astype(F32)[0:1, :], folded)
                fold[part * half + s0:part * half + s0 + tf, :] = folded.astype(BF16)

    for r0 in range(0, x_ref.shape[0], ROW_BLOCK):
        rows = slice(r0, r0 + ROW_BLOCK)
        yc = _dot(dft_ref[rows, :], fold[...])
        out = _dot(yc.astype(BF16), wo_ref[0:D_C, :]) + _dot(yd_ref[rows, :].astype(BF16), wo_ref[D_C:, :])
        o_ref[rows, :] = x_ref[rows, :] + mod_ref[2:3, :] * out


def _cd_out(dft, ab, yd, x, mod, w_out, tm=512):
    B, T, D = x.shape
    return pl.pallas_call(
        functools.partial(_cd_out_kernel, T=T, tf=256),
        grid=(B, T // tm),
        in_specs=[pl.BlockSpec((tm, T), lambda b, i: (i, 0)),
                  pl.BlockSpec((None, 2, T, D_C), lambda b, i: (b, 0, 0, 0)),
                  pl.BlockSpec((None, tm, D_D), lambda b, i: (b, i, 0)),
                  pl.BlockSpec((None, tm, D), lambda b, i: (b, i, 0)),
                  pl.BlockSpec((None, 6, D), lambda b, i: (b, 0, 0)), _const_spec((D, D))],
        out_specs=pl.BlockSpec((None, tm, D), lambda b, i: (b, i, 0)),
        out_shape=jax.ShapeDtypeStruct((B, T, D), F32),
        scratch_shapes=[pltpu.VMEM((T, D_C), BF16)],
        compiler_params=_params(("parallel", "arbitrary"), 56),
        name="cd_out",
    )(dft, ab, yd, x, mod, w_out)


def _seq_dft_matrix(T):
    rows = 64
    s = jnp.arange(T // 2, dtype=jnp.int32)[None, :]
    angle = lambda t: ((t[:, None] * s) % T).astype(F32) * (2.0 * math.pi / T)
    ang_a = angle(jnp.arange(T // rows, dtype=jnp.int32) * rows)[:, None, :]
    ang_b = angle(jnp.arange(rows, dtype=jnp.int32))[None, :, :]
    scale = 1.0 / math.sqrt(T)
    cos_a, sin_a, cos_b, sin_b = jnp.cos(ang_a) * scale, jnp.sin(ang_a) * scale, jnp.cos(ang_b), jnp.sin(ang_b)
    cos_ts = (cos_a * cos_b - sin_a * sin_b).reshape(T, T // 2)
    sin_ts = (sin_a * cos_b + cos_a * sin_b).reshape(T, T // 2)
    nyquist = jnp.where(jnp.arange(T) % 2 == 0, scale, -scale)[:, None]
    return jnp.concatenate([cos_ts, jnp.where(s == 0, nyquist, -sin_ts)], axis=1).astype(BF16)


def _channel_dft_matrices():
    c = np.arange(D_C)
    ang = 2.0 * np.pi * ((c[:, None] % FNET_GROUP_DIM) * (c[None, :] % FNET_GROUP_DIM) % FNET_GROUP_DIM) / FNET_GROUP_DIM
    same = (c[:, None] // FNET_GROUP_DIM) == (c[None, :] // FNET_GROUP_DIM)
    scale = 1.0 / math.sqrt(FNET_GROUP_DIM)
    cbd = np.where(same, np.cos(ang) * scale, 0.0).astype(np.float32)
    sbd = np.where(same, np.sin(ang) * scale, 0.0).astype(np.float32)
    return jnp.asarray(cbd).astype(BF16), jnp.asarray(sbd).astype(BF16)


def _trunk(x, mods, wts):
    B, T, D = x.shape
    nch = T // S5_CHUNK
    mod = mods[0]
    u, q, k, v = _ab_in(x, mod, wts["norm_mix_g"][0], wts["ab_w_in"], wts["hsum"], wts["q_g"], wts["k_g"])
    ya = _s5(u, wts["s5_w1"], wts["s5_w2"], wts["s5_mult"])
    yb = _attention(q, k, v, wts["attn_bias"])
    x = _ab_out(x, ya, yb, mod, wts["s5_w_glu"], wts["s5_b_glu"], wts["ab_w_out"])
    x = _ffn(x, mod, wts["norm_ffn_g"][0], *wts["ffn"][0])
    mod = mods[1]
    ab, yd = _cd_in(x, mod, wts["norm_mix_g"][1], wts["cd_w_in"], wts["cbd"], wts["sbd"], wts["sconv_w"])
    x = _cd_out(_seq_dft_matrix(T), ab, yd, x, mod, wts["cd_w_out"])
    x = _ffn(x, mod, wts["norm_ffn_g"][1], *wts["ffn"][1])
    return x


def kernel(x_prompt, x_sample, c_prompt, c_sample, ada_w, ada_b, norm_mix_g, norm_ffn_g, ffn_w_gate, ffn_w_up, ffn_conv_w, ffn_conv_b, ffn_w_down, ab_w_in, ab_w_out, s5_lam_re_f, s5_lam_im_f, s5_log_dt_f, s5_lam_re_b, s5_lam_im_b, s5_log_dt_b, s5_b_re, s5_b_im, s5_c_re, s5_c_im, s5_d, s5_w_glu, s5_b_glu, q_norm_g, k_norm_g, cd_w_in, cd_w_out, sconv_w):
    depth = ada_w.shape[0]
    assert depth == 2 and ab_w_in.shape[0] == 1 and cd_w_in.shape[0] == 1
    bp = x_prompt.shape[0]
    mod_all = _ada(jnp.concatenate([c_prompt, c_sample], axis=0), ada_w, ada_b)
    s5_w1, s5_w2, s5_mult = _s5_tables(s5_lam_re_f[0], s5_lam_im_f[0], s5_log_dt_f[0], s5_lam_re_b[0],
                                       s5_lam_im_b[0], s5_log_dt_b[0], s5_b_re[0], s5_b_im[0],
                                       s5_c_re[0], s5_c_im[0], s5_d[0])
    head = np.arange(D_B) // HEAD_DIM
    cbd, sbd = _channel_dft_matrices()
    wts = dict(
        norm_mix_g=norm_mix_g.reshape(depth, 1, D_MODEL), norm_ffn_g=norm_ffn_g.reshape(depth, 1, D_MODEL),
        ab_w_in=ab_w_in[0].astype(BF16), ab_w_out=ab_w_out[0].astype(BF16),
        hsum=jnp.asarray(head[:, None] == head[None, :], BF16),
        q_g=jnp.tile(q_norm_g[0], N_HEADS).reshape(1, D_B), k_g=jnp.tile(k_norm_g[0], N_HEADS).reshape(1, D_B),
        s5_w1=s5_w1, s5_w2=s5_w2, s5_mult=s5_mult,
        s5_w_glu=s5_w_glu[0].astype(BF16), s5_b_glu=s5_b_glu[0].reshape(1, D_A),
        attn_bias=jnp.asarray(_attn_bias_table()),
        ffn=[(ffn_w_gate[l].astype(BF16), ffn_w_up[l].astype(BF16), ffn_conv_w[l], ffn_conv_b[l].reshape(1, D_FF),
              ffn_w_down[l].astype(BF16)) for l in range(depth)],
        cd_w_in=cd_w_in[0].astype(BF16), cd_w_out=cd_w_out[0].astype(BF16), cbd=cbd, sbd=sbd, sconv_w=sconv_w[0],
    )
    outs = []
    for x, rows in ((x_prompt, slice(0, bp)), (x_sample, slice(bp, None))):
        mods = [mod_all[l, rows].reshape(x.shape[0], 6, D_MODEL) for l in range(depth)]
        outs.append(_trunk(x, mods, wts))
    return tuple(outs)
```

```python
import functools
import math

import jax
import jax.numpy as jnp
import numpy as np
from jax import lax
from jax.experimental import pallas as pl
from jax.experimental.pallas import tpu as pltpu

F32 = jnp.float32
BF16 = jnp.bfloat16

D_MODEL = 1024
D_A = 512
S5_GROUP = 16
S5_GROUPS = 32
S5_STATE = 64
S5_CHUNK = 16
S5_LEVELS = 8
S5_LANE_GROUPS = 128 // S5_GROUP
D_B = 512
HEAD_DIM = 64
N_HEADS = 8
DILATED_CONFIGS = ((128, 1), (512, 4), (2048, 16))
ATTN_R = 64
ATTN_QB = 128
ATTN_KW = ATTN_QB + 2 * ATTN_R
ATTN_PAD = ATTN_R * 16
ATTN_UNROLL = {1: 16, 4: 16, 16: 16}
NEG_BIG = -1e30
LOG2_E = 1.4426950408889634
D_C = 512
FNET_GROUP_DIM = 128
D_D = 512
D_FF = 2816
FFN_CHUNK = 256
CD_CHUNK = 256
EPS = 1e-6
ROW_BLOCK = 256
TILE_ROWS = 8
MXU_TILE = 256
HALO = 16
MIB = 2 ** 20


def _params(sem, vmem_mib):
    return pltpu.CompilerParams(dimension_semantics=sem, vmem_limit_bytes=vmem_mib * MIB)


def _const_spec(shape):
    nd = len(shape)
    return pl.BlockSpec(shape, lambda *_: (0,) * nd, pipeline_mode=pl.Buffered(1))


def _dot(a, b):
    return jnp.dot(a, b, preferred_element_type=F32)


def _block_diag_dot(a, m_ref):
    n = m_ref.shape[0]
    return jnp.concatenate([_dot(a[:, c:c + MXU_TILE], m_ref[c:c + MXU_TILE, c:c + MXU_TILE])
                            for c in range(0, n, MXU_TILE)], axis=1)


def _norm_mod(x, g, shift, scale):
    ms = jnp.mean(x * x, axis=-1, keepdims=True)
    y = x * lax.rsqrt(ms + EPS) * g
    return y * (1.0 + scale) + shift


def _sigmoid(x):
    return 1.0 / (1.0 + jnp.exp(-x))


def _ada_kernel(c_ref, w_ref, b_ref, o_ref):
    c = c_ref[...]
    cond = (c * _sigmoid(c)).astype(BF16)
    o_ref[...] = _dot(cond, w_ref[...].astype(BF16)) + b_ref[...]


def _ada(c_all, ada_w, ada_b):
    depth, d, n = ada_w.shape
    rows = c_all.shape[0]
    tn = 1536
    return pl.pallas_call(
        _ada_kernel,
        grid=(depth, n // tn),
        in_specs=[pl.BlockSpec((rows, d), lambda l, j: (0, 0)),
                  pl.BlockSpec((None, d, tn), lambda l, j: (l, 0, j)),
                  pl.BlockSpec((None, 1, tn), lambda l, j: (l, 0, j))],
        out_specs=pl.BlockSpec((None, rows, tn), lambda l, j: (l, 0, j)),
        out_shape=jax.ShapeDtypeStruct((depth, rows, n), F32),
        compiler_params=_params(("parallel", "parallel"), 32),
        name="ada_mod",
    )(c_all, ada_w, ada_b.reshape(depth, 1, n))


def _ab_in_kernel(x_ref, mod_ref, g_ref, w_ref, hsum_ref, qg_ref, kg_ref, u_ref, q_ref, k_ref, v_ref):
    def head_norm(a, g):
        ms = _block_diag_dot((a * a).astype(BF16), hsum_ref) * (1.0 / HEAD_DIM)
        return a * lax.rsqrt(ms + EPS) * g

    for r0 in range(0, x_ref.shape[0], ROW_BLOCK):
        rows = slice(r0, r0 + ROW_BLOCK)
        h = _norm_mod(x_ref[rows, :], g_ref[...], mod_ref[0:1, :], mod_ref[1:2, :])
        z = _dot(h.astype(BF16), w_ref[...])
        u_ref[rows, :] = z[:, 0:D_A]
        q = head_norm(z[:, D_A:D_A + D_B], qg_ref[...]) * (LOG2_E / math.sqrt(HEAD_DIM))
        k = head_norm(z[:, D_A + D_B:D_A + 2 * D_B], kg_ref[...])
        half_rows = slice(r0 // 2, (r0 + ROW_BLOCK) // 2)
        for ref, val in ((q_ref, q), (k_ref, k), (v_ref, z[:, D_A + 2 * D_B:])):
            for parity in range(2):
                ref[parity, half_rows, :] = _tiles_of_parity(val, parity)


def _tiles_of_parity(rows, parity):
    return jnp.concatenate([rows[t * TILE_ROWS:(t + 1) * TILE_ROWS]
                            for t in range(parity, rows.shape[0] // TILE_ROWS, 2)], axis=0)


def _interleave_tiles(even, odd):
    pieces = []
    for t in range(even.shape[0] // TILE_ROWS):
        pieces += [even[t * TILE_ROWS:(t + 1) * TILE_ROWS], odd[t * TILE_ROWS:(t + 1) * TILE_ROWS]]
    return jnp.concatenate(pieces, axis=0)


def _ab_in(x, mod, g, w_in, hsum, qg, kg, tm=1024):
    B, T, D = x.shape
    n = w_in.shape[1]
    tok = pl.BlockSpec((None, tm, D_B), lambda b, i: (b, i, 0))
    split = pl.BlockSpec((None, 2, tm // 2, D_B), lambda b, i: (b, 0, i, 0))
    out = jax.ShapeDtypeStruct((B, T, D_B), F32)
    out_split = jax.ShapeDtypeStruct((B, 2, T // 2, D_B), F32)
    return pl.pallas_call(
        _ab_in_kernel,
        grid=(B, T // tm),
        in_specs=[pl.BlockSpec((None, tm, D), lambda b, i: (b, i, 0)),
                  pl.BlockSpec((None, 6, D), lambda b, i: (b, 0, 0)),
                  _const_spec((1, D)), _const_spec((D, n)), _const_spec((D_B, D_B)),
                  _const_spec((1, D_B)), _const_spec((1, D_B))],
        out_specs=[tok, split, split, split],
        out_shape=[out, out_split, out_split, out_split],
        compiler_params=_params(("parallel", "parallel"), 48),
        name="ab_in",
    )(x, mod, g, w_in, hsum, qg, kg)


def _cmul(a, b):
    return a[0] * b[0] - a[1] * b[1], a[0] * b[1] + a[1] * b[0]


def _s5_param_kernel(lam_ref, dt_ref, b_re_ref, b_im_ref, ct_re_ref, ct_im_ref, c_re_ref, c_im_ref, d_ref,
                     wf_re_ref, wf_im_ref, wb_re_ref, wb_im_ref, kf_ref, kb_ref,
                     caf_re_ref, caf_im_ref, cab_re_ref, cab_im_ref, ap_ref):
    P, W = S5_STATE, S5_CHUNK * S5_GROUP
    kidx = lax.broadcasted_iota(jnp.int32, (P, W), 1) // S5_GROUP
    lane_ap = lax.broadcasted_iota(jnp.int32, (P, 128), 1)
    ap = jnp.zeros((P, 128), F32)
    c_re, c_im = c_re_ref[...], c_im_ref[...]
    outs = ((wf_re_ref, wf_im_ref, kf_ref, caf_re_ref, caf_im_ref),
            (wb_re_ref, wb_im_ref, kb_ref, cab_re_ref, cab_im_ref))
    for direction in range(2):
        lam_re = lam_ref[:, 2 * direction:2 * direction + 1]
        lam_im = lam_ref[:, 2 * direction + 1:2 * direction + 2]
        dt = jnp.exp(dt_ref[:, direction:direction + 1])
        mag = jnp.exp(lam_re * dt)
        a_re = mag * jnp.cos(lam_im * dt)
        a_im = mag * jnp.sin(lam_im * dt)
        den = lam_re * lam_re + lam_im * lam_im
        coef_re = ((a_re - 1.0) * lam_re + a_im * lam_im) / den
        coef_im = (a_im * lam_re - (a_re - 1.0) * lam_im) / den
        bb = _cmul((coef_re, coef_im), (b_re_ref[...], b_im_ref[...]))
        a_pow = (a_re, a_im)
        pw = (jnp.ones((P, W), F32), jnp.zeros((P, W), F32))
        for j in range(4):
            bit = ((kidx >> j) & 1) == 1
            pw = _cmul(pw, (jnp.where(bit, a_pow[0], 1.0), jnp.where(bit, a_pow[1], 0.0)))
            a_pow = _cmul(a_pow, a_pow)
        for j in range(S5_LEVELS):
            base = 4 * j + 2 * direction
            ap = jnp.where(lane_ap == base, a_pow[0], ap)
            ap = jnp.where(lane_ap == base + 1, a_pow[1], ap)
            a_pow = _cmul(a_pow, a_pow)
        w_re, w_im = _cmul(pw, bb)
        ca_re, ca_im = _cmul(_cmul(pw, (a_re, a_im)), (ct_re_ref[...], ct_im_ref[...]))
        hp = lax.Precision.HIGHEST
        kmat = (jnp.dot(c_re, w_re, precision=hp, preferred_element_type=F32)
                - jnp.dot(c_im, w_im, precision=hp, preferred_element_type=F32))
        o_w_re, o_w_im, o_k, o_ca_re, o_ca_im = outs[direction]
        o_w_re[...] = w_re
        o_w_im[...] = w_im
        o_k[...] = kmat
        o_ca_re[...] = ca_re
        o_ca_im[...] = -ca_im
    row = lax.broadcasted_iota(jnp.int32, (S5_GROUP, W), 0)
    lane = lax.broadcasted_iota(jnp.int32, (S5_GROUP, W), 1)
    lag0 = kb_ref[...] + jnp.where(row == lane, d_ref[...], 0.0)
    kf_ref[...] = kf_ref[...] + jnp.where(lane < S5_GROUP, lag0, 0.0)
    ap_ref[...] = ap


def _s5_tables(lam_re_f, lam_im_f, log_dt_f, lam_re_b, lam_im_b, log_dt_b, b_re, b_im, c_re, c_im, d_skip):
    G, P, C, L, W = S5_GROUPS, S5_STATE, S5_GROUP, S5_CHUNK, S5_CHUNK * S5_GROUP
    lam = jnp.stack([lam_re_f, lam_im_f, lam_re_b, lam_im_b], axis=-1)
    dts = jnp.stack([log_dt_f, log_dt_b], axis=-1).reshape(G, 1, 2)
    tile_k = lambda a: jnp.tile(a, (1, 1, L))
    ct = lambda a: tile_k(jnp.swapaxes(a, 1, 2))
    grp = lambda r, c: pl.BlockSpec((None, r, c), lambda g: (g, 0, 0))
    pw_out = jax.ShapeDtypeStruct((G, P, W), F32)
    k_out = jax.ShapeDtypeStruct((G, C, W), F32)
    wf_re, wf_im, wb_re, wb_im, kf, kb, caf_re, caf_im, cab_re, cab_im, ap = pl.pallas_call(
        _s5_param_kernel,
        grid=(G,),
        in_specs=[grp(P, 4), grp(1, 2), grp(P, W), grp(P, W), grp(P, W), grp(P, W), grp(C, P), grp(C, P), grp(C, 1)],
        out_specs=[grp(P, W)] * 4 + [grp(C, W)] * 2 + [grp(P, W)] * 4 + [grp(P, 128)],
        out_shape=[pw_out] * 4 + [k_out] * 2 + [pw_out] * 4 + [jax.ShapeDtypeStruct((G, P, 128), F32)],
        compiler_params=_params(("parallel",), 32),
        name="s5_params",
    )(lam, dts, tile_k(b_re), tile_k(b_im), ct(c_re), ct(c_im), c_re, c_im, d_skip.reshape(G, C, 1))

    NG, J, hp = S5_LANE_GROUPS, G // S5_LANE_GROUPS, lax.Precision.HIGHEST
    order = _s5_token_order()
    place = (order[:, :, None] == np.arange(L)).astype(np.float32)
    place_rev = (order[:, :, None] == L - 1 - np.arange(L)).astype(np.float32)
    lag = order[:, None, :] - order[:, :, None] + (L - 1)
    lag_hot = (lag[..., None] == np.arange(2 * L - 1)).astype(np.float32)
    lag_tab = jnp.concatenate([kb.reshape(G, C, L, C)[:, :, :0:-1], kf.reshape(G, C, L, C)], axis=2)
    toep = jnp.einsum("xstl,jxoli->jxsito", lag_hot, lag_tab.reshape(J, NG, C, 2 * L - 1, C), precision=hp)
    toep = toep.reshape(G, W, W)

    def slots(a, hot):
        return jnp.einsum("xqk,jxpkc->jxpqc", hot, a.reshape(J, NG, P, L, C), precision=hp).reshape(G, P, W)

    t_ = lambda a: jnp.swapaxes(a, 1, 2)
    sf_re, sf_im = t_(slots(wf_re, place_rev)), t_(slots(wf_im, place_rev))
    sb_re, sb_im = t_(slots(wb_re, place)), t_(slots(wb_im, place))
    w1 = jnp.concatenate([toep, sf_re, sf_im, sf_im, sf_re, sb_re, sb_im, sb_im, sb_re], axis=2)

    def pair_rows(a):
        a = a.reshape(G // 2, 2, P, W)
        zero = jnp.zeros_like(a[:, 0])
        return jnp.concatenate([jnp.concatenate([a[:, 0], zero], axis=2),
                                jnp.concatenate([zero, a[:, 1]], axis=2)], axis=1)

    w2 = jnp.concatenate([pair_rows(slots(caf_re, place)), pair_rows(slots(caf_im, place)),
                          pair_rows(slots(cab_re, place_rev)), pair_rows(slots(cab_im, place_rev))], axis=1)
    apj = ap[:, :, :4 * S5_LEVELS].reshape(G // 2, 2, P, S5_LEVELS, 4).transpose(0, 3, 4, 1, 2)
    mult = apj.reshape(G // 2, S5_LEVELS, 4 * 2 * P)
    return w1.astype(BF16), w2.astype(BF16), mult


def _s5_token_order():
    NG = S5_LANE_GROUPS
    order = np.empty((NG, S5_CHUNK), np.int64)
    for gl in range(NG):
        for half in range(S5_CHUNK // NG):
            for blk in range(NG):
                order[gl, half * NG + blk] = half * NG + (blk - gl) % NG
    return order


def _s5_kernel(u_ref, w1_ref, w2_ref, mult_ref, y_ref, ucbuf, ybuf, *, nch, nlev):
    row = lax.broadcasted_iota(jnp.int32, (nch, 128), 0)
    lane_blk = lax.broadcasted_iota(jnp.int32, (nch, 128), 1) // S5_GROUP
    W, L, C, NG = S5_CHUNK * S5_GROUP, S5_CHUNK, S5_GROUP, S5_LANE_GROUPS

    def pick(slabs, shift):
        acc = slabs[(0 - shift) % NG]
        for blk in range(1, NG):
            acc = jnp.where(lane_blk == blk, slabs[(blk - shift) % NG], acc)
        return acc

    def shift_down(x, s):
        if s % 8 == 0:
            return jnp.concatenate([jnp.zeros((s, 128), F32), x[:nch - s]], axis=0)
        return jnp.where(row >= s, pltpu.roll(x, s, 0), 0.0)

    def shift_up(x, s):
        if s % 8 == 0:
            return jnp.concatenate([x[s:], jnp.zeros((s, 128), F32)], axis=0)
        return jnp.where(row < nch - s, pltpu.roll(x, nch - s, 0), 0.0)

    for half in range(L // NG):
        rolled = []
        for k in range(NG):
            slab = u_ref[pl.ds(half * NG + k, nch, stride=L), :]
            rolled.append(pltpu.roll(slab, k * C, 1) if k else slab)
        for gl in range(NG):
            ucbuf[gl, :, half * 128:(half + 1) * 128] = pick(rolled, gl).astype(BF16)

    first_group = lax.broadcasted_iota(jnp.int32, (nch, 128), 1) < S5_STATE

    for m in range(NG // 2):
        r0 = _dot(ucbuf[2 * m], w1_ref[2 * m])
        r1 = _dot(ucbuf[2 * m + 1], w1_ref[2 * m + 1])
        xf_re = jnp.where(first_group, r0[:, W:W + 128], r1[:, W + 128:W + 256])
        xf_im = jnp.where(first_group, r0[:, W + 128:W + 256], r1[:, W:W + 128])
        xb_re = jnp.where(first_group, r0[:, W + 256:W + 384], r1[:, W + 384:W + 512])
        xb_im = jnp.where(first_group, r0[:, W + 384:W + 512], r1[:, W + 256:W + 384])
        mult = mult_ref[m]
        for j in range(nlev):
            s = 1 << j
            a_re, a_im = mult[j:j + 1, 0:128], mult[j:j + 1, 128:256]
            s_re, s_im = shift_down(xf_re, s), shift_down(xf_im, s)
            xf_re, xf_im = xf_re + (a_re * s_re - a_im * s_im), xf_im + (a_re * s_im + a_im * s_re)
            a_re, a_im = mult[j:j + 1, 256:384], mult[j:j + 1, 384:512]
            s_re, s_im = shift_up(xb_re, s), shift_up(xb_im, s)
            xb_re, xb_im = xb_re + (a_re * s_re - a_im * s_im), xb_im + (a_re * s_im + a_im * s_re)
        xs = jnp.concatenate([shift_down(xf_re, 1), shift_down(xf_im, 1), shift_up(xb_re, 1), shift_up(xb_im, 1)],
                             axis=1).astype(BF16)
        y = _dot(xs, w2_ref[m])
        ybuf[2 * m] = r0[:, 0:W] + y[:, 0:W]
        ybuf[2 * m + 1] = r1[:, 0:W] + y[:, W:]

    for half in range(L // NG):
        for k in range(NG):
            slab = pick([ybuf[gl, :, half * 128:(half + 1) * 128] for gl in range(NG)], k)
            y_ref[pl.ds(half * NG + k, nch, stride=L), :] = pltpu.roll(slab, (NG - k) * C, 1) if k else slab


def _s5(u, w1, w2, mult):
    B, T, D = u.shape
    nch, W, NG = T // S5_CHUNK, S5_CHUNK * S5_GROUP, S5_LANE_GROUPS
    nlev = int(math.log2(nch))
    assert 1 << nlev == nch and nlev <= S5_LEVELS
    tok = pl.BlockSpec((None, T, 128), lambda b, j: (b, 0, j))
    return pl.pallas_call(
        functools.partial(_s5_kernel, nch=nch, nlev=nlev),
        grid=(B, D // 128),
        in_specs=[tok,
                  pl.BlockSpec((NG, W, 3 * W), lambda b, j: (j, 0, 0)),
                  pl.BlockSpec((NG // 2, 2 * W, 2 * W), lambda b, j: (j, 0, 0)),
                  pl.BlockSpec((NG // 2, S5_LEVELS, 2 * W), lambda b, j: (j, 0, 0))],
        out_specs=tok,
        out_shape=jax.ShapeDtypeStruct((B, T, D), F32),
        scratch_shapes=[pltpu.VMEM((NG, nch, W), BF16), pltpu.VMEM((NG, nch, W), F32)],
        compiler_params=_params(("parallel", "parallel"), 32),
        name="s5_scan",
    )(u, w1, w2, mult)


def _attn_block_positions(d):
    a, c = np.arange(ATTN_QB), np.arange(ATTN_KW)
    if d == 1:
        token = lambda row, slab: 16 * ((row % slab) // TILE_ROWS) + TILE_ROWS * (row // slab) + row % TILE_ROWS
        return token(a, ATTN_QB // 2), token(c, ATTN_KW // 2) - ATTN_R
    if d == 4:
        pos = lambda row, slab: 4 * ((row % slab) // 2) + 2 * (row // slab) + row % 2
        return pos(a, ATTN_QB // 2), pos(c, ATTN_KW // 2) - ATTN_R
    return a, c - ATTN_R


def _attn_bias_table():
    slopes = 2.0 ** (-8.0 * np.arange(1, N_HEADS + 1, dtype=np.float64) / N_HEADS)
    tab = np.empty((N_HEADS // 2, len(DILATED_CONFIGS), 4, 2, ATTN_QB, ATTN_KW), np.float32)
    for bi, (_, d) in enumerate(DILATED_CONFIGS):
        pos_q, pos_k = _attn_block_positions(d)
        rel = pos_k[None, :] - pos_q[:, None]
        for ty in range(4):
            valid = np.abs(rel) <= ATTN_R
            if ty & 1:
                valid = valid & (pos_k[None, :] >= 0)
            if ty & 2:
                valid = valid & (pos_k[None, :] < ATTN_QB)
            for h in range(N_HEADS):
                dist = (np.abs(rel) * d).astype(np.float32)
                alibi = (-np.float32(slopes[h] * LOG2_E)) * dist
                bias = np.where(valid, alibi, np.float32(NEG_BIG))
                if ty == 3:
                    bias = np.concatenate([bias[:, ATTN_R:ATTN_R + ATTN_QB],
                                           np.full((ATTN_QB, 2 * ATTN_R), NEG_BIG, np.float32)], axis=1)
                tab[h // 2, bi, ty, h % 2] = bias
    return tab.reshape(N_HEADS // 2, len(DILATED_CONFIGS) * 4, 2 * ATTN_QB, ATTN_KW)


def _attn_kernel(q_ref, k_ref, v_ref, bias_ref, o_ref, kp, vp, acc_o, acc_m, acc_l, *, T):
    half, pad = T // 2, ATTN_PAD // 2
    zeros = jnp.zeros((pad, 128), F32)
    for buf, src in ((kp, k_ref), (vp, v_ref)):
        for parity in range(2):
            buf[parity, 0:pad, :] = zeros
            buf[parity, pad + half:pad + half + pad, :] = zeros
            buf[parity, pad:pad + half, :] = src[parity]
    first_head = lax.broadcasted_iota(jnp.int32, (ATTN_QB, 128), 1) < HEAD_DIM
    ones = jnp.ones((ATTN_KW, 128), BF16)
    n_branch = len(DILATED_CONFIGS)

    def block_slabs(d, nblk, it, u):
        idx = it * ATTN_UNROLL[d] + u
        tile = lambda n: pl.multiple_of(TILE_ROWS * n, TILE_ROWS)
        if d == 1:
            blk, nq, nr = idx, ATTN_QB // 2, ATTN_R // 2
            q = [(p, pl.ds(tile(nq // TILE_ROWS * blk), nq), nq) for p in range(2)]
            k = [(p, pl.ds(pad - nr + tile(nq // TILE_ROWS * blk), 2 * nq), 2 * nq) for p in range(2)]
        elif d == 4:
            r, blk, nq, nr = u % d, idx // d, ATTN_QB // 2, ATTN_R // 2
            q = [(p, pl.ds(tile(nq // 2 * blk) + r, nq, stride=4), nq) for p in range(2)]
            k = [(p, pl.ds(pad - 4 * nr + tile(nq // 2 * blk) + r, 2 * nq, stride=4), 2 * nq) for p in range(2)]
        else:
            assert d == 2 * TILE_ROWS and ATTN_UNROLL[d] % TILE_ROWS == 0
            sub, parity, blk = u % TILE_ROWS, (idx // TILE_ROWS) % 2, idx // d
            q = [(parity, pl.ds(tile(ATTN_QB * blk) + sub, ATTN_QB, stride=TILE_ROWS), ATTN_QB)]
            back, nk = (0, ATTN_QB) if nblk == 1 else (TILE_ROWS * ATTN_R, ATTN_KW)
            k = [(parity, pl.ds(pad - back + tile(ATTN_QB * blk) + sub, nk, stride=TILE_ROWS), nk)]
        return q, k, blk

    def gather(ref, slabs):
        return jnp.concatenate([ref[p, rows, :] for p, rows, _ in slabs], axis=0)

    def scatter(ref, slabs, val):
        start = 0
        for p, rows, n in slabs:
            ref[p, rows, :] = val[start:start + n]
            start += n

    def block_softmax(bi, d, nblk, it, u):
        rows, keys, blk = block_slabs(d, nblk, it, u)
        qf = gather(q_ref, rows)
        q2 = jnp.concatenate([jnp.where(first_head, qf, 0.0), jnp.where(first_head, 0.0, qf)], axis=0).astype(BF16)
        if nblk == 1:
            bias = bias_ref[bi * 4 + 3, :, 0:ATTN_QB]
        else:
            bias = bias_ref[bi * 4 + jnp.where(blk == 0, 1, 0) + jnp.where(blk == nblk - 1, 2, 0)]
        kw = gather(kp, keys).astype(BF16)
        s = lax.dot_general(q2, kw, (((1,), (1,)), ((), ())), preferred_element_type=F32)
        s = s + bias
        m = jnp.max(s, axis=-1, keepdims=True)
        p = jnp.exp2(s - m).astype(BF16)
        ov = _dot(p, jnp.concatenate([gather(vp, keys).astype(BF16), ones[0:kw.shape[0]]], axis=1))
        o_new = jnp.where(first_head, ov[0:ATTN_QB, 0:128], ov[ATTN_QB:, 0:128])
        l_new = jnp.where(first_head, ov[0:ATTN_QB, 128:], ov[ATTN_QB:, 128:])
        m_new = jnp.where(first_head, m[0:ATTN_QB], m[ATTN_QB:])
        return rows, o_new, m_new, l_new

    for bi, (_, d) in enumerate(DILATED_CONFIGS):
        nblk = T // d // ATTN_QB

        def group(it, carry, bi=bi, d=d, nblk=nblk):
            new = [block_softmax(bi, d, nblk, it, u) for u in range(ATTN_UNROLL[d])]
            if bi == 0:
                for rows, o_new, m_new, l_new in new:
                    scatter(acc_o, rows, o_new)
                    scatter(acc_m, rows, m_new)
                    scatter(acc_l, rows, l_new)
                return carry
            old = [(gather(acc_o, rows), gather(acc_m, rows), gather(acc_l, rows)) for rows, _, _, _ in new]
            merged = []
            for (rows, o_new, m_new, l_new), (o_old, m_old, l_old) in zip(new, old):
                e = jnp.exp2(-jnp.abs(m_old - m_new))
                keep = m_old >= m_new
                a = jnp.where(keep, 1.0, e)
                b = jnp.where(keep, e, 1.0)
                merged.append((rows, a * o_old + b * o_new, jnp.maximum(m_old, m_new), a * l_old + b * l_new))
            for rows, o_tot, m_tot, l_tot in merged:
                if bi == n_branch - 1:
                    scatter(o_ref, rows, o_tot / l_tot)
                else:
                    scatter(acc_o, rows, o_tot)
                    scatter(acc_m, rows, m_tot)
                    scatter(acc_l, rows, l_tot)
            return carry

        lax.fori_loop(0, d * nblk // ATTN_UNROLL[d], group, 0)


def _attention(q, k, v, bias):
    B, _, half, _ = q.shape
    T = 2 * half
    assert T % (ATTN_QB * DILATED_CONFIGS[-1][1]) == 0
    pair = pl.BlockSpec((None, 2, half, 128), lambda b, p: (b, 0, 0, p))
    nb = bias.shape[1]
    acc = pltpu.VMEM((2, half, 128), F32)
    padded = pltpu.VMEM((2, half + ATTN_PAD, 128), F32)
    return pl.pallas_call(
        functools.partial(_attn_kernel, T=T),
        grid=(B, N_HEADS // 2),
        in_specs=[pair, pair, pair, pl.BlockSpec((None, nb, 2 * ATTN_QB, ATTN_KW), lambda b, p: (p, 0, 0, 0))],
        out_specs=pair,
        out_shape=jax.ShapeDtypeStruct((B, 2, half, D_B), F32),
        scratch_shapes=[padded, padded, acc, acc, acc],
        compiler_params=_params(("parallel", "arbitrary"), 48),
        name="dilated_attn",
    )(q, k, v, bias)


def _ab_out_kernel(x_ref, ya_ref, yb_ref, mod_ref, wglu_ref, bglu_ref, wo_ref, o_ref):
    for r0 in range(0, x_ref.shape[0], ROW_BLOCK):
        rows = slice(r0, r0 + ROW_BLOCK)
        y = ya_ref[rows, :]
        y = 0.5 * y * (1.0 + jnp.tanh(math.sqrt(2.0 / math.pi) * (y + 0.044715 * (y * y * y))))
        y = y * _sigmoid(_dot(y.astype(BF16), wglu_ref[...]) + bglu_ref[...])
        half_rows = slice(r0 // 2, (r0 + ROW_BLOCK) // 2)
        yb = _interleave_tiles(yb_ref[0, half_rows, :], yb_ref[1, half_rows, :])
        out = _dot(y.astype(BF16), wo_ref[0:D_A, :]) + _dot(yb.astype(BF16), wo_ref[D_A:, :])
        o_ref[rows, :] = x_ref[rows, :] + mod_ref[2:3, :] * out


def _ab_out(x, ya, yb, mod, w_glu, b_glu, w_out, tm=1024):
    B, T, D = x.shape
    half = pl.BlockSpec((None, tm, D_A), lambda b, i: (b, i, 0))
    split = pl.BlockSpec((None, 2, tm // 2, D_B), lambda b, i: (b, 0, i, 0))
    full = pl.BlockSpec((None, tm, D), lambda b, i: (b, i, 0))
    return pl.pallas_call(
        _ab_out_kernel,
        grid=(B, T // tm),
        in_specs=[full, half, split, pl.BlockSpec((None, 6, D), lambda b, i: (b, 0, 0)),
                  _const_spec((D_A, D_A)), _const_spec((1, D_A)), _const_spec((D, D))],
        out_specs=full,
        out_shape=jax.ShapeDtypeStruct((B, T, D), F32),
        compiler_params=_params(("parallel", "parallel"), 48),
        name="ab_out",
    )(x, ya, yb, mod, w_glu, b_glu, w_out)


def _halo_specs(tm, T, D):
    nh = T // HALO
    per = tm // HALO
    main = pl.BlockSpec((None, tm, D), lambda b, i: (b, i, 0))
    prev = pl.BlockSpec((None, HALO, D), lambda b, i: (b, jnp.maximum(i * per - 1, 0), 0))
    nxt = pl.BlockSpec((None, HALO, D), lambda b, i: (b, jnp.minimum((i + 1) * per, nh - 1), 0))
    return main, prev, nxt


def _ffn_kernel(x_ref, xp_ref, xn_ref, mod_ref, g_ref, wg_ref, wu_ref, cw_ref, cb_ref, wd_ref, o_ref, gbuf, abuf, *, tm):
    x = x_ref[...]
    g, shift, scale = g_ref[...], mod_ref[3:4, :], mod_ref[4:5, :]
    has_prev = jnp.where(pl.program_id(1) > 0, 1.0, 0.0)
    has_next = jnp.where(pl.program_id(1) < pl.num_programs(1) - 1, 1.0, 0.0)
    h = _norm_mod(x, g, shift, scale)
    he = jnp.concatenate([_norm_mod(xp_ref[...], g, shift, scale) * has_prev, h,
                          _norm_mod(xn_ref[...], g, shift, scale) * has_next], axis=0).astype(BF16)
    h = he[HALO:HALO + tm]
    for c in range(D_FF // FFN_CHUNK):
        cols = slice(c * FFN_CHUNK, (c + 1) * FFN_CHUNK)
        gbuf[:, cols] = _dot(he, wg_ref[:, cols])
        gate = (gbuf[pl.ds(HALO - 1, tm), cols] * cw_ref[0:1, cols] + gbuf[pl.ds(HALO, tm), cols] * cw_ref[1:2, cols]
                + gbuf[pl.ds(HALO + 1, tm), cols] * cw_ref[2:3, cols] + cb_ref[:, cols])
        up = _dot(h, wu_ref[:, cols])
        abuf[:, cols] = ((gate * _sigmoid(gate)) * up).astype(BF16)
    o_ref[...] = x + mod_ref[5:6, :] * _dot(abuf[...], wd_ref[...])


def _ffn(x, mod, g, w_gate, w_up, conv_w, conv_b, w_down, tm=512):
    B, T, D = x.shape
    main, prev, nxt = _halo_specs(tm, T, D)
    return pl.pallas_call(
        functools.partial(_ffn_kernel, tm=tm),
        grid=(B, T // tm),
        in_specs=[main, prev, nxt, pl.BlockSpec((None, 6, D), lambda b, i: (b, 0, 0)), _const_spec((1, D)),
                  _const_spec((D, D_FF)), _const_spec((D, D_FF)), _const_spec((3, D_FF)), _const_spec((1, D_FF)),
                  _const_spec((D_FF, D))],
        out_specs=main,
        out_shape=jax.ShapeDtypeStruct((B, T, D), F32),
        scratch_shapes=[pltpu.VMEM((tm + 2 * HALO, D_FF), F32), pltpu.VMEM((tm, D_FF), BF16)],
        compiler_params=_params(("parallel", "arbitrary"), 56),
        name="conv_ffn",
    )(x, x, x, mod, g, w_gate, w_up, conv_w, conv_b, w_down)


def _cd_in_kernel(x_ref, xp_ref, xn_ref, mod_ref, g_ref, w_ref, cbd_ref, sbd_ref, sw_ref, ab_ref, yd_ref, cbuf, *, tm):
    g, shift, scale = g_ref[...], mod_ref[0:1, :], mod_ref[1:2, :]
    has_prev = jnp.where(pl.program_id(1) > 0, 1.0, 0.0)
    has_next = jnp.where(pl.program_id(1) < pl.num_programs(1) - 1, 1.0, 0.0)
    he = jnp.concatenate([_norm_mod(xp_ref[...], g, shift, scale) * has_prev, _norm_mod(x_ref[...], g, shift, scale),
                          _norm_mod(xn_ref[...], g, shift, scale) * has_next], axis=0).astype(BF16)
    h = he[HALO:HALO + tm]
    for c0 in range(0, D_D, CD_CHUNK):
        cols = slice(c0, c0 + CD_CHUNK)
        hs = _dot(he, w_ref[:, D_C + c0:D_C + c0 + CD_CHUNK])
        gc = _dot(he, w_ref[:, D_C + 2 * D_D + c0:D_C + 2 * D_D + c0 + CD_CHUNK])
        cbuf[:, cols] = gc * hs
        conv = (cbuf[pl.ds(HALO - 1, tm), cols] * sw_ref[0:1, cols] + cbuf[pl.ds(HALO, tm), cols] * sw_ref[1:2, cols]
                + cbuf[pl.ds(HALO + 1, tm), cols] * sw_ref[2:3, cols])
        yd_ref[:, cols] = _dot(h, w_ref[:, D_C + D_D + c0:D_C + D_D + c0 + CD_CHUNK]) * conv
    uc = _dot(h, w_ref[:, 0:D_C]).astype(BF16)
    ab_ref[0] = _block_diag_dot(uc, cbd_ref).astype(BF16)
    ab_ref[1] = _block_diag_dot(uc, sbd_ref).astype(BF16)


def _cd_in(x, mod, g, w_in, cbd, sbd, sconv_w, tm=1024):
    B, T, D = x.shape
    main, prev, nxt = _halo_specs(tm, T, D)
    return pl.pallas_call(
        functools.partial(_cd_in_kernel, tm=tm),
        grid=(B, T // tm),
        in_specs=[main, prev, nxt, pl.BlockSpec((None, 6, D), lambda b, i: (b, 0, 0)), _const_spec((1, D)),
                  _const_spec((D, w_in.shape[1])), _const_spec((D_C, D_C)), _const_spec((D_C, D_C)),
                  _const_spec((3, D_D))],
        out_specs=[pl.BlockSpec((None, 2, tm, D_C), lambda b, i: (b, 0, i, 0)),
                   pl.BlockSpec((None, tm, D_D), lambda b, i: (b, i, 0))],
        out_shape=[jax.ShapeDtypeStruct((B, 2, T, D_C), BF16), jax.ShapeDtypeStruct((B, T, D_D), F32)],
        scratch_shapes=[pltpu.VMEM((tm + 2 * HALO, D_D), F32)],
        compiler_params=_params(("parallel", "arbitrary"), 48),
        name="cd_in",
    )(x, x, x, mod, g, w_in, cbd, sbd, sconv_w)


def _cd_out_kernel(dft_ref, ab_ref, yd_ref, x_ref, mod_ref, wo_ref, o_ref, fold, *, T, tf):
    half = T // 2

    @pl.when(pl.program_id(1) == 0)
    def _():
        r = lax.broadcasted_iota(jnp.int32, (tf, tf + HALO), 0)
        c = lax.broadcasted_iota(jnp.int32, (tf, tf + HALO), 1)
        mirror = jnp.where(c == tf - r, 1.0, 0.0).astype(BF16)
        first_row = lax.broadcasted_iota(jnp.int32, (tf, D_C), 0) == 0
        for part, sign in ((0, 1.0), (1, -1.0)):
            for s0 in range(0, half, tf):
                nxt = ab_ref[part, T - s0:T - s0 + HALO, :] if s0 else jnp.zeros((HALO, D_C), BF16)
                src = jnp.concatenate([ab_ref[part, T - s0 - tf:T - s0, :], nxt], axis=0)
                folded = ab_ref[part, s0:s0 + tf, :].astype(F32) + sign * _dot(mirror, src)
                if part == 1 and s0 == 0:
                    folded = jnp.where(first_row, ab_ref[0, half:half + HALO, :].astype(F32)[0:1, :], folded)
                fold[part * half + s0:part * half + s0 + tf, :] = folded.astype(BF16)

    for r0 in range(0, x_ref.shape[0], ROW_BLOCK):
        rows = slice(r0, r0 + ROW_BLOCK)
        yc = _dot(dft_ref[rows, :], fold[...])
        out = _dot(yc.astype(BF16), wo_ref[0:D_C, :]) + _dot(yd_ref[rows, :].astype(BF16), wo_ref[D_C:, :])
        o_ref[rows, :] = x_ref[rows, :] + mod_ref[2:3, :] * out


def _cd_out(dft, ab, yd, x, mod, w_out, tm=512):
    B, T, D = x.shape
    return pl.pallas_call(
        functools.partial(_cd_out_kernel, T=T, tf=256),
        grid=(B, T // tm),
        in_specs=[pl.BlockSpec((tm, T), lambda b, i: (i, 0)),
                  pl.BlockSpec((None, 2, T, D_C), lambda b, i: (b, 0, 0, 0)),
                  pl.BlockSpec((None, tm, D_D), lambda b, i: (b, i, 0)),
                  pl.BlockSpec((None, tm, D), lambda b, i: (b, i, 0)),
                  pl.BlockSpec((None, 6, D), lambda b, i: (b, 0, 0)), _const_spec((D, D))],
        out_specs=pl.BlockSpec((None, tm, D), lambda b, i: (b, i, 0)),
        out_shape=jax.ShapeDtypeStruct((B, T, D), F32),
        scratch_shapes=[pltpu.VMEM((T, D_C), BF16)],
        compiler_params=_params(("parallel", "arbitrary"), 56),
        name="cd_out",
    )(dft, ab, yd, x, mod, w_out)


def _seq_dft_matrix(T):
    rows = 64
    s = jnp.arange(T // 2, dtype=jnp.int32)[None, :]
    angle = lambda t: ((t[:, None] * s) % T).astype(F32) * (2.0 * math.pi / T)
    ang_a = angle(jnp.arange(T // rows, dtype=jnp.int32) * rows)[:, None, :]
    ang_b = angle(jnp.arange(rows, dtype=jnp.int32))[None, :, :]
    scale = 1.0 / math.sqrt(T)
    cos_a, sin_a, cos_b, sin_b = jnp.cos(ang_a) * scale, jnp.sin(ang_a) * scale, jnp.cos(ang_b), jnp.sin(ang_b)
    cos_ts = (cos_a * cos_b - sin_a * sin_b).reshape(T, T // 2)
    sin_ts = (sin_a * cos_b + cos_a * sin_b).reshape(T, T // 2)
    nyquist = jnp.where(jnp.arange(T) % 2 == 0, scale, -scale)[:, None]
    return jnp.concatenate([cos_ts, jnp.where(s == 0, nyquist, -sin_ts)], axis=1).astype(BF16)


def _channel_dft_matrices():
    c = np.arange(D_C)
    ang = 2.0 * np.pi * ((c[:, None] % FNET_GROUP_DIM) * (c[None, :] % FNET_GROUP_DIM) % FNET_GROUP_DIM) / FNET_GROUP_DIM
    same = (c[:, None] // FNET_GROUP_DIM) == (c[None, :] // FNET_GROUP_DIM)
    scale = 1.0 / math.sqrt(FNET_GROUP_DIM)
    cbd = np.where(same, np.cos(ang) * scale, 0.0).astype(np.float32)
    sbd = np.where(same, np.sin(ang) * scale, 0.0).astype(np.float32)
    return jnp.asarray(cbd).astype(BF16), jnp.asarray(sbd).astype(BF16)


def _trunk(x, mods, wts):
    B, T, D = x.shape
    nch = T // S5_CHUNK
    mod = mods[0]
    u, q, k, v = _ab_in(x, mod, wts["norm_mix_g"][0], wts["ab_w_in"], wts["hsum"], wts["q_g"], wts["k_g"])
    ya = _s5(u, wts["s5_w1"], wts["s5_w2"], wts["s5_mult"])
    yb = _attention(q, k, v, wts["attn_bias"])
    x = _ab_out(x, ya, yb, mod, wts["s5_w_glu"], wts["s5_b_glu"], wts["ab_w_out"])
    x = _ffn(x, mod, wts["norm_ffn_g"][0], *wts["ffn"][0])
    mod = mods[1]
    ab, yd = _cd_in(x, mod, wts["norm_mix_g"][1], wts["cd_w_in"], wts["cbd"], wts["sbd"], wts["sconv_w"])
    x = _cd_out(_seq_dft_matrix(T), ab, yd, x, mod, wts["cd_w_out"])
    x = _ffn(x, mod, wts["norm_ffn_g"][1], *wts["ffn"][1])
    return x


def kernel(x_prompt, x_sample, c_prompt, c_sample, ada_w, ada_b, norm_mix_g, norm_ffn_g, ffn_w_gate, ffn_w_up, ffn_conv_w, ffn_conv_b, ffn_w_down, ab_w_in, ab_w_out, s5_lam_re_f, s5_lam_im_f, s5_log_dt_f, s5_lam_re_b, s5_lam_im_b, s5_log_dt_b, s5_b_re, s5_b_im, s5_c_re, s5_c_im, s5_d, s5_w_glu, s5_b_glu, q_norm_g, k_norm_g, cd_w_in, cd_w_out, sconv_w):
    depth = ada_w.shape[0]
    assert depth == 2 and ab_w_in.shape[0] == 1 and cd_w_in.shape[0] == 1
    bp = x_prompt.shape[0]
    mod_all = _ada(jnp.concatenate([c_prompt, c_sample], axis=0), ada_w, ada_b)
    s5_w1, s5_w2, s5_mult = _s5_tables(s5_lam_re_f[0], s5_lam_im_f[0], s5_log_dt_f[0], s5_lam_re_b[0],
                                       s5_lam_im_b[0], s5_log_dt_b[0], s5_b_re[0], s5_b_im[0],
                                       s5_c_re[0], s5_c_im[0], s5_d[0])
    head = np.arange(D_B) // HEAD_DIM
    cbd, sbd = _channel_dft_matrices()
    wts = dict(
        norm_mix_g=norm_mix_g.reshape(depth, 1, D_MODEL), norm_ffn_g=norm_ffn_g.reshape(depth, 1, D_MODEL),
        ab_w_in=ab_w_in[0].astype(BF16), ab_w_out=ab_w_out[0].astype(BF16),
        hsum=jnp.asarray(head[:, None] == head[None, :], BF16),
        q_g=jnp.tile(q_norm_g[0], N_HEADS).reshape(1, D_B), k_g=jnp.tile(k_norm_g[0], N_HEADS).reshape(1, D_B),
        s5_w1=s5_w1, s5_w2=s5_w2, s5_mult=s5_mult,
        s5_w_glu=s5_w_glu[0].astype(BF16), s5_b_glu=s5_b_glu[0].reshape(1, D_A),
        attn_bias=jnp.asarray(_attn_bias_table()),
        ffn=[(ffn_w_gate[l].astype(BF16), ffn_w_up[l].astype(BF16), ffn_conv_w[l], ffn_conv_b[l].reshape(1, D_FF),
              ffn_w_down[l].astype(BF16)) for l in range(depth)],
        cd_w_in=cd_w_in[0].astype(BF16), cd_w_out=cd_w_out[0].astype(BF16), cbd=cbd, sbd=sbd, sconv_w=sconv_w[0],
    )
    outs = []
    for x, rows in ((x_prompt, slice(0, bp)), (x_sample, slice(bp, None))):
        mods = [mod_all[l, rows].reshape(x.shape[0], 6, D_MODEL) for l in range(depth)]
        outs.append(_trunk(x, mods, wts))
    return tuple(outs)
```
